```python
import math
import jax
import jax.numpy as jnp
from jax import lax
import numpy as np

D_MODEL = 1024
BATCH = 1
SEQ = 16384
DEPTH = 2
DEC_BATCH = 8
DEC_SEQ = 16
PAST_LEN = 1024

CHUNK = 64
Q_BLOCK = 128
HEAD_DIM = 64
ROT_DIM = HEAD_DIM // 4
ROPE_THETA = 500000.0
A_HEADS = D_MODEL // (4 * HEAD_DIM)
B_HEADS = D_MODEL // (2 * HEAD_DIM)
C_HEADS = D_MODEL // HEAD_DIM
A_W = 2 * A_HEADS * HEAD_DIM
B_W = B_HEADS * HEAD_DIM
C_PAST_CHUNKS = 8
C_BAND = (C_PAST_CHUNKS + 1) * CHUNK
REL_CLIP = 128
MEM_LEN = 256
MEM_HEADS = 4
MEM_HEAD_DIM = D_MODEL // MEM_HEADS
D_FF = 2816
N_EVEN = (DEPTH + 1) // 2
N_ODD = DEPTH // 2
RMS_EPS = 1e-6
NEG = -1e30

kernel_name = 'hybrid_streaming_encoder_step'


def rms_norm(x, g):
    xf = x.astype(jnp.float32)
    y = xf * lax.rsqrt(jnp.mean(xf * xf, axis=-1, keepdims=True) + RMS_EPS)
    return (y * g.astype(jnp.float32)).astype(x.dtype)


def partial_rope(x, pos):
    half = ROT_DIM // 2
    inv_freq = ROPE_THETA ** (-2.0 * jnp.arange(half, dtype=jnp.float32) / ROT_DIM)
    ang = pos.astype(jnp.float32)[:, None] * inv_freq[None, :]
    cos = jnp.cos(ang)[None, :, None, :]
    sin = jnp.sin(ang)[None, :, None, :]
    xr = x[..., :ROT_DIM].astype(jnp.float32)
    x1, x2 = xr[..., :half], xr[..., half:]
    rot = jnp.concatenate([x1 * cos - x2 * sin, x2 * cos + x1 * sin], axis=-1).astype(x.dtype)
    return jnp.concatenate([rot, x[..., ROT_DIM:]], axis=-1)


def chunk_visible(qpos, kpos):
    return (kpos[None, :] // CHUNK) <= (qpos[:, None] // CHUNK)


def sweep_blocks(fn, q, qpos):
    B, T = q.shape[0], q.shape[1]
    nb = T // Q_BLOCK
    qb = jnp.moveaxis(q.reshape((B, nb, Q_BLOCK) + q.shape[2:]), 1, 0)
    pb = qpos.reshape(nb, Q_BLOCK)
    out = jnp.moveaxis(lax.map(lambda a: fn(a[0], a[1]), (qb, pb)), 0, 1)
    return out.reshape((B, T) + out.shape[3:])


def diff_attention(q, k, v, lam, qpos, kpos):
    B, Tq = q.shape[0], q.shape[1]
    s = jnp.einsum('bqhd,bkhd->bhqk', q, k).astype(jnp.float32) * (HEAD_DIM ** -0.5)
    s = jnp.where(chunk_visible(qpos, kpos), s, NEG)
    p = jax.nn.softmax(s, axis=-1).reshape(B, A_HEADS, 2, Tq, -1)
    a = p[:, :, 0] - lam * p[:, :, 1]
    return jnp.einsum('bhqk,bkhe->bqhe', a.astype(v.dtype), v)


def stick_breaking(q, k, v, qpos, kpos):
    z = jnp.einsum('bqhd,bkhd->bhqk', q, k).astype(jnp.float32) * (HEAD_DIM ** -0.5)
    causal = kpos[None, :] < qpos[:, None]
    log_1m = jnp.where(causal, jax.nn.log_sigmoid(-z), 0.0)
    after = lax.cumsum(log_1m, axis=3, reverse=True) - log_1m
    w = jnp.where(causal, jnp.exp(jax.nn.log_sigmoid(z) + after), 0.0)
    return jnp.einsum('bhqk,bkhd->bqhd', w.astype(v.dtype), v)


def band_attention(q, k, v, bias_table, qpos, kpos):
    s = jnp.einsum('bqhd,bkhd->bhqk', q, k).astype(jnp.float32) * (HEAD_DIM ** -0.5)
    rel = jnp.clip(qpos[:, None] - kpos[None, :], -REL_CLIP, REL_CLIP) + REL_CLIP
    s = s + bias_table.astype(jnp.float32)[:, rel][None]
    qc = qpos[:, None] // CHUNK
    kc = kpos[None, :] // CHUNK
    vis = (kpos[None, :] >= 0) & (kc <= qc) & (qc - kc <= C_PAST_CHUNKS)
    p = jax.nn.softmax(jnp.where(vis, s, NEG), axis=-1)
    return jnp.einsum('bhqk,bkhd->bqhd', p.astype(v.dtype), v)


def mixer_ab(h, pos, past, w_in, w_out, gq, gk, lq1, lk1, lq2, lk2, subln_g, lam_init):
    B, T, _ = h.shape
    proj = h @ w_in
    aq, ak, av, bq, bk, bv = jnp.split(proj, 6, axis=-1)
    aq = partial_rope(rms_norm(aq.reshape(B, T, 2 * A_HEADS, HEAD_DIM), gq), pos)
    ak = partial_rope(rms_norm(ak.reshape(B, T, 2 * A_HEADS, HEAD_DIM), gk), pos)
    av = av.reshape(B, T, A_HEADS, 2 * HEAD_DIM)
    bq = bq.reshape(B, T, B_HEADS, HEAD_DIM)
    bk = bk.reshape(B, T, B_HEADS, HEAD_DIM)
    bv = bv.reshape(B, T, B_HEADS, HEAD_DIM)
    lam = (jnp.exp(jnp.sum(lq1.astype(jnp.float32) * lk1.astype(jnp.float32)))
           - jnp.exp(jnp.sum(lq2.astype(jnp.float32) * lk2.astype(jnp.float32))) + lam_init)
    if past is None:
        kpos = pos
        a_out = sweep_blocks(lambda qb, pb: diff_attention(qb, ak, av, lam, pb, kpos), aq, pos)
        b_out = sweep_blocks(lambda qb, pb: stick_breaking(qb, bk, bv, pb, kpos), bq, pos)
    else:
        pka, pva, pkb, pvb = past
        kpos = jnp.concatenate([jnp.arange(pka.shape[1]), pos])
        a_out = diff_attention(aq, jnp.concatenate([pka, ak], 1), jnp.concatenate([pva, av], 1), lam, pos, kpos)
        b_out = stick_breaking(bq, jnp.concatenate([pkb, bk], 1), jnp.concatenate([pvb, bv], 1), pos, kpos)
    a_out = rms_norm(a_out, subln_g) * (1.0 - lam_init)
    merged = jnp.concatenate([a_out.reshape(B, T, A_W), b_out.reshape(B, T, B_W)], axis=-1)
    return merged @ w_out, (ak, av, bk, bv)


def mixer_c(h, pos, past, w_in, w_out, gq, gk, bias_table):
    B, T, _ = h.shape
    q, k, v = jnp.split(h @ w_in, 3, axis=-1)
    q = rms_norm(q.reshape(B, T, C_HEADS, HEAD_DIM), gq)
    k = rms_norm(k.reshape(B, T, C_HEADS, HEAD_DIM), gk)
    v = v.reshape(B, T, C_HEADS, HEAD_DIM)
    if past is None:
        nc = T // CHUNK
        pad = C_PAST_CHUNKS * CHUNK
        kpad = jnp.pad(k, ((0, 0), (pad, 0), (0, 0), (0, 0)))
        vpad = jnp.pad(v, ((0, 0), (pad, 0), (0, 0), (0, 0)))
        qch = jnp.moveaxis(q.reshape(B, nc, CHUNK, C_HEADS, HEAD_DIM), 1, 0)

        def one_chunk(args):
            qc, c = args
            start = c * CHUNK
            kb = lax.dynamic_slice_in_dim(kpad, start, C_BAND, axis=1)
            vb = lax.dynamic_slice_in_dim(vpad, start, C_BAND, axis=1)
            qpos = start + jnp.arange(CHUNK)
            kpos = start - pad + jnp.arange(C_BAND)
            return band_attention(qc, kb, vb, bias_table, qpos, kpos)

        o = jnp.moveaxis(lax.map(one_chunk, (qch, jnp.arange(nc))), 0, 1).reshape(B, T, C_HEADS, HEAD_DIM)
        keep = min(pad, T)
        new_k, new_v = k[:, T - keep:], v[:, T - keep:]
    else:
        pk, pv = past
        pc = pk.shape[1]
        kall = jnp.concatenate([pk, k], 1)
        vall = jnp.concatenate([pv, v], 1)
        kpos = jnp.concatenate([pos[0] - pc + jnp.arange(pc), pos])
        o = band_attention(q, kall, vall, bias_table, pos, kpos)
        new_k, new_v = kall[:, -pc:], vall[:, -pc:]
    return o.reshape(B, T, D_MODEL) @ w_out, (new_k, new_v)


def memory_kv(mem, g_m, wk, wv, gk):
    B, M, _ = mem.shape
    m = rms_norm(mem, g_m)
    k = rms_norm((m @ wk).reshape(B, M, MEM_HEADS, MEM_HEAD_DIM), gk)
    v = (m @ wv).reshape(B, M, MEM_HEADS, MEM_HEAD_DIM)
    return k, v


def memory_attention(h, mk, mv, wq, wo, gq):
    B, T, _ = h.shape
    q = rms_norm((h @ wq).reshape(B, T, MEM_HEADS, MEM_HEAD_DIM), gq)
    s = jnp.einsum('bqhd,bkhd->bhqk', q, mk).astype(jnp.float32) * (MEM_HEAD_DIM ** -0.5)
    p = jax.nn.softmax(s, axis=-1)
    o = jnp.einsum('bhqk,bkhd->bqhd', p.astype(mv.dtype), mv).reshape(B, T, D_MODEL)
    return o @ wo


def swiglu(h, wg, wu, wd):
    return (jax.nn.silu(h @ wg) * (h @ wu)) @ wd


def lambda_init(layer):
    return 0.8 - 0.6 * math.exp(-0.3 * layer)


def setup_inputs(seed: int = 0) -> dict:
    key = jax.random.key(seed)
    ks = iter(jax.random.split(key, 64))

    def nrm(shape, scale=1.0):
        return scale * jax.random.normal(next(ks), shape, jnp.float32)

    def gain(shape):
        return 1.0 + 0.01 * jax.random.normal(next(ks), shape, jnp.float32)

    L, NE, NO, D, F = DEPTH, N_EVEN, N_ODD, D_MODEL, D_FF
    c_cache = min(C_PAST_CHUNKS * CHUNK, PAST_LEN)
    sd, sf = D ** -0.5, F ** -0.5
    return {
        'x_prompt': nrm((BATCH, SEQ, D)),
        'x_sample': nrm((DEC_BATCH, DEC_SEQ, D)),
        'cache_a_k': nrm((NE, DEC_BATCH, PAST_LEN, 2 * A_HEADS, HEAD_DIM)),
        'cache_a_v': nrm((NE, DEC_BATCH, PAST_LEN, A_HEADS, 2 * HEAD_DIM)),
        'cache_b_k': nrm((NE, DEC_BATCH, PAST_LEN, B_HEADS, HEAD_DIM)),
        'cache_b_v': nrm((NE, DEC_BATCH, PAST_LEN, B_HEADS, HEAD_DIM)),
        'cache_c_k': nrm((NO, DEC_BATCH, c_cache, C_HEADS, HEAD_DIM)),
        'cache_c_v': nrm((NO, DEC_BATCH, c_cache, C_HEADS, HEAD_DIM)),
        'cache_mem_k': nrm((L, DEC_BATCH, MEM_LEN, MEM_HEADS, MEM_HEAD_DIM)),
        'cache_mem_v': nrm((L, DEC_BATCH, MEM_LEN, MEM_HEADS, MEM_HEAD_DIM)),
        'mem_prompt': nrm((BATCH, MEM_LEN, D)),
        'ffn1_g': gain((L, D)),
        'ffn1_wg': nrm((L, D, F), sd),
        'ffn1_wu': nrm((L, D, F), sd),
        'ffn1_wd': nrm((L, F, D), sf),
        'ffn2_g': gain((L, D)),
        'ffn2_wg': nrm((L, D, F), sd),
        'ffn2_wu': nrm((L, D, F), sd),
        'ffn2_wd': nrm((L, F, D), sf),
        'mix_g': gain((L, D)),
        'ab_w_in': nrm((NE, D, 3 * A_W + 3 * B_W), sd),
        'ab_w_out': nrm((NE, A_W + B_W, D), (A_W + B_W) ** -0.5),
        'a_gq': gain((NE, HEAD_DIM)),
        'a_gk': gain((NE, HEAD_DIM)),
        'a_lq1': nrm((NE, HEAD_DIM), 0.1),
        'a_lk1': nrm((NE, HEAD_DIM), 0.1),
        'a_lq2': nrm((NE, HEAD_DIM), 0.1),
        'a_lk2': nrm((NE, HEAD_DIM), 0.1),
        'a_subln_g': gain((NE, 2 * HEAD_DIM)),
        'c_w_in': nrm((NO, D, 3 * D), sd),
        'c_w_out': nrm((NO, D, D), sd),
        'c_gq': gain((NO, HEAD_DIM)),
        'c_gk': gain((NO, HEAD_DIM)),
        'c_bias': nrm((NO, C_HEADS, 2 * REL_CLIP + 1), 0.2),
        'mem_g_x': gain((L, D)),
        'mem_g_m': gain((L, D)),
        'mem_wq': nrm((L, D, D), sd),
        'mem_wk': nrm((L, D, D), sd),
        'mem_wv': nrm((L, D, D), sd),
        'mem_wo': nrm((L, D, D), sd),
        'mem_gq': gain((L, MEM_HEAD_DIM)),
        'mem_gk': gain((L, MEM_HEAD_DIM)),
    }


def reference(x_prompt, x_sample, cache_a_k, cache_a_v, cache_b_k, cache_b_v, cache_c_k, cache_c_v,
              cache_mem_k, cache_mem_v, mem_prompt,
              ffn1_g, ffn1_wg, ffn1_wu, ffn1_wd, ffn2_g, ffn2_wg, ffn2_wu, ffn2_wd, mix_g,
              ab_w_in, ab_w_out, a_gq, a_gk, a_lq1, a_lk1, a_lq2, a_lk2, a_subln_g,
              c_w_in, c_w_out, c_gq, c_gk, c_bias,
              mem_g_x, mem_g_m, mem_wq, mem_wk, mem_wv, mem_wo, mem_gq, mem_gk):

    def trunk(x, pos, mem_k, mem_v, past_ab, past_c):
        new_ab, new_c = [], []
        for l in range(DEPTH):
            x = x + 0.5 * swiglu(rms_norm(x, ffn1_g[l]), ffn1_wg[l], ffn1_wu[l], ffn1_wd[l])
            h = rms_norm(x, mix_g[l])
            i = l // 2
            if l % 2 == 0:
                past = None if past_ab is None else (past_ab[0][i], past_ab[1][i], past_ab[2][i], past_ab[3][i])
                out, rows = mixer_ab(h, pos, past, ab_w_in[i], ab_w_out[i], a_gq[i], a_gk[i],
                                     a_lq1[i], a_lk1[i], a_lq2[i], a_lk2[i], a_subln_g[i], lambda_init(l))
                new_ab.append(rows)
            else:
                past = None if past_c is None else (past_c[0][i], past_c[1][i])
                out, rows = mixer_c(h, pos, past, c_w_in[i], c_w_out[i], c_gq[i], c_gk[i], c_bias[i])
                new_c.append(rows)
            x = x + out
            x = x + memory_attention(rms_norm(x, mem_g_x[l]), mem_k[l], mem_v[l], mem_wq[l], mem_wo[l], mem_gq[l])
            x = x + 0.5 * swiglu(rms_norm(x, ffn2_g[l]), ffn2_wg[l], ffn2_wu[l], ffn2_wd[l])
        return x, new_ab, new_c

    pos_p = jnp.arange(x_prompt.shape[1])
    mem_p = [memory_kv(mem_prompt, mem_g_m[l], mem_wk[l], mem_wv[l], mem_gk[l]) for l in range(DEPTH)]
    mem_k_list = [m[0] for m in mem_p]
    mem_v_list = [m[1] for m in mem_p]
    y_prompt, ab_p, c_p = trunk(x_prompt, pos_p, mem_k_list, mem_v_list, None, None)

    pos_s = PAST_LEN + jnp.arange(x_sample.shape[1])
    y_sample, ab_s, c_s = trunk(x_sample, pos_s, cache_mem_k, cache_mem_v,
                                (cache_a_k, cache_a_v, cache_b_k, cache_b_v), (cache_c_k, cache_c_v))

    a_k_p = jnp.stack([r[0] for r in ab_p])
    a_v_p = jnp.stack([r[1] for r in ab_p])
    b_k_p = jnp.stack([r[2] for r in ab_p])
    b_v_p = jnp.stack([r[3] for r in ab_p])
    c_k_p = jnp.stack([r[0] for r in c_p])
    c_v_p = jnp.stack([r[1] for r in c_p])
    mem_k_p = jnp.stack(mem_k_list)
    mem_v_p = jnp.stack(mem_v_list)
    a_k_s = jnp.stack([r[0] for r in ab_s])
    a_v_s = jnp.stack([r[1] for r in ab_s])
    b_k_s = jnp.stack([r[2] for r in ab_s])
    b_v_s = jnp.stack([r[3] for r in ab_s])
    c_k_s = jnp.stack([r[0] for r in c_s])
    c_v_s = jnp.stack([r[1] for r in c_s])
    return (y_prompt, y_sample, a_k_p, a_v_p, b_k_p, b_v_p, c_k_p, c_v_p, mem_k_p, mem_v_p,
            a_k_s, a_v_s, b_k_s, b_v_s, c_k_s, c_v_s)
```

```python
import functools
import math

import jax
import jax.numpy as jnp
import numpy as np
from jax import lax
from jax.experimental import pallas as pl
from jax.experimental.pallas import tpu as pltpu

D_MODEL = 1024
CHUNK = 64
HEAD_DIM = 64
ROT_DIM = HEAD_DIM // 4
ROPE_THETA = 500000.0
C_PAST_CHUNKS = 8
REL_CLIP = 128
MEM_HEADS = 4
MEM_HEAD_DIM = D_MODEL // MEM_HEADS
RMS_EPS = 1e-6
NEG = -1e30
LANES = 128
TILE_N = 512
EXP_UNDERFLOW = -104.0
VMEM_LIMIT = 56 * 1024 * 1024

BF16 = jnp.bfloat16
F32 = jnp.float32


def _dot(a, b):
    return jnp.dot(a, b, preferred_element_type=F32)


def _dot_nt(a, b):
    return lax.dot_general(a, b, (((1,), (1,)), ((), ())), preferred_element_type=F32)


def _rms(x, g):
    return x * lax.rsqrt(jnp.mean(x * x, axis=-1, keepdims=True) + RMS_EPS) * g


def _split_bf16(x):
    hi = x.astype(BF16)
    lo = (x - hi.astype(F32)).astype(BF16)
    return hi, lo


def _params(*sem):
    return pltpu.CompilerParams(dimension_semantics=sem, vmem_limit_bytes=VMEM_LIMIT)


def _ffn_kernel(x_ref, g_ref, wg_ref, wu_ref, wd_ref, o_ref, h_ref, acc_ref):
    f = pl.program_id(1)

    @pl.when(f == 0)
    def _():
        h_ref[...] = _rms(x_ref[...], g_ref[...]).astype(BF16)
        acc_ref[...] = jnp.zeros_like(acc_ref)

    h = h_ref[...]
    a = _dot(h, wg_ref[...])
    u = _dot(h, wu_ref[...])
    act = a * (1.0 / (1.0 + jnp.exp(-a))) * u
    acc_ref[...] += _dot(act.astype(BF16), wd_ref[...])

    @pl.when(f == pl.num_programs(1) - 1)
    def _():
        o_ref[...] = x_ref[...] + 0.5 * acc_ref[...]


def _ffn(x, g, wg, wu, wd):
    rows, d = x.shape
    ff = wg.shape[1]
    tm = min(512, rows)
    tf = ff // 2 if (ff // 2) % LANES == 0 else ff
    return pl.pallas_call(
        _ffn_kernel,
        grid=(rows // tm, ff // tf),
        in_specs=[
            pl.BlockSpec((tm, d), lambda i, f: (i, 0)),
            pl.BlockSpec((1, d), lambda i, f: (0, 0)),
            pl.BlockSpec((d, tf), lambda i, f: (0, f)),
            pl.BlockSpec((d, tf), lambda i, f: (0, f)),
            pl.BlockSpec((tf, d), lambda i, f: (f, 0)),
        ],
        out_specs=pl.BlockSpec((tm, d), lambda i, f: (i, 0)),
        out_shape=jax.ShapeDtypeStruct((rows, d), F32),
        scratch_shapes=[pltpu.VMEM((tm, d), BF16), pltpu.VMEM((tm, d), F32)],
        compiler_params=_params("parallel", "arbitrary"),
    )(x, g.reshape(1, d), wg, wu, wd)


def _proj_kernel(*refs, n_norm, rope):
    if rope:
        x_ref, g_ref, w_ref, gain_ref, seg_ref, cos_ref, sa_ref, sb_ref, o_ref, h_ref = refs
    else:
        x_ref, g_ref, w_ref, gain_ref, seg_ref, o_ref, h_ref = refs
    n = pl.program_id(1)

    @pl.when(n == 0)
    def _():
        h_ref[...] = _rms(x_ref[...], g_ref[...]).astype(BF16)

    y = _dot(h_ref[...], w_ref[...])

    @pl.when(n < n_norm)
    def _():
        hi, lo = _split_bf16(y * y)
        ms = _dot(hi, seg_ref[...]) + _dot(lo, seg_ref[...])
        yn = y * lax.rsqrt(ms + RMS_EPS) * gain_ref[...]
        if rope:
            cos, sa, sb = cos_ref[...], sa_ref[...], sb_ref[...]
            half = ROT_DIM // 2
            for c in range(TILE_N // LANES):
                yb = yn[:, c * LANES:(c + 1) * LANES]
                o_ref[:, c * LANES:(c + 1) * LANES] = (
                    yb * cos + pltpu.roll(yb, LANES - half, 1) * sa + pltpu.roll(yb, half, 1) * sb)
        else:
            o_ref[...] = yn

    @pl.when(n >= n_norm)
    def _():
        o_ref[...] = y


def _seg_matrix(seg):
    idx = np.arange(TILE_N) // seg
    return jnp.asarray((idx[:, None] == idx[None, :]).astype(np.float32) / seg, dtype=BF16)


def _proj(x, g, w, gains, seg, rope_tables=None):
    rows, d = x.shape
    nt = w.shape[1] // TILE_N
    n_norm = gains.shape[0]
    tm = min(1024, rows)
    rope = rope_tables is not None
    in_specs = [
        pl.BlockSpec((tm, d), lambda i, n: (i, 0)),
        pl.BlockSpec((1, d), lambda i, n: (0, 0)),
        pl.BlockSpec((d, TILE_N), lambda i, n: (0, n)),
        pl.BlockSpec((None, 1, TILE_N), lambda i, n: (jnp.minimum(n, n_norm - 1), 0, 0)),
        pl.BlockSpec((TILE_N, TILE_N), lambda i, n: (0, 0)),
    ]
    args = [x, g.reshape(1, d), w, gains.reshape(n_norm, 1, TILE_N), _seg_matrix(seg)]
    if rope:
        in_specs += [pl.BlockSpec((tm, LANES), lambda i, n: (i, 0))] * 3
        args += list(rope_tables)
    return pl.pallas_call(
        functools.partial(_proj_kernel, n_norm=n_norm, rope=rope),
        grid=(rows // tm, nt),
        in_specs=in_specs,
        out_specs=pl.BlockSpec((None, tm, TILE_N), lambda i, n: (n, i, 0)),
        out_shape=jax.ShapeDtypeStruct((nt, rows, TILE_N), F32),
        scratch_shapes=[pltpu.VMEM((tm, d), BF16)],
        compiler_params=_params("parallel", "arbitrary"),
    )(*args)


def _rope_tables(pos):
    half = ROT_DIM // 2
    rows = pos.shape[0]
    inv_freq = ROPE_THETA ** (-2.0 * jnp.arange(half, dtype=F32) / ROT_DIM)
    ang = pos.astype(F32)[:, None] * inv_freq[None, :]
    c, s = jnp.cos(ang), jnp.sin(ang)
    rest = HEAD_DIM - ROT_DIM
    cos = jnp.concatenate([c, c, jnp.ones((rows, rest), F32)], axis=-1)
    sa = jnp.concatenate([-s, jnp.zeros((rows, HEAD_DIM - half), F32)], axis=-1)
    sb = jnp.concatenate([jnp.zeros((rows, half), F32), s, jnp.zeros((rows, rest), F32)], axis=-1)
    rep = LANES // HEAD_DIM
    return tuple(jnp.tile(t, (1, rep)) for t in (cos, sa, sb))


def _out_kernel(*refs, n_parts):
    x_ref = refs[0]
    o_ref = refs[1 + 2 * n_parts]
    acc = x_ref[...]
    for p in range(n_parts):
        acc = acc + _dot(refs[1 + p][...].astype(BF16), refs[1 + n_parts + p][...])
    o_ref[...] = acc


def _out_proj(x, parts, w_parts):
    rows, d = x.shape
    tm = min(512, rows)
    n_parts = len(parts)
    in_specs = [pl.BlockSpec((tm, d), lambda i: (i, 0))]
    in_specs += [pl.BlockSpec((tm, p.shape[1]), lambda i: (i, 0)) for p in parts]
    in_specs += [pl.BlockSpec(w.shape, lambda i: (0, 0)) for w in w_parts]
    return pl.pallas_call(
        functools.partial(_out_kernel, n_parts=n_parts),
        grid=(rows // tm,),
        in_specs=in_specs,
        out_specs=pl.BlockSpec((tm, d), lambda i: (i, 0)),
        out_shape=jax.ShapeDtypeStruct((rows, d), F32),
        compiler_params=_params("parallel"),
    )(x, *parts, *w_parts)


def _mem_kernel(x_ref, g_ref, wq_ref, gq_ref, k_ref, v_ref, wo_ref, o_ref):
    x = x_ref[...]
    h = _rms(x, g_ref[...]).astype(BF16)
    q = _dot(h, wq_ref[...])
    gq = gq_ref[...]
    scale = MEM_HEAD_DIM ** -0.5
    outs = []
    for hd in range(MEM_HEADS):
        sl = slice(hd * MEM_HEAD_DIM, (hd + 1) * MEM_HEAD_DIM)
        qh = _rms(q[:, sl], gq) * scale
        s = _dot_nt(qh.astype(BF16), k_ref[:, sl].astype(BF16))
        m = jnp.max(s, axis=-1, keepdims=True)
        p = jnp.exp(s - m)
        l = jnp.sum(p, axis=-1, keepdims=True)
        oh = _dot(p.astype(BF16), v_ref[:, sl].astype(BF16)) / l
        outs.append(oh.astype(BF16))
    o = jnp.concatenate(outs, axis=-1)
    o_ref[...] = x + _dot(o, wo_ref[...])


def _mem_attn(x, g, wq, gq, mk, mv, wo):
    b, t, d = x.shape
    m = mk.shape[1]
    tm = min(512, t)
    return pl.pallas_call(
        _mem_kernel,
        grid=(b, t // tm),
        in_specs=[
            pl.BlockSpec((None, tm, d), lambda bi, i: (bi, i, 0)),
            pl.BlockSpec((1, d), lambda bi, i: (0, 0)),
            pl.BlockSpec((d, d), lambda bi, i: (0, 0)),
            pl.BlockSpec((1, MEM_HEAD_DIM), lambda bi, i: (0, 0)),
            pl.BlockSpec((None, m, d), lambda bi, i: (bi, 0, 0)),
            pl.BlockSpec((None, m, d), lambda bi, i: (bi, 0, 0)),
            pl.BlockSpec((d, d), lambda bi, i: (0, 0)),
        ],
        out_specs=pl.BlockSpec((None, tm, d), lambda bi, i: (bi, i, 0)),
        out_shape=jax.ShapeDtypeStruct((b, t, d), F32),
        compiler_params=_params("parallel", "parallel"),
    )(x, g.reshape(1, d), wq, gq.reshape(1, MEM_HEAD_DIM), mk, mv, wo)


def _lane_masks(shape):
    lane = lax.broadcasted_iota(jnp.int32, shape, 1)
    return lane < HEAD_DIM


def _diff_kernel(lam_ref, g_ref, q_ref, k_ref, v_ref, o_ref,
                 m1_ref, l1_ref, acc1_ref, m2_ref, l2_ref, acc2_ref,
                 *, q_off, tq, tk, lam_init):
    i = pl.program_id(2)
    j = pl.program_id(3)
    nk = pl.num_programs(3)

    @pl.when(j == 0)
    def _():
        for m_ref, l_ref, acc_ref in ((m1_ref, l1_ref, acc1_ref), (m2_ref, l2_ref, acc2_ref)):
            m_ref[...] = jnp.full_like(m_ref, NEG)
            l_ref[...] = jnp.zeros_like(l_ref)
            acc_ref[...] = jnp.zeros_like(acc_ref)

    q_first_chunk = (q_off + i * tq) // CHUNK
    q_last_chunk = (q_off + i * tq + tq - 1) // CHUNK
    k_first_chunk = (j * tk) // CHUNK
    k_last_chunk = (j * tk + tk - 1) // CHUNK

    def step(masked):
        q = q_ref[...] * (HEAD_DIM ** -0.5)
        first = _lane_masks(q.shape)
        kb = k_ref[...].astype(BF16)
        vb = v_ref[...].astype(BF16)
        if masked:
            qc = (q_off + i * tq + lax.broadcasted_iota(jnp.int32, (tq, tk), 0)) // CHUNK
            kc = (j * tk + lax.broadcasted_iota(jnp.int32, (tq, tk), 1)) // CHUNK
            vis = kc <= qc
        for qm, m_ref, l_ref, acc_ref in (
                (jnp.where(first, q, 0.0), m1_ref, l1_ref, acc1_ref),
                (jnp.where(first, 0.0, q), m2_ref, l2_ref, acc2_ref)):
            s = _dot_nt(qm.astype(BF16), kb)
            if masked:
                s = jnp.where(vis, s, NEG)
            m_old = m_ref[...]
            m_new = jnp.maximum(m_old, jnp.max(s, axis=-1, keepdims=True))
            alpha = jnp.exp(m_old - m_new)
            p = jnp.exp(s - m_new)
            l_ref[...] = alpha * l_ref[...] + jnp.sum(p, axis=-1, keepdims=True)
            acc_ref[...] = alpha * acc_ref[...] + _dot(p.astype(BF16), vb)
            m_ref[...] = m_new

    @pl.when(k_last_chunk <= q_first_chunk)
    def _():
        step(False)

    @pl.when(jnp.logical_and(k_last_chunk > q_first_chunk, k_first_chunk <= q_last_chunk))
    def _():
        step(True)

    @pl.when(j == nk - 1)
    def _():
        lv = lam_ref[...]
        lam = (jnp.exp(jnp.sum(lv[0:1] * lv[1:2], axis=-1, keepdims=True))
               - jnp.exp(jnp.sum(lv[2:3] * lv[3:4], axis=-1, keepdims=True)) + lam_init)
        o = acc1_ref[...] / l1_ref[...] - lam * (acc2_ref[...] / l2_ref[...])
        o_ref[...] = _rms(o, g_ref[...]) * (1.0 - lam_init)


def _diff_attn(lam_vecs, subln_g, q_sel, k_sel, v_sel, *, q_off, tq, tk, lam_init):
    (qa, qt), (ka, kt), (va, vt) = q_sel, k_sel, v_sel
    b, t_q = qa.shape[1], qa.shape[2]
    t_k = ka.shape[2]
    n_heads = qa.shape[3] // LANES
    nq, nk = t_q // tq, t_k // tk

    def kv_block(i, j):
        last_chunk_end = (q_off + i * tq + tq - 1) // CHUNK * CHUNK + CHUNK - 1
        return jnp.minimum(j, jnp.minimum(last_chunk_end // tk, nk - 1))

    return pl.pallas_call(
        functools.partial(_diff_kernel, q_off=q_off, tq=tq, tk=tk, lam_init=lam_init),
        grid=(b, n_heads, nq, nk),
        in_specs=[
            pl.BlockSpec((4, HEAD_DIM), lambda bi, h, i, j: (0, 0)),
            pl.BlockSpec((1, LANES), lambda bi, h, i, j: (0, 0)),
            pl.BlockSpec((None, None, tq, LANES), lambda bi, h, i, j: (qt, bi, i, h)),
            pl.BlockSpec((None, None, tk, LANES), lambda bi, h, i, j: (kt, bi, kv_block(i, j), h)),
            pl.BlockSpec((None, None, tk, LANES), lambda bi, h, i, j: (vt, bi, kv_block(i, j), h)),
        ],
        out_specs=pl.BlockSpec((None, tq, LANES), lambda bi, h, i, j: (bi, i, h)),
        out_shape=jax.ShapeDtypeStruct((b, t_q, n_heads * LANES), F32),
        scratch_shapes=[pltpu.VMEM((tq, 1), F32), pltpu.VMEM((tq, 1), F32), pltpu.VMEM((tq, LANES), F32)] * 2,
        compiler_params=_params("parallel", "parallel", "parallel", "arbitrary"),
    )(lam_vecs, subln_g.reshape(1, LANES), qa, ka, va)


def _stick_kernel(u_ref, q_ref, k_ref, v_ref, o_ref, acc_e_ref, acc_o_ref, r_e_ref, r_o_ref, done_ref,
                  *, q_off, tq, tk, nk):
    i = pl.program_id(2)
    jj = pl.program_id(3)
    j_start = jnp.minimum((q_off + i * tq + tq - 2) // tk, nk - 1)
    j = j_start - jj

    @pl.when(jj == 0)
    def _():
        acc_e_ref[...] = jnp.zeros_like(acc_e_ref)
        acc_o_ref[...] = jnp.zeros_like(acc_o_ref)
        r_e_ref[...] = jnp.zeros_like(r_e_ref)
        r_o_ref[...] = jnp.zeros_like(r_o_ref)
        done_ref[0] = 0

    def step(masked):
        q = q_ref[...] * (HEAD_DIM ** -0.5)
        first = _lane_masks(q.shape)
        kb = k_ref[...].astype(BF16)
        vb = v_ref[...].astype(BF16)
        u = u_ref[...]
        if masked:
            qpos = q_off + i * tq + lax.broadcasted_iota(jnp.int32, (tq, tk), 0)
            kpos = j * tk + lax.broadcasted_iota(jnp.int32, (tq, tk), 1)
            causal = kpos < qpos
        r_max = None
        for qm, acc_ref, r_ref in ((jnp.where(first, q, 0.0), acc_e_ref, r_e_ref),
                                   (jnp.where(first, 0.0, q), acc_o_ref, r_o_ref)):
            z = _dot_nt(qm.astype(BF16), kb)
            sp = jnp.maximum(z, 0.0) + jnp.log(1.0 + jnp.exp(-jnp.abs(z)))
            log_1m = -sp
            if masked:
                log_1m = jnp.where(causal, log_1m, 0.0)
            hi, lo = _split_bf16(log_1m)
            after = _dot(hi, u) + _dot(lo, u)
            r_old = r_ref[...]
            w = jnp.exp((z - sp) + after + r_old)
            if masked:
                w = jnp.where(causal, w, 0.0)
            acc_ref[...] += _dot(w.astype(BF16), vb)
            r_new = r_old + after[:, 0:1] + log_1m[:, 0:1]
            r_ref[...] = r_new
            mx = jnp.max(r_new)
            r_max = mx if r_max is None else jnp.maximum(r_max, mx)
        done_ref[0] = (r_max < EXP_UNDERFLOW).astype(jnp.int32)

    live = jnp.logical_and(j >= 0, done_ref[0] == 0)
    needs_mask = j * tk + tk - 1 >= q_off + i * tq

    @pl.when(jnp.logical_and(live, jnp.logical_not(needs_mask)))
    def _():
        step(False)

    @pl.when(jnp.logical_and(live, needs_mask))
    def _():
        step(True)

    @pl.when(jj == pl.num_programs(3) - 1)
    def _():
        o_ref[...] = jnp.where(_lane_masks(o_ref.shape), acc_e_ref[...], acc_o_ref[...])


def _stick_attn(q_sel, k_sel, v_sel, *, q_off, tq, tk):
    (qa, qt), (ka, kt), (va, vt) = q_sel, k_sel, v_sel
    b, t_q = qa.shape[1], qa.shape[2]
    t_k = ka.shape[2]
    n_pairs = qa.shape[3] // LANES
    nq, nk = t_q // tq, t_k // tk
    upper = jnp.asarray((np.arange(tk)[:, None] > np.arange(tk)[None, :]).astype(np.float32), dtype=BF16)

    def kv_block(i, jj):
        j_start = jnp.minimum((q_off + i * tq + tq - 2) // tk, nk - 1)
        return jnp.maximum(j_start - jj, 0)

    return pl.pallas_call(
        functools.partial(_stick_kernel, q_off=q_off, tq=tq, tk=tk, nk=nk),
        grid=(b, n_pairs, nq, nk),
        in_specs=[
            pl.BlockSpec((tk, tk), lambda bi, h, i, jj: (0, 0)),
            pl.BlockSpec((None, None, tq, LANES), lambda bi, h, i, jj: (qt, bi, i, h)),
            pl.BlockSpec((None, None, tk, LANES), lambda bi, h, i, jj: (kt, bi, kv_block(i, jj), h)),
            pl.BlockSpec((None, None, tk, LANES), lambda bi, h, i, jj: (vt, bi, kv_block(i, jj), h)),
        ],
        out_specs=pl.BlockSpec((None, tq, LANES), lambda bi, h, i, jj: (bi, i, h)),
        out_shape=jax.ShapeDtypeStruct((b, t_q, n_pairs * LANES), F32),
        scratch_shapes=[pltpu.VMEM((tq, LANES), F32), pltpu.VMEM((tq, LANES), F32),
                        pltpu.VMEM((tq, 1), F32), pltpu.VMEM((tq, 1), F32), pltpu.SMEM((1,), jnp.int32)],
        compiler_params=_params("parallel", "parallel", "parallel", "arbitrary"),
    )(upper, qa, ka, va)


def _band_kernel(*refs, nb, tkb):
    bias_ref, q_ref = refs[0], refs[1]
    k_refs = refs[2:2 + nb]
    v_refs = refs[2 + nb:2 + 2 * nb]
    o_ref = refs[2 + 2 * nb]
    i = pl.program_id(2)
    q = q_ref[...] * (HEAD_DIM ** -0.5)
    first = _lane_masks(q.shape)
    kbs = [r[...].astype(BF16) for r in k_refs]
    vbs = [r[...].astype(BF16) for r in v_refs]
    outs = []
    for hd, qm in enumerate((jnp.where(first, q, 0.0), jnp.where(first, 0.0, q))):
        qm = qm.astype(BF16)
        ss = []
        for d in range(nb):
            s = _dot_nt(qm, kbs[d]) + bias_ref[hd, :, d * tkb:(d + 1) * tkb]
            if d < nb - 1:
                s = jnp.where(i - (nb - 1) + d >= 0, s, NEG)
            ss.append(s)
        m = functools.reduce(jnp.maximum, [jnp.max(s, axis=-1, keepdims=True) for s in ss])
        ps = [jnp.exp(s - m) for s in ss]
        l = functools.reduce(lambda a, c: a + c, [jnp.sum(p, axis=-1, keepdims=True) for p in ps])
        o = functools.reduce(lambda a, c: a + c, [_dot(p.astype(BF16), vb) for p, vb in zip(ps, vbs)])
        outs.append(o / l)
    o_ref[...] = jnp.where(first, outs[0], outs[1])


def _band_bias(bias_table, qpos, kpos):
    rel = np.clip(qpos[:, None] - kpos[None, :], -REL_CLIP, REL_CLIP) + REL_CLIP
    qc, kc = qpos[:, None] // CHUNK, kpos[None, :] // CHUNK
    vis = (kpos[None, :] >= 0) & (kc <= qc) & (qc - kc <= C_PAST_CHUNKS)
    return jnp.where(jnp.asarray(vis)[None], bias_table.astype(F32)[:, jnp.asarray(rel)], NEG)


def _band_attn(bias, q_sel, k_sel, v_sel, *, tq, nb, tkb):
    (qa, qt), (ka, kt), (va, vt) = q_sel, k_sel, v_sel
    b, t_q = qa.shape[1], qa.shape[2]
    per_tile = TILE_N // LANES
    n_pairs = bias.shape[0] // 2
    nq = t_q // tq

    def kv_spec(d, t0):
        return pl.BlockSpec((None, None, tkb, LANES),
                            lambda bi, h, i: (t0 + h // per_tile, bi, jnp.maximum(i - (nb - 1) + d, 0), h % per_tile))

    in_specs = [
        pl.BlockSpec((2, tq, nb * tkb), lambda bi, h, i: (h, 0, 0)),
        pl.BlockSpec((None, None, tq, LANES), lambda bi, h, i: (qt + h // per_tile, bi, i, h % per_tile)),
    ]
    in_specs += [kv_spec(d, kt) for d in range(nb)] + [kv_spec(d, vt) for d in range(nb)]
    return pl.pallas_call(
        functools.partial(_band_kernel, nb=nb, tkb=tkb),
        grid=(b, n_pairs, nq),
        in_specs=in_specs,
        out_specs=pl.BlockSpec((None, None, tq, LANES), lambda bi, h, i: (h // per_tile, bi, i, h % per_tile)),
        out_shape=jax.ShapeDtypeStruct((n_pairs // per_tile, b, t_q, TILE_N), F32),
        compiler_params=_params("parallel", "parallel", "parallel"),
    )(bias, qa, *([ka] * nb), *([va] * nb))


def _lambda_init(layer):
    return 0.8 - 0.6 * math.exp(-0.3 * layer)


def _tile_gain(g, n):
    return jnp.tile(g.astype(F32), n)


def _trunk(x, pos0, mem_k, mem_v, past_ab, past_c, p):
    b, t, d = x.shape
    rows = b * t
    depth = p['ffn1_g'].shape[0]
    prompt = past_ab is None
    new_ab, new_c = [], []
    pos_rows = jnp.tile(pos0 + jnp.arange(t), b)
    x = x.reshape(rows, d)
    heads_per_tile = TILE_N // HEAD_DIM
    for l in range(depth):
        x = _ffn(x, p['ffn1_g'][l], p['ffn1_wg'][l], p['ffn1_wu'][l], p['ffn1_wd'][l])
        li = l // 2
        if l % 2 == 0:
            gains = jnp.stack([_tile_gain(p['a_gq'][li], heads_per_tile), _tile_gain(p['a_gk'][li], heads_per_tile)])
            proj = _proj(x, p['mix_g'][l], p['ab_w_in'][li], gains, HEAD_DIM, _rope_tables(pos_rows))
            proj = proj.reshape(6, b, t, TILE_N)
            new_ab.append((proj[1], proj[2], proj[4], proj[5]))
            lam_vecs = jnp.stack([p['a_lq1'][li], p['a_lk1'][li], p['a_lq2'][li], p['a_lk2'][li]]).astype(F32)
            lam_init = _lambda_init(l)
            if prompt:
                tq = min(512, t)
                a_out = _diff_attn(lam_vecs, p['a_subln_g'][li], (proj, 0), (proj, 1), (proj, 2),
                                   q_off=0, tq=tq, tk=tq, lam_init=lam_init)
                b_out = _stick_attn((proj, 3), (proj, 4), (proj, 5), q_off=0, tq=tq, tk=min(256, t))
            else:
                pka, pva, pkb, pvb = (c[li].reshape(b, -1, TILE_N) for c in past_ab)
                past_len = pka.shape[1]
                ka = jnp.concatenate([pka, proj[1]], axis=1)[None]
                va = jnp.concatenate([pva, proj[2]], axis=1)[None]
                kb = jnp.concatenate([pkb, proj[4]], axis=1)[None]
                vb = jnp.concatenate([pvb, proj[5]], axis=1)[None]
                tk = past_len + t
                a_out = _diff_attn(lam_vecs, p['a_subln_g'][li], (proj, 0), (ka, 0), (va, 0),
                                   q_off=past_len, tq=t, tk=tk, lam_init=lam_init)
                b_out = _stick_attn((proj, 3), (kb, 0), (vb, 0), q_off=past_len, tq=t, tk=tk)
            w_out = p['ab_w_out'][li]
            half = w_out.shape[0] // 2
            x = _out_proj(x, [a_out.reshape(rows, -1), b_out.reshape(rows, -1)], [w_out[:half], w_out[half:]])
        else:
            gains = jnp.stack([_tile_gain(p['c_gq'][li], heads_per_tile)] * 2
                              + [_tile_gain(p['c_gk'][li], heads_per_tile)] * 2)
            proj = _proj(x, p['mix_g'][l], p['c_w_in'][li], gains, HEAD_DIM).reshape(6, b, t, TILE_N)
            k_full = jnp.concatenate([proj[2], proj[3]], axis=-1)
            v_full = jnp.concatenate([proj[4], proj[5]], axis=-1)
            band = C_PAST_CHUNKS * CHUNK
            if prompt:
                tq = min(256, t)
                nb = band // tq + 1
                qpos = (nb - 1) * tq + np.arange(tq)
                kpos = np.arange(nb * tq)
                bias = _band_bias(p['c_bias'][li], qpos, kpos)
                o = _band_attn(bias, (proj, 0), (proj, 2), (proj, 4), tq=tq, nb=nb, tkb=tq)
                keep = min(band, t)
                new_c.append((k_full[:, t - keep:], v_full[:, t - keep:]))
            else:
                pk, pv = (c[li].reshape(b, -1, 2, TILE_N) for c in past_c)
                pc = pk.shape[1]
                k_all = jnp.concatenate([jnp.moveaxis(pk, 2, 0), proj[2:4]], axis=2)
                v_all = jnp.concatenate([jnp.moveaxis(pv, 2, 0), proj[4:6]], axis=2)
                qpos = pos0 + np.arange(t)
                kpos = pos0 - pc + np.arange(pc + t)
                bias = _band_bias(p['c_bias'][li], qpos, kpos)
                o = _band_attn(bias, (proj, 0), (k_all, 0), (v_all, 0), tq=t, nb=1, tkb=pc + t)
                new_c.append((jnp.concatenate([pk.reshape(b, pc, -1), k_full], axis=1)[:, -pc:],
                              jnp.concatenate([pv.reshape(b, pc, -1), v_full], axis=1)[:, -pc:]))
            w_out = p['c_w_out'][li]
            half = w_out.shape[0] // 2
            x = _out_proj(x, [o[0].reshape(rows, -1), o[1].reshape(rows, -1)], [w_out[:half], w_out[half:]])
        x = _mem_attn(x.reshape(b, t, d), p['mem_g_x'][l], p['mem_wq'][l], p['mem_gq'][l],
                      mem_k[l], mem_v[l], p['mem_wo'][l]).reshape(rows, d)
        x = _ffn(x, p['ffn2_g'][l], p['ffn2_wg'][l], p['ffn2_wu'][l], p['ffn2_wd'][l])
    return x.reshape(b, t, d), new_ab, new_c


def kernel(x_prompt, x_sample, cache_a_k, cache_a_v, cache_b_k, cache_b_v, cache_c_k, cache_c_v, cache_mem_k, cache_mem_v, mem_prompt, ffn1_g, ffn1_wg, ffn1_wu, ffn1_wd, ffn2_g, ffn2_wg, ffn2_wu, ffn2_wd, mix_g, ab_w_in, ab_w_out, a_gq, a_gk, a_lq1, a_lk1, a_lq2, a_lk2, a_subln_g, c_w_in, c_w_out, c_gq, c_gk, c_bias, mem_g_x, mem_g_m, mem_wq, mem_wk, mem_wv, mem_wo, mem_gq, mem_gk):
    depth = ffn1_g.shape[0]
    p = dict(ffn1_g=ffn1_g, ffn2_g=ffn2_g, mix_g=mix_g, a_gq=a_gq, a_gk=a_gk, a_lq1=a_lq1, a_lk1=a_lk1,
             a_lq2=a_lq2, a_lk2=a_lk2, a_subln_g=a_subln_g, c_gq=c_gq, c_gk=c_gk, c_bias=c_bias,
             mem_g_x=mem_g_x, mem_gq=mem_gq)
    for name, w in (('ffn1_wg', ffn1_wg), ('ffn1_wu', ffn1_wu), ('ffn1_wd', ffn1_wd), ('ffn2_wg', ffn2_wg),
                    ('ffn2_wu', ffn2_wu), ('ffn2_wd', ffn2_wd), ('ab_w_in', ab_w_in), ('ab_w_out', ab_w_out),
                    ('c_w_in', c_w_in), ('c_w_out', c_w_out), ('mem_wq', mem_wq), ('mem_wo', mem_wo)):
        p[name] = w.astype(BF16)

    bp, tp, d = x_prompt.shape
    bs, ts, _ = x_sample.shape
    m_len = mem_prompt.shape[1]
    past_len = cache_a_k.shape[2]

    heads_per_tile = TILE_N // MEM_HEAD_DIM
    mem_k_list, mem_v_list = [], []
    for l in range(depth):
        w_kv = jnp.concatenate([mem_wk[l], mem_wv[l]], axis=1).astype(BF16)
        gains = jnp.stack([_tile_gain(mem_gk[l], heads_per_tile)] * (d // TILE_N))
        kv = _proj(mem_prompt.reshape(bp * m_len, d), mem_g_m[l], w_kv, gains, MEM_HEAD_DIM)
        kv = kv.reshape(2, d // TILE_N, bp, m_len, TILE_N)
        kv = jnp.moveaxis(kv, 1, 3).reshape(2, bp, m_len, d)
        mem_k_list.append(kv[0])
        mem_v_list.append(kv[1])

    y_prompt, ab_p, c_p = _trunk(x_prompt, 0, mem_k_list, mem_v_list, None, None, p)
    cache_mem_k2 = cache_mem_k.reshape(depth, bs, m_len, d)
    cache_mem_v2 = cache_mem_v.reshape(depth, bs, m_len, d)
    y_sample, ab_s, c_s = _trunk(x_sample, past_len, cache_mem_k2, cache_mem_v2,
                                 (cache_a_k, cache_a_v, cache_b_k, cache_b_v), (cache_c_k, cache_c_v), p)

    a_heads = cache_a_v.shape[3]
    b_heads = cache_b_k.shape[3]
    c_heads = cache_c_k.shape[3]

    def stack(rows, idx, heads, width):
        return jnp.stack([r[idx].reshape(r[idx].shape[0], r[idx].shape[1], heads, width) for r in rows])

    outs = [y_prompt, y_sample]
    outs += [stack(ab_p, 0, 2 * a_heads, HEAD_DIM), stack(ab_p, 1, a_heads, 2 * HEAD_DIM),
             stack(ab_p, 2, b_heads, HEAD_DIM), stack(ab_p, 3, b_heads, HEAD_DIM),
             stack(c_p, 0, c_heads, HEAD_DIM), stack(c_p, 1, c_heads, HEAD_DIM)]
    outs += [jnp.stack(mem_k_list).reshape(depth, bp, m_len, MEM_HEADS, MEM_HEAD_DIM),
             jnp.stack(mem_v_list).reshape(depth, bp, m_len, MEM_HEADS, MEM_HEAD_DIM)]
    outs += [stack(ab_s, 0, 2 * a_heads, HEAD_DIM), stack(ab_s, 1, a_heads, 2 * HEAD_DIM),
             stack(ab_s, 2, b_heads, HEAD_DIM), stack(ab_s, 3, b_heads, HEAD_DIM),
             stack(c_s, 0, c_heads, HEAD_DIM), stack(c_s, 1, c_heads, HEAD_DIM)]
    return tuple(outs)
```

```python
import functools
import math

import jax
import jax.numpy as jnp
import numpy as np
from jax import lax
from jax.experimental import pallas as pl
from jax.experimental.pallas import tpu as pltpu

D_MODEL = 1024
CHUNK = 64
HEAD_DIM = 64
ROT_DIM = HEAD_DIM // 4
ROPE_THETA = 500000.0
C_PAST_CHUNKS = 8
REL_CLIP = 128
MEM_HEADS = 4
MEM_HEAD_DIM = D_MODEL // MEM_HEADS
RMS_EPS = 1e-6
NEG = -1e30
LANES = 128
TILE_N = 512
EXP_UNDERFLOW = -104.0
VMEM_LIMIT = 56 * 1024 * 1024

BF16 = jnp.bfloat16
F32 = jnp.float32


def _dot(a, b):
    return jnp.dot(a, b, preferred_element_type=F32)


def _dot_nt(a, b):
    return lax.dot_general(a, b, (((1,), (1,)), ((), ())), preferred_element_type=F32)


def _rms(x, g):
    return x * lax.rsqrt(jnp.mean(x * x, axis=-1, keepdims=True) + RMS_EPS) * g


def _split_bf16(x):
    hi = x.astype(BF16)
    lo = (x - hi.astype(F32)).astype(BF16)
    return hi, lo


def _params(*sem):
    return pltpu.CompilerParams(dimension_semantics=sem, vmem_limit_bytes=VMEM_LIMIT)


def _ffn_kernel(x_ref, g_ref, wg_ref, wu_ref, wd_ref, o_ref, h_ref, acc_ref):
    f = pl.program_id(1)

    @pl.when(f == 0)
    def _():
        h_ref[...] = _rms(x_ref[...], g_ref[...]).astype(BF16)
        acc_ref[...] = jnp.zeros_like(acc_ref)

    h = h_ref[...]
    a = _dot(h, wg_ref[...])
    u = _dot(h, wu_ref[...])
    act = a * (1.0 / (1.0 + jnp.exp(-a))) * u
    acc_ref[...] += _dot(act.astype(BF16), wd_ref[...])

    @pl.when(f == pl.num_programs(1) - 1)
    def _():
        o_ref[...] = x_ref[...] + 0.5 * acc_ref[...]


def _ffn(x, g, wg, wu, wd):
    rows, d = x.shape
    ff = wg.shape[1]
    tm = min(512, rows)
    tf = ff // 2 if (ff // 2) % LANES == 0 else ff
    return pl.pallas_call(
        _ffn_kernel,
        grid=(rows // tm, ff // tf),
        in_specs=[
            pl.BlockSpec((tm, d), lambda i, f: (i, 0)),
            pl.BlockSpec((1, d), lambda i, f: (0, 0)),
            pl.BlockSpec((d, tf), lambda i, f: (0, f)),
            pl.BlockSpec((d, tf), lambda i, f: (0, f)),
            pl.BlockSpec((tf, d), lambda i, f: (f, 0)),
        ],
        out_specs=pl.BlockSpec((tm, d), lambda i, f: (i, 0)),
        out_shape=jax.ShapeDtypeStruct((rows, d), F32),
        scratch_shapes=[pltpu.VMEM((tm, d), BF16), pltpu.VMEM((tm, d), F32)],
        compiler_params=_params("parallel", "arbitrary"),
        name="ffn",
    )(x, g.reshape(1, d), wg, wu, wd)


def _proj_kernel(*refs, n_norm, rope, q_tiles, q_scale):
    if rope:
        x_ref, g_ref, w_ref, gain_ref, seg_ref, cos_ref, sa_ref, sb_ref, o_ref, ob_ref, h_ref = refs
    else:
        x_ref, g_ref, w_ref, gain_ref, seg_ref, o_ref, ob_ref, h_ref = refs
    n = pl.program_id(1)

    @pl.when(n == 0)
    def _():
        h_ref[...] = _rms(x_ref[...], g_ref[...]).astype(BF16)

    y = _dot(h_ref[...], w_ref[...])
    scale = functools.reduce(lambda acc, t: jnp.where(n == t, q_scale, acc), q_tiles, 1.0)

    def emit(val):
        o_ref[...] = val
        ob_ref[...] = (val * scale).astype(BF16)

    @pl.when(n < n_norm)
    def _():
        hi, lo = _split_bf16(y * y)
        ms = _dot(hi, seg_ref[...]) + _dot(lo, seg_ref[...])
        yn = y * lax.rsqrt(ms + RMS_EPS) * gain_ref[...]
        if rope:
            cos, sa, sb = cos_ref[...], sa_ref[...], sb_ref[...]
            half = ROT_DIM // 2
            blocks = []
            for c in range(TILE_N // LANES):
                yb = yn[:, c * LANES:(c + 1) * LANES]
                blocks.append(yb * cos + pltpu.roll(yb, LANES - half, 1) * sa + pltpu.roll(yb, half, 1) * sb)
            yn = jnp.concatenate(blocks, axis=-1)
        emit(yn)

    @pl.when(n >= n_norm)
    def _():
        emit(y)


def _seg_matrix(seg):
    idx = np.arange(TILE_N) // seg
    return jnp.asarray((idx[:, None] == idx[None, :]).astype(np.float32) / seg, dtype=BF16)


def _proj(x, g, w, gains, seg, rope_tables=None, q_tiles=(), q_scale=1.0, name="proj"):
    rows, d = x.shape
    nt = w.shape[1] // TILE_N
    n_norm = gains.shape[0]
    tm = min(1024, rows)
    rope = rope_tables is not None
    in_specs = [
        pl.BlockSpec((tm, d), lambda i, n: (i, 0)),
        pl.BlockSpec((1, d), lambda i, n: (0, 0)),
        pl.BlockSpec((d, TILE_N), lambda i, n: (0, n)),
        pl.BlockSpec((None, 1, TILE_N), lambda i, n: (jnp.minimum(n, n_norm - 1), 0, 0)),
        pl.BlockSpec((TILE_N, TILE_N), lambda i, n: (0, 0)),
    ]
    args = [x, g.reshape(1, d), w, gains.reshape(n_norm, 1, TILE_N), _seg_matrix(seg)]
    if rope:
        in_specs += [pl.BlockSpec((tm, LANES), lambda i, n: (i, 0))] * 3
        args += list(rope_tables)
    return pl.pallas_call(
        functools.partial(_proj_kernel, n_norm=n_norm, rope=rope, q_tiles=tuple(q_tiles), q_scale=q_scale),
        grid=(rows // tm, nt),
        in_specs=in_specs,
        out_specs=[pl.BlockSpec((None, tm, TILE_N), lambda i, n: (n, i, 0))] * 2,
        out_shape=[jax.ShapeDtypeStruct((nt, rows, TILE_N), F32), jax.ShapeDtypeStruct((nt, rows, TILE_N), BF16)],
        scratch_shapes=[pltpu.VMEM((tm, d), BF16)],
        compiler_params=_params("parallel", "arbitrary"),
        name=name,
    )(*args)


def _rope_tables(pos):
    half = ROT_DIM // 2
    rows = pos.shape[0]
    inv_freq = ROPE_THETA ** (-2.0 * jnp.arange(half, dtype=F32) / ROT_DIM)
    ang = pos.astype(F32)[:, None] * inv_freq[None, :]
    c, s = jnp.cos(ang), jnp.sin(ang)
    rest = HEAD_DIM - ROT_DIM
    cos = jnp.concatenate([c, c, jnp.ones((rows, rest), F32)], axis=-1)
    sa = jnp.concatenate([-s, jnp.zeros((rows, HEAD_DIM - half), F32)], axis=-1)
    sb = jnp.concatenate([jnp.zeros((rows, half), F32), s, jnp.zeros((rows, rest), F32)], axis=-1)
    rep = LANES // HEAD_DIM
    return tuple(jnp.tile(t, (1, rep)) for t in (cos, sa, sb))


def _out_kernel(*refs, n_parts):
    x_ref = refs[0]
    o_ref = refs[1 + 2 * n_parts]
    acc = x_ref[...]
    for p in range(n_parts):
        acc = acc + _dot(refs[1 + p][...].astype(BF16), refs[1 + n_parts + p][...])
    o_ref[...] = acc


def _out_proj(x, parts, w_parts):
    rows, d = x.shape
    tm = min(512, rows)
    n_parts = len(parts)
    in_specs = [pl.BlockSpec((tm, d), lambda i: (i, 0))]
    in_specs += [pl.BlockSpec((tm, p.shape[1]), lambda i: (i, 0)) for p in parts]
    in_specs += [pl.BlockSpec(w.shape, lambda i: (0, 0)) for w in w_parts]
    return pl.pallas_call(
        functools.partial(_out_kernel, n_parts=n_parts),
        grid=(rows // tm,),
        in_specs=in_specs,
        out_specs=pl.BlockSpec((tm, d), lambda i: (i, 0)),
        out_shape=jax.ShapeDtypeStruct((rows, d), F32),
        compiler_params=_params("parallel"),
        name="out_proj",
    )(x, *parts, *w_parts)


def _mem_kernel(x_ref, g_ref, wq_ref, gq_ref, k_ref, v_ref, wo_ref, o_ref):
    x = x_ref[...]
    h = _rms(x, g_ref[...]).astype(BF16)
    q = _dot(h, wq_ref[...])
    gq = gq_ref[...]
    scale = MEM_HEAD_DIM ** -0.5
    outs = []
    for hd in range(MEM_HEADS):
        sl = slice(hd * MEM_HEAD_DIM, (hd + 1) * MEM_HEAD_DIM)
        qh = _rms(q[:, sl], gq) * scale
        s = _dot_nt(qh.astype(BF16), k_ref[:, sl].astype(BF16))
        m = jnp.max(s, axis=-1, keepdims=True)
        p = jnp.exp(s - m)
        l = jnp.sum(p, axis=-1, keepdims=True)
        oh = _dot(p.astype(BF16), v_ref[:, sl].astype(BF16)) / l
        outs.append(oh.astype(BF16))
    o = jnp.concatenate(outs, axis=-1)
    o_ref[...] = x + _dot(o, wo_ref[...])


def _mem_attn(x, g, wq, gq, mk, mv, wo):
    b, t, d = x.shape
    m = mk.shape[1]
    tm = min(512, t)
    return pl.pallas_call(
        _mem_kernel,
        grid=(b, t // tm),
        in_specs=[
            pl.BlockSpec((None, tm, d), lambda bi, i: (bi, i, 0)),
            pl.BlockSpec((1, d), lambda bi, i: (0, 0)),
            pl.BlockSpec((d, d), lambda bi, i: (0, 0)),
            pl.BlockSpec((1, MEM_HEAD_DIM), lambda bi, i: (0, 0)),
            pl.BlockSpec((None, m, d), lambda bi, i: (bi, 0, 0)),
            pl.BlockSpec((None, m, d), lambda bi, i: (bi, 0, 0)),
            pl.BlockSpec((d, d), lambda bi, i: (0, 0)),
        ],
        out_specs=pl.BlockSpec((None, tm, d), lambda bi, i: (bi, i, 0)),
        out_shape=jax.ShapeDtypeStruct((b, t, d), F32),
        compiler_params=_params("parallel", "parallel"),
        name="mem_attn",
    )(x, g.reshape(1, d), wq, gq.reshape(1, MEM_HEAD_DIM), mk, mv, wo)


def _lane_masks(shape):
    lane = lax.broadcasted_iota(jnp.int32, shape, 1)
    return lane < HEAD_DIM


def _diff_kernel(lam_ref, g_ref, q_ref, k_ref, v_ref, o_ref,
                 m1_ref, l1_ref, acc1_ref, m2_ref, l2_ref, acc2_ref,
                 *, q_off, tq, tk, lam_init):
    i = pl.program_id(2)
    j = pl.program_id(3)
    nk = pl.num_programs(3)

    @pl.when(j == 0)
    def _():
        for m_ref, l_ref, acc_ref in ((m1_ref, l1_ref, acc1_ref), (m2_ref, l2_ref, acc2_ref)):
            m_ref[...] = jnp.full_like(m_ref, NEG)
            l_ref[...] = jnp.zeros_like(l_ref)
            acc_ref[...] = jnp.zeros_like(acc_ref)

    q_first_chunk = (q_off + i * tq) // CHUNK
    q_last_chunk = (q_off + i * tq + tq - 1) // CHUNK
    k_first_chunk = (j * tk) // CHUNK
    k_last_chunk = (j * tk + tk - 1) // CHUNK

    def step(masked):
        q = q_ref[...] * (HEAD_DIM ** -0.5)
        first = _lane_masks(q.shape)
        kb = k_ref[...].astype(BF16)
        vb = v_ref[...].astype(BF16)
        if masked:
            qc = (q_off + i * tq + lax.broadcasted_iota(jnp.int32, (tq, tk), 0)) // CHUNK
            kc = (j * tk + lax.broadcasted_iota(jnp.int32, (tq, tk), 1)) // CHUNK
            vis = kc <= qc
        for qm, m_ref, l_ref, acc_ref in (
                (jnp.where(first, q, 0.0), m1_ref, l1_ref, acc1_ref),
                (jnp.where(first, 0.0, q), m2_ref, l2_ref, acc2_ref)):
            s = _dot_nt(qm.astype(BF16), kb)
            if masked:
                s = jnp.where(vis, s, NEG)
            m_old = m_ref[...]
            m_new = jnp.maximum(m_old, jnp.max(s, axis=-1, keepdims=True))
            alpha = jnp.exp(m_old - m_new)
            p = jnp.exp(s - m_new)
            l_ref[...] = alpha * l_ref[...] + jnp.sum(p, axis=-1, keepdims=True)
            acc_ref[...] = alpha * acc_ref[...] + _dot(p.astype(BF16), vb)
            m_ref[...] = m_new

    @pl.when(k_last_chunk <= q_first_chunk)
    def _():
        step(False)

    @pl.when(jnp.logical_and(k_last_chunk > q_first_chunk, k_first_chunk <= q_last_chunk))
    def _():
        step(True)

    @pl.when(j == nk - 1)
    def _():
        lv = lam_ref[...]
        lam = (jnp.exp(jnp.sum(lv[0:1] * lv[1:2], axis=-1, keepdims=True))
               - jnp.exp(jnp.sum(lv[2:3] * lv[3:4], axis=-1, keepdims=True)) + lam_init)
        o = acc1_ref[...] / l1_ref[...] - lam * (acc2_ref[...] / l2_ref[...])
        o_ref[...] = _rms(o, g_ref[...]) * (1.0 - lam_init)


def _diff_attn(lam_vecs, subln_g, q_sel, k_sel, v_sel, *, q_off, tq, tk, lam_init):
    (qa, qt), (ka, kt), (va, vt) = q_sel, k_sel, v_sel
    b, t_q = qa.shape[1], qa.shape[2]
    t_k = ka.shape[2]
    n_heads = qa.shape[3] // LANES
    nq, nk = t_q // tq, t_k // tk

    def kv_block(i, j):
        last_chunk_end = (q_off + i * tq + tq - 1) // CHUNK * CHUNK + CHUNK - 1
        return jnp.minimum(j, jnp.minimum(last_chunk_end // tk, nk - 1))

    return pl.pallas_call(
        functools.partial(_diff_kernel, q_off=q_off, tq=tq, tk=tk, lam_init=lam_init),
        grid=(b, n_heads, nq, nk),
        in_specs=[
            pl.BlockSpec((4, HEAD_DIM), lambda bi, h, i, j: (0, 0)),
            pl.BlockSpec((1, LANES), lambda bi, h, i, j: (0, 0)),
            pl.BlockSpec((None, None, tq, LANES), lambda bi, h, i, j: (qt, bi, i, h)),
            pl.BlockSpec((None, None, tk, LANES), lambda bi, h, i, j: (kt, bi, kv_block(i, j), h)),
            pl.BlockSpec((None, None, tk, LANES), lambda bi, h, i, j: (vt, bi, kv_block(i, j), h)),
        ],
        out_specs=pl.BlockSpec((None, tq, LANES), lambda bi, h, i, j: (bi, i, h)),
        out_shape=jax.ShapeDtypeStruct((b, t_q, n_heads * LANES), F32),
        scratch_shapes=[pltpu.VMEM((tq, 1), F32), pltpu.VMEM((tq, 1), F32), pltpu.VMEM((tq, LANES), F32)] * 2,
        compiler_params=_params("parallel", "parallel", "parallel", "arbitrary"),
        name="diff_attn",
    )(lam_vecs, subln_g.reshape(1, LANES), qa, ka, va)


def _stick_kernel(u_ref, q_ref, k_ref, v_ref, o_ref, acc_e_ref, acc_o_ref, r_e_ref, r_o_ref, done_ref,
                  *, q_off, tq, tk, nk):
    i = pl.program_id(2)
    jj = pl.program_id(3)
    j_start = jnp.minimum((q_off + i * tq + tq - 2) // tk, nk - 1)
    j = j_start - jj

    @pl.when(jj == 0)
    def _():
        acc_e_ref[...] = jnp.zeros_like(acc_e_ref)
        acc_o_ref[...] = jnp.zeros_like(acc_o_ref)
        r_e_ref[...] = jnp.zeros_like(r_e_ref)
        r_o_ref[...] = jnp.zeros_like(r_o_ref)
        done_ref[0] = 0

    def step(masked):
        q = q_ref[...] * (HEAD_DIM ** -0.5)
        first = _lane_masks(q.shape)
        kb = k_ref[...].astype(BF16)
        vb = v_ref[...].astype(BF16)
        u = u_ref[...]
        if masked:
            qpos = q_off + i * tq + lax.broadcasted_iota(jnp.int32, (tq, tk), 0)
            kpos = j * tk + lax.broadcasted_iota(jnp.int32, (tq, tk), 1)
            causal = kpos < qpos
        r_max = None
        for qm, acc_ref, r_ref in ((jnp.where(first, q, 0.0), acc_e_ref, r_e_ref),
                                   (jnp.where(first, 0.0, q), acc_o_ref, r_o_ref)):
            z = _dot_nt(qm.astype(BF16), kb)
            sp = jnp.maximum(z, 0.0) + jnp.log(1.0 + jnp.exp(-jnp.abs(z)))
            log_1m = -sp
            if masked:
                log_1m = jnp.where(causal, log_1m, 0.0)
            hi, lo = _split_bf16(log_1m)
            after = _dot(hi, u) + _dot(lo, u)
            r_old = r_ref[...]
            w = jnp.exp((z - sp) + after + r_old)
            if masked:
                w = jnp.where(causal, w, 0.0)
            acc_ref[...] += _dot(w.astype(BF16), vb)
            r_new = r_old + after[:, 0:1] + log_1m[:, 0:1]
            r_ref[...] = r_new
            mx = jnp.max(r_new)
            r_max = mx if r_max is None else jnp.maximum(r_max, mx)
        done_ref[0] = (r_max < EXP_UNDERFLOW).astype(jnp.int32)

    live = jnp.logical_and(j >= 0, done_ref[0] == 0)
    needs_mask = j * tk + tk - 1 >= q_off + i * tq

    @pl.when(jnp.logical_and(live, jnp.logical_not(needs_mask)))
    def _():
        step(False)

    @pl.when(jnp.logical_and(live, needs_mask))
    def _():
        step(True)

    @pl.when(jj == pl.num_programs(3) - 1)
    def _():
        o_ref[...] = jnp.where(_lane_masks(o_ref.shape), acc_e_ref[...], acc_o_ref[...])


def _stick_attn(q_sel, k_sel, v_sel, *, q_off, tq, tk):
    (qa, qt), (ka, kt), (va, vt) = q_sel, k_sel, v_sel
    b, t_q = qa.shape[1], qa.shape[2]
    t_k = ka.shape[2]
    n_pairs = qa.shape[3] // LANES
    nq, nk = t_q // tq, t_k // tk
    upper = jnp.asarray((np.arange(tk)[:, None] > np.arange(tk)[None, :]).astype(np.float32), dtype=BF16)

    def kv_block(i, jj):
        j_start = jnp.minimum((q_off + i * tq + tq - 2) // tk, nk - 1)
        return jnp.maximum(j_start - jj, 0)

    return pl.pallas_call(
        functools.partial(_stick_kernel, q_off=q_off, tq=tq, tk=tk, nk=nk),
        grid=(b, n_pairs, nq, nk),
        in_specs=[
            pl.BlockSpec((tk, tk), lambda bi, h, i, jj: (0, 0)),
            pl.BlockSpec((None, None, tq, LANES), lambda bi, h, i, jj: (qt, bi, i, h)),
            pl.BlockSpec((None, None, tk, LANES), lambda bi, h, i, jj: (kt, bi, kv_block(i, jj), h)),
            pl.BlockSpec((None, None, tk, LANES), lambda bi, h, i, jj: (vt, bi, kv_block(i, jj), h)),
        ],
        out_specs=pl.BlockSpec((None, tq, LANES), lambda bi, h, i, jj: (bi, i, h)),
        out_shape=jax.ShapeDtypeStruct((b, t_q, n_pairs * LANES), F32),
        scratch_shapes=[pltpu.VMEM((tq, LANES), F32), pltpu.VMEM((tq, LANES), F32),
                        pltpu.VMEM((tq, 1), F32), pltpu.VMEM((tq, 1), F32), pltpu.SMEM((1,), jnp.int32)],
        compiler_params=_params("parallel", "parallel", "parallel", "arbitrary"),
        name="stick_attn",
    )(upper, qa, ka, va)


def _diff_seq_kernel(lam_ref, g_ref, q_ref, k_ref, vt_ref, o_ref,
                     m1_ref, l1_ref, acc1_ref, m2_ref, l2_ref, acc2_ref, *, t_blk, lam_init):
    i = pl.program_id(2)
    q = q_ref[...]
    first = _lane_masks(q.shape)
    zero = jnp.zeros_like(q)
    q_maps = (jnp.where(first, q, zero), jnp.where(first, zero, q))
    states = ((m1_ref, l1_ref, acc1_ref), (m2_ref, l2_ref, acc2_ref))
    for m_ref, l_ref, acc_ref in states:
        m_ref[...] = jnp.full_like(m_ref, NEG)
        l_ref[...] = jnp.zeros_like(l_ref)
        acc_ref[...] = jnp.zeros_like(acc_ref)

    def block(j, masked):
        kb = k_ref[j]
        vtb = vt_ref[j]
        if masked:
            vis = (lax.broadcasted_iota(jnp.int32, (t_blk, t_blk), 0) // CHUNK
                   <= lax.broadcasted_iota(jnp.int32, (t_blk, t_blk), 1) // CHUNK)
        for qm, (m_ref, l_ref, acc_ref) in zip(q_maps, states):
            st = _dot_nt(kb, qm)
            if masked:
                st = jnp.where(vis, st, NEG)
            m_old = m_ref[...]
            m_new = jnp.maximum(m_old, jnp.max(st, axis=0, keepdims=True))
            alpha = jnp.exp(m_old - m_new)
            pt = jnp.exp(st - m_new)
            l_ref[...] = alpha * l_ref[...] + jnp.sum(pt, axis=0, keepdims=True)
            acc_ref[...] = alpha * acc_ref[...] + _dot(vtb, pt.astype(BF16))
            m_ref[...] = m_new

    def body(j, carry):
        block(j, False)
        return carry

    lax.fori_loop(0, i, body, 0)
    block(i, True)

    lv = lam_ref[...]
    lam = (jnp.exp(jnp.sum(lv[0:1] * lv[1:2], axis=-1, keepdims=True))
           - jnp.exp(jnp.sum(lv[2:3] * lv[3:4], axis=-1, keepdims=True)) + lam_init)
    ot = acc1_ref[...] / l1_ref[...] - lam * (acc2_ref[...] / l2_ref[...])
    ot = ot * lax.rsqrt(jnp.mean(ot * ot, axis=0, keepdims=True) + RMS_EPS) * g_ref[...] * (1.0 - lam_init)
    o_ref[...] = ot.T


def _diff_attn_seq(lam_vecs, subln_g, pb, *, t_blk, lam_init):
    _, b, t, width = pb.shape
    n_heads = width // LANES
    n_blk = t // t_blk
    kv = pb.reshape(pb.shape[0], b, n_blk, t_blk, width)
    vt = jnp.transpose(pb[2].reshape(b, n_blk, t_blk, n_heads, LANES), (0, 3, 1, 4, 2))
    return pl.pallas_call(
        functools.partial(_diff_seq_kernel, t_blk=t_blk, lam_init=lam_init),
        grid=(b, n_heads, n_blk),
        in_specs=[
            pl.BlockSpec((4, HEAD_DIM), lambda bi, h, i: (0, 0)),
            pl.BlockSpec((LANES, 1), lambda bi, h, i: (0, 0)),
            pl.BlockSpec((None, None, t_blk, LANES), lambda bi, h, i: (0, bi, i, h)),
            pl.BlockSpec((None, None, n_blk, t_blk, LANES), lambda bi, h, i: (1, bi, 0, 0, h)),
            pl.BlockSpec((None, None, n_blk, LANES, t_blk), lambda bi, h, i: (bi, h, 0, 0, 0)),
        ],
        out_specs=pl.BlockSpec((None, t_blk, LANES), lambda bi, h, i: (bi, i, h)),
        out_shape=jax.ShapeDtypeStruct((b, t, width), F32),
        scratch_shapes=[pltpu.VMEM((1, t_blk), F32), pltpu.VMEM((1, t_blk), F32),
                        pltpu.VMEM((LANES, t_blk), F32)] * 2,
        compiler_params=_params("parallel", "parallel", "arbitrary"),
        name="diff_attn_seq",
    )(lam_vecs, subln_g.reshape(LANES, 1), pb, kv, vt)


def _stick_seq_kernel(u_ref, q_ref, k_ref, v_ref, o_ref, acc_e_ref, acc_o_ref, r_e_ref, r_o_ref, *, t_blk):
    i = pl.program_id(2)
    q = q_ref[...]
    first = _lane_masks(q.shape)
    zero = jnp.zeros_like(q)
    heads = ((jnp.where(first, q, zero), acc_e_ref, r_e_ref), (jnp.where(first, zero, q), acc_o_ref, r_o_ref))
    for _, acc_ref, r_ref in heads:
        acc_ref[...] = jnp.zeros_like(acc_ref)
        r_ref[...] = jnp.zeros_like(r_ref)
    u = u_ref[...]

    def block(j, masked):
        kb = k_ref[j]
        vb = v_ref[j]
        if masked:
            causal = (lax.broadcasted_iota(jnp.int32, (t_blk, t_blk), 1)
                      < lax.broadcasted_iota(jnp.int32, (t_blk, t_blk), 0))
        r_max = None
        for qm, acc_ref, r_ref in heads:
            z = _dot_nt(qm, kb)
            sp = jnp.maximum(z, 0.0) + jnp.log(1.0 + jnp.exp(-jnp.abs(z)))
            log_1m = -sp
            if masked:
                log_1m = jnp.where(causal, log_1m, 0.0)
            hi, lo = _split_bf16(log_1m)
            after = _dot(hi, u) + _dot(lo, u)
            r_old = r_ref[...]
            w = jnp.exp((z - sp) + after + r_old)
            if masked:
                w = jnp.where(causal, w, 0.0)
            acc_ref[...] += _dot(w.astype(BF16), vb)
            r_new = r_old + after[:, 0:1] + log_1m[:, 0:1]
            r_ref[...] = r_new
            mx = jnp.max(r_new)
            r_max = mx if r_max is None else jnp.maximum(r_max, mx)
        return r_max

    block(i, True)

    def cond(carry):
        j, done = carry
        return jnp.logical_and(j >= 0, done == 0)

    def body(carry):
        j, _ = carry
        r_max = block(j, False)
        return j - 1, (r_max < EXP_UNDERFLOW).astype(jnp.int32)

    lax.while_loop(cond, body, (i - 1, jnp.int32(0)))
    o_ref[...] = jnp.where(first, acc_e_ref[...], acc_o_ref[...])


def _stick_attn_seq(pb, *, t_blk):
    _, b, t, width = pb.shape
    n_pairs = width // LANES
    n_blk = t // t_blk
    kv = pb.reshape(pb.shape[0], b, n_blk, t_blk, width)
    upper = jnp.asarray((np.arange(t_blk)[:, None] > np.arange(t_blk)[None, :]).astype(np.float32), dtype=BF16)
    return pl.pallas_call(
        functools.partial(_stick_seq_kernel, t_blk=t_blk),
        grid=(b, n_pairs, n_blk),
        in_specs=[
            pl.BlockSpec((t_blk, t_blk), lambda bi, h, i: (0, 0)),
            pl.BlockSpec((None, None, t_blk, LANES), lambda bi, h, i: (3, bi, i, h)),
            pl.BlockSpec((None, None, n_blk, t_blk, LANES), lambda bi, h, i: (4, bi, 0, 0, h)),
            pl.BlockSpec((None, None, n_blk, t_blk, LANES), lambda bi, h, i: (5, bi, 0, 0, h)),
        ],
        out_specs=pl.BlockSpec((None, t_blk, LANES), lambda bi, h, i: (bi, i, h)),
        out_shape=jax.ShapeDtypeStruct((b, t, width), F32),
        scratch_shapes=[pltpu.VMEM((t_blk, LANES), F32), pltpu.VMEM((t_blk, LANES), F32),
                        pltpu.VMEM((t_blk, 1), F32), pltpu.VMEM((t_blk, 1), F32)],
        compiler_params=_params("parallel", "parallel", "arbitrary"),
        name="stick_attn_seq",
    )(upper, pb, kv, kv)


def _band_kernel(*refs, nb, tkb):
    bias_ref, q_ref = refs[0], refs[1]
    k_refs = refs[2:2 + nb]
    v_refs = refs[2 + nb:2 + 2 * nb]
    o_ref = refs[2 + 2 * nb]
    i = pl.program_id(2)
    q = q_ref[...]
    if q.dtype != BF16:
        q = (q * (HEAD_DIM ** -0.5)).astype(BF16)
    first = _lane_masks(q.shape)
    zero = jnp.zeros_like(q)
    kbs = [r[...].astype(BF16) for r in k_refs]
    vbs = [r[...].astype(BF16) for r in v_refs]
    outs = []
    for hd, qm in enumerate((jnp.where(first, q, zero), jnp.where(first, zero, q))):
        ss = []
        for d in range(nb):
            s = _dot_nt(qm, kbs[d]) + bias_ref[hd, :, d * tkb:(d + 1) * tkb]
            if d < nb - 1:
                s = jnp.where(i - (nb - 1) + d >= 0, s, NEG)
            ss.append(s)
        m = functools.reduce(jnp.maximum, [jnp.max(s, axis=-1, keepdims=True) for s in ss])
        ps = [jnp.exp(s - m) for s in ss]
        l = functools.reduce(lambda a, c: a + c, [jnp.sum(p, axis=-1, keepdims=True) for p in ps])
        o = functools.reduce(lambda a, c: a + c, [_dot(p.astype(BF16), vb) for p, vb in zip(ps, vbs)])
        outs.append(o / l)
    o_ref[...] = jnp.where(first, outs[0], outs[1])


def _band_bias(bias_table, qpos, kpos):
    tq, tk = len(qpos), len(kpos)
    n = tq + tk
    shift = np.arange(n)
    c_minus_r = np.where(shift < tk, shift, shift - n)
    rel = np.clip(int(qpos[0] - kpos[0]) - c_minus_r, -REL_CLIP, REL_CLIP) + REL_CLIP
    vec = bias_table.astype(F32)[:, rel]
    toeplitz = jnp.tile(vec, (1, tq))[:, :tq * (n - 1)].reshape(-1, tq, n - 1)[:, :, :tk]
    qc, kc = qpos[:, None] // CHUNK, kpos[None, :] // CHUNK
    vis = (kpos[None, :] >= 0) & (kc <= qc) & (qc - kc <= C_PAST_CHUNKS)
    return jnp.where(jnp.asarray(vis)[None], toeplitz, NEG)


def _band_attn(bias, q_sel, k_sel, v_sel, *, tq, nb, tkb):
    (qa, qt), (ka, kt), (va, vt) = q_sel, k_sel, v_sel
    b, t_q = qa.shape[1], qa.shape[2]
    per_tile = TILE_N // LANES
    n_pairs = bias.shape[0] // 2
    nq = t_q // tq

    def kv_spec(d, t0):
        return pl.BlockSpec((None, None, tkb, LANES),
                            lambda bi, h, i: (t0 + h // per_tile, bi, jnp.maximum(i - (nb - 1) + d, 0), h % per_tile))

    in_specs = [
        pl.BlockSpec((2, tq, nb * tkb), lambda bi, h, i: (h, 0, 0)),
        pl.BlockSpec((None, None, tq, LANES), lambda bi, h, i: (qt + h // per_tile, bi, i, h % per_tile)),
    ]
    in_specs += [kv_spec(d, kt) for d in range(nb)] + [kv_spec(d, vt) for d in range(nb)]
    return pl.pallas_call(
        functools.partial(_band_kernel, nb=nb, tkb=tkb),
        grid=(b, n_pairs, nq),
        in_specs=in_specs,
        out_specs=pl.BlockSpec((None, None, tq, LANES), lambda bi, h, i: (h // per_tile, bi, i, h % per_tile)),
        out_shape=jax.ShapeDtypeStruct((n_pairs // per_tile, b, t_q, TILE_N), F32),
        compiler_params=_params("parallel", "parallel", "parallel"),
        name="band_attn",
    )(bias, qa, *([ka] * nb), *([va] * nb))


def _lambda_init(layer):
    return 0.8 - 0.6 * math.exp(-0.3 * layer)


def _tile_gain(g, n):
    return jnp.tile(g.astype(F32), n)


def _trunk(x, pos0, mem_k, mem_v, past_ab, past_c, p):
    b, t, d = x.shape
    rows = b * t
    depth = p['ffn1_g'].shape[0]
    prompt = past_ab is None
    new_ab, new_c = [], []
    pos_rows = jnp.tile(pos0 + jnp.arange(t), b)
    x = x.reshape(rows, d)
    heads_per_tile = TILE_N // HEAD_DIM
    for l in range(depth):
        x = _ffn(x, p['ffn1_g'][l], p['ffn1_wg'][l], p['ffn1_wu'][l], p['ffn1_wd'][l])
        li = l // 2
        if l % 2 == 0:
            gains = jnp.stack([_tile_gain(p['a_gq'][li], heads_per_tile), _tile_gain(p['a_gk'][li], heads_per_tile)])
            proj, pb = _proj(x, p['mix_g'][l], p['ab_w_in'][li], gains, HEAD_DIM, _rope_tables(pos_rows),
                             q_tiles=(0, 3), q_scale=HEAD_DIM ** -0.5, name="proj_ab")
            proj = proj.reshape(6, b, t, TILE_N)
            pb = pb.reshape(6, b, t, TILE_N)
            new_ab.append((proj[1], proj[2], proj[4], proj[5]))
            lam_vecs = jnp.stack([p['a_lq1'][li], p['a_lk1'][li], p['a_lq2'][li], p['a_lk2'][li]]).astype(F32)
            lam_init = _lambda_init(l)
            if prompt:
                a_out = _diff_attn_seq(lam_vecs, p['a_subln_g'][li], pb, t_blk=min(512, t), lam_init=lam_init)
                b_out = _stick_attn_seq(pb, t_blk=min(256, t))
            else:
                pka, pva, pkb, pvb = (c[li].reshape(b, -1, TILE_N) for c in past_ab)
                past_len = pka.shape[1]
                ka = jnp.concatenate([pka, proj[1]], axis=1)[None]
                va = jnp.concatenate([pva, proj[2]], axis=1)[None]
                kb = jnp.concatenate([pkb, proj[4]], axis=1)[None]
                vb = jnp.concatenate([pvb, proj[5]], axis=1)[None]
                tk = past_len + t
                a_out = _diff_attn(lam_vecs, p['a_subln_g'][li], (proj, 0), (ka, 0), (va, 0),
                                   q_off=past_len, tq=t, tk=tk, lam_init=lam_init)
                b_out = _stick_attn((proj, 3), (kb, 0), (vb, 0), q_off=past_len, tq=t, tk=tk)
            w_out = p['ab_w_out'][li]
            half = w_out.shape[0] // 2
            x = _out_proj(x, [a_out.reshape(rows, -1), b_out.reshape(rows, -1)], [w_out[:half], w_out[half:]])
        else:
            gains = jnp.stack([_tile_gain(p['c_gq'][li], heads_per_tile)] * 2
                              + [_tile_gain(p['c_gk'][li], heads_per_tile)] * 2)
            proj, pb = _proj(x, p['mix_g'][l], p['c_w_in'][li], gains, HEAD_DIM,
                             q_tiles=(0, 1), q_scale=HEAD_DIM ** -0.5, name="proj_c")
            proj = proj.reshape(6, b, t, TILE_N)
            pb = pb.reshape(6, b, t, TILE_N)
            k_full = jnp.concatenate([proj[2], proj[3]], axis=-1)
            v_full = jnp.concatenate([proj[4], proj[5]], axis=-1)
            band = C_PAST_CHUNKS * CHUNK
            if prompt:
                tq = min(256, t)
                nb = band // tq + 1
                qpos = (nb - 1) * tq + np.arange(tq)
                kpos = np.arange(nb * tq)
                bias = _band_bias(p['c_bias'][li], qpos, kpos)
                o = _band_attn(bias, (pb, 0), (pb, 2), (pb, 4), tq=tq, nb=nb, tkb=tq)
                keep = min(band, t)
                new_c.append((k_full[:, t - keep:], v_full[:, t - keep:]))
            else:
                pk, pv = (c[li].reshape(b, -1, 2, TILE_N) for c in past_c)
                pc = pk.shape[1]
                k_all = jnp.concatenate([jnp.moveaxis(pk, 2, 0), proj[2:4]], axis=2)
                v_all = jnp.concatenate([jnp.moveaxis(pv, 2, 0), proj[4:6]], axis=2)
                qpos = pos0 + np.arange(t)
                kpos = pos0 - pc + np.arange(pc + t)
                bias = _band_bias(p['c_bias'][li], qpos, kpos)
                o = _band_attn(bias, (proj, 0), (k_all, 0), (v_all, 0), tq=t, nb=1, tkb=pc + t)
                new_c.append((jnp.concatenate([pk.reshape(b, pc, -1), k_full], axis=1)[:, -pc:],
                              jnp.concatenate([pv.reshape(b, pc, -1), v_full], axis=1)[:, -pc:]))
            w_out = p['c_w_out'][li]
            half = w_out.shape[0] // 2
            x = _out_proj(x, [o[0].reshape(rows, -1), o[1].reshape(rows, -1)], [w_out[:half], w_out[half:]])
        x = _mem_attn(x.reshape(b, t, d), p['mem_g_x'][l], p['mem_wq'][l], p['mem_gq'][l],
                      mem_k[l], mem_v[l], p['mem_wo'][l]).reshape(rows, d)
        x = _ffn(x, p['ffn2_g'][l], p['ffn2_wg'][l], p['ffn2_wu'][l], p['ffn2_wd'][l])
    return x.reshape(b, t, d), new_ab, new_c


def kernel(x_prompt, x_sample, cache_a_k, cache_a_v, cache_b_k, cache_b_v, cache_c_k, cache_c_v, cache_mem_k, cache_mem_v, mem_prompt, ffn1_g, ffn1_wg, ffn1_wu, ffn1_wd, ffn2_g, ffn2_wg, ffn2_wu, ffn2_wd, mix_g, ab_w_in, ab_w_out, a_gq, a_gk, a_lq1, a_lk1, a_lq2, a_lk2, a_subln_g, c_w_in, c_w_out, c_gq, c_gk, c_bias, mem_g_x, mem_g_m, mem_wq, mem_wk, mem_wv, mem_wo, mem_gq, mem_gk):
    depth = ffn1_g.shape[0]
    p = dict(ffn1_g=ffn1_g, ffn2_g=ffn2_g, mix_g=mix_g, a_gq=a_gq, a_gk=a_gk, a_lq1=a_lq1, a_lk1=a_lk1,
             a_lq2=a_lq2, a_lk2=a_lk2, a_subln_g=a_subln_g, c_gq=c_gq, c_gk=c_gk, c_bias=c_bias,
             mem_g_x=mem_g_x, mem_gq=mem_gq)
    for name, w in (('ffn1_wg', ffn1_wg), ('ffn1_wu', ffn1_wu), ('ffn1_wd', ffn1_wd), ('ffn2_wg', ffn2_wg),
                    ('ffn2_wu', ffn2_wu), ('ffn2_wd', ffn2_wd), ('ab_w_in', ab_w_in), ('ab_w_out', ab_w_out),
                    ('c_w_in', c_w_in), ('c_w_out', c_w_out), ('mem_wq', mem_wq), ('mem_wo', mem_wo)):
        p[name] = w.astype(BF16)

    bp, tp, d = x_prompt.shape
    bs, ts, _ = x_sample.shape
    m_len = mem_prompt.shape[1]
    past_len = cache_a_k.shape[2]

    heads_per_tile = TILE_N // MEM_HEAD_DIM
    mem_k_list, mem_v_list = [], []
    for l in range(depth):
        w_kv = jnp.concatenate([mem_wk[l], mem_wv[l]], axis=1).astype(BF16)
        gains = jnp.stack([_tile_gain(mem_gk[l], heads_per_tile)] * (d // TILE_N))
        kv, _ = _proj(mem_prompt.reshape(bp * m_len, d), mem_g_m[l], w_kv, gains, MEM_HEAD_DIM, name="proj_mem_kv")
        kv = kv.reshape(2, d // TILE_N, bp, m_len, TILE_N)
        kv = jnp.moveaxis(kv, 1, 3).reshape(2, bp, m_len, d)
        mem_k_list.append(kv[0])
        mem_v_list.append(kv[1])

    y_prompt, ab_p, c_p = _trunk(x_prompt, 0, mem_k_list, mem_v_list, None, None, p)
    cache_mem_k2 = cache_mem_k.reshape(depth, bs, m_len, d)
    cache_mem_v2 = cache_mem_v.reshape(depth, bs, m_len, d)
    y_sample, ab_s, c_s = _trunk(x_sample, past_len, cache_mem_k2, cache_mem_v2,
                                 (cache_a_k, cache_a_v, cache_b_k, cache_b_v), (cache_c_k, cache_c_v), p)

    a_heads = cache_a_v.shape[3]
    b_heads = cache_b_k.shape[3]
    c_heads = cache_c_k.shape[3]

    def stack(rows, idx, heads, width):
        return jnp.stack([r[idx].reshape(r[idx].shape[0], r[idx].shape[1], heads, width) for r in rows])

    outs = [y_prompt, y_sample]
    outs += [stack(ab_p, 0, 2 * a_heads, HEAD_DIM), stack(ab_p, 1, a_heads, 2 * HEAD_DIM),
             stack(ab_p, 2, b_heads, HEAD_DIM), stack(ab_p, 3, b_heads, HEAD_DIM),
             stack(c_p, 0, c_heads, HEAD_DIM), stack(c_p, 1, c_heads, HEAD_DIM)]
    outs += [jnp.stack(mem_k_list).reshape(depth, bp, m_len, MEM_HEADS, MEM_HEAD_DIM),
             jnp.stack(mem_v_list).reshape(depth, bp, m_len, MEM_HEADS, MEM_HEAD_DIM)]
    outs += [stack(ab_s, 0, 2 * a_heads, HEAD_DIM), stack(ab_s, 1, a_heads, 2 * HEAD_DIM),
             stack(ab_s, 2, b_heads, HEAD_DIM), stack(ab_s, 3, b_heads, HEAD_DIM),
             stack(c_s, 0, c_heads, HEAD_DIM), stack(c_s, 1, c_heads, HEAD_DIM)]
    return tuple(outs)
```

```python
import functools
import math

import jax
import jax.numpy as jnp
import numpy as np
from jax import lax
from jax.experimental import pallas as pl
from jax.experimental.pallas import tpu as pltpu

D_MODEL = 1024
CHUNK = 64
HEAD_DIM = 64
ROT_DIM = HEAD_DIM // 4
ROPE_THETA = 500000.0
C_PAST_CHUNKS = 8
REL_CLIP = 128
MEM_HEADS = 4
MEM_HEAD_DIM = D_MODEL // MEM_HEADS
RMS_EPS = 1e-6
NEG = -1e30
LOG2E = 1.4426950408889634
LANES = 128
TILE_N = 512
SEQ_BLOCK = 512
STICK_BLOCK = 256
EXP_UNDERFLOW = -104.0
VMEM_LIMIT = 56 * 1024 * 1024
SOFTMAX_Q_SCALE = HEAD_DIM ** -0.5 * LOG2E
STICK_Q_SCALE = HEAD_DIM ** -0.5

BF16 = jnp.bfloat16
F32 = jnp.float32


def _dot(a, b):
    return jnp.dot(a, b, preferred_element_type=F32)


def _dot_nt(a, b):
    return lax.dot_general(a, b, (((1,), (1,)), ((), ())), preferred_element_type=F32)


def _rms(x, g):
    return x * lax.rsqrt(jnp.mean(x * x, axis=-1, keepdims=True) + RMS_EPS) * g


def _split_bf16(x):
    hi = x.astype(BF16)
    lo = (x - hi.astype(F32)).astype(BF16)
    return hi, lo


def _params(*sem):
    return pltpu.CompilerParams(dimension_semantics=sem, vmem_limit_bytes=VMEM_LIMIT)


def _lane_masks(shape):
    lane = lax.broadcasted_iota(jnp.int32, shape, 1)
    return lane < HEAD_DIM


def _head_pair_queries(q):
    first = _lane_masks(q.shape)
    zero = jnp.zeros_like(q)
    return jnp.where(first, q, zero), jnp.where(first, zero, q)


def _lambda(lam_ref, lam_init):
    lv = lam_ref[...]
    return (jnp.exp(jnp.sum(lv[0:1] * lv[1:2], axis=-1, keepdims=True))
            - jnp.exp(jnp.sum(lv[2:3] * lv[3:4], axis=-1, keepdims=True)) + lam_init)


def _ffn_kernel(x_ref, g_ref, wg_ref, wu_ref, wd_ref, o_ref, h_ref, acc_ref):
    f = pl.program_id(1)

    @pl.when(f == 0)
    def _():
        h_ref[...] = _rms(x_ref[...], g_ref[...]).astype(BF16)
        acc_ref[...] = jnp.zeros_like(acc_ref)

    h = h_ref[...]
    a = _dot(h, wg_ref[...])
    u = _dot(h, wu_ref[...])
    act = a * (1.0 / (1.0 + jnp.exp(-a))) * u
    acc_ref[...] += _dot(act.astype(BF16), wd_ref[...])

    @pl.when(f == pl.num_programs(1) - 1)
    def _():
        o_ref[...] = x_ref[...] + 0.5 * acc_ref[...]


def _ffn(x, g, wg, wu, wd):
    rows, d = x.shape
    ff = wg.shape[1]
    tm = min(512, rows)
    tf = ff // 2 if (ff // 2) % LANES == 0 else ff
    return pl.pallas_call(
        _ffn_kernel,
        grid=(rows // tm, ff // tf),
        in_specs=[
            pl.BlockSpec((tm, d), lambda i, f: (i, 0)),
            pl.BlockSpec((1, d), lambda i, f: (0, 0)),
            pl.BlockSpec((d, tf), lambda i, f: (0, f)),
            pl.BlockSpec((d, tf), lambda i, f: (0, f)),
            pl.BlockSpec((tf, d), lambda i, f: (f, 0)),
        ],
        out_specs=pl.BlockSpec((tm, d), lambda i, f: (i, 0)),
        out_shape=jax.ShapeDtypeStruct((rows, d), F32),
        scratch_shapes=[pltpu.VMEM((tm, d), BF16), pltpu.VMEM((tm, d), F32)],
        compiler_params=_params("parallel", "arbitrary"),
        name="ffn",
    )(x, g.reshape(1, d), wg, wu, wd)


def _proj_kernel(*refs, tiles, rope):
    x_ref, g_ref, w_ref, gain_ref, seg_ref = refs[:5]
    n_in = 8 if rope else 5
    out_refs = refs[n_in:]
    h = _rms(x_ref[...], g_ref[...]).astype(BF16)
    for t, (norm, use_rope, dests) in enumerate(tiles):
        y = _dot(h, w_ref[:, t * TILE_N:(t + 1) * TILE_N])
        if norm is not None:
            hi, lo = _split_bf16(y * y)
            ms = _dot(hi, seg_ref[...]) + _dot(lo, seg_ref[...])
            y = y * lax.rsqrt(ms + RMS_EPS) * gain_ref[norm:norm + 1, :]
        if use_rope:
            cos, sa, sb = refs[5][...], refs[6][...], refs[7][...]
            half = ROT_DIM // 2
            blocks = []
            for c in range(TILE_N // LANES):
                yb = y[:, c * LANES:(c + 1) * LANES]
                blocks.append(yb * cos + pltpu.roll(yb, LANES - half, 1) * sa + pltpu.roll(yb, half, 1) * sb)
            y = jnp.concatenate(blocks, axis=-1)
        for dest in dests:
            ref = out_refs[dest[1]]
            if dest[0] == "copy":
                _, _, col, scale = dest
                ref[:, col:col + TILE_N] = (y if scale == 1.0 else y * scale).astype(ref.dtype)
            else:
                ref[...] = y.T.reshape(ref.shape).astype(ref.dtype)


def _seg_matrix(seg):
    idx = np.arange(TILE_N) // seg
    return jnp.asarray((idx[:, None] == idx[None, :]).astype(np.float32) / seg, dtype=BF16)


def _proj(x, g, w, gains, seg, tiles, outs, rope_tables=None, name="proj"):
    rows, d = x.shape
    tm = min(SEQ_BLOCK, rows)
    rope = rope_tables is not None
    in_specs = [
        pl.BlockSpec((tm, d), lambda i: (i, 0)),
        pl.BlockSpec((1, d), lambda i: (0, 0)),
        pl.BlockSpec(w.shape, lambda i: (0, 0)),
        pl.BlockSpec(gains.shape, lambda i: (0, 0)),
        pl.BlockSpec((TILE_N, TILE_N), lambda i: (0, 0)),
    ]
    args = [x, g.reshape(1, d), w, gains, _seg_matrix(seg)]
    if rope:
        in_specs += [pl.BlockSpec((tm, LANES), lambda i: (i, 0))] * 3
        args += list(rope_tables)
    out_specs, out_shape = [], []
    for o in outs:
        if o[0] == "rows":
            _, width, dtype = o
            out_specs.append(pl.BlockSpec((tm, width), lambda i: (i, 0)))
            out_shape.append(jax.ShapeDtypeStruct((rows, width), dtype))
        else:
            groups = TILE_N // LANES
            out_specs.append(pl.BlockSpec((groups, None, LANES, tm), lambda i: (0, i, 0, 0)))
            out_shape.append(jax.ShapeDtypeStruct((groups, rows // tm, LANES, tm), o[1]))
    return pl.pallas_call(
        functools.partial(_proj_kernel, tiles=tuple(tiles), rope=rope),
        grid=(rows // tm,),
        in_specs=in_specs,
        out_specs=out_specs,
        out_shape=out_shape,
        compiler_params=_params("parallel"),
        name=name,
    )(*args)


def _rope_tables(pos):
    half = ROT_DIM // 2
    rows = pos.shape[0]
    inv_freq = ROPE_THETA ** (-2.0 * jnp.arange(half, dtype=F32) / ROT_DIM)
    ang = pos.astype(F32)[:, None] * inv_freq[None, :]
    c, s = jnp.cos(ang), jnp.sin(ang)
    rest = HEAD_DIM - ROT_DIM
    cos = jnp.concatenate([c, c, jnp.ones((rows, rest), F32)], axis=-1)
    sa = jnp.concatenate([-s, jnp.zeros((rows, HEAD_DIM - half), F32)], axis=-1)
    sb = jnp.concatenate([jnp.zeros((rows, half), F32), s, jnp.zeros((rows, rest), F32)], axis=-1)
    rep = LANES // HEAD_DIM
    return tuple(jnp.tile(t, (1, rep)) for t in (cos, sa, sb))


def _out_kernel(*refs, n_parts):
    x_ref = refs[0]
    o_ref = refs[1 + 2 * n_parts]
    acc = x_ref[...]
    for p in range(n_parts):
        acc = acc + _dot(refs[1 + p][...], refs[1 + n_parts + p][...])
    o_ref[...] = acc


def _out_proj(x, parts, w_parts):
    rows, d = x.shape
    tm = min(512, rows)
    n_parts = len(parts)
    in_specs = [pl.BlockSpec((tm, d), lambda i: (i, 0))]
    in_specs += [pl.BlockSpec((tm, p.shape[1]), lambda i: (i, 0)) for p in parts]
    in_specs += [pl.BlockSpec(w.shape, lambda i: (0, 0)) for w in w_parts]
    return pl.pallas_call(
        functools.partial(_out_kernel, n_parts=n_parts),
        grid=(rows // tm,),
        in_specs=in_specs,
        out_specs=pl.BlockSpec((tm, d), lambda i: (i, 0)),
        out_shape=jax.ShapeDtypeStruct((rows, d), F32),
        compiler_params=_params("parallel"),
        name="out_proj",
    )(x, *parts, *w_parts)


def _mem_kernel(x_ref, g_ref, wq_ref, gq_ref, k_ref, v_ref, wo_ref, o_ref):
    x = x_ref[...]
    h = _rms(x, g_ref[...]).astype(BF16)
    q = _dot(h, wq_ref[...])
    gq = gq_ref[...]
    scale = MEM_HEAD_DIM ** -0.5
    outs = []
    for hd in range(MEM_HEADS):
        sl = slice(hd * MEM_HEAD_DIM, (hd + 1) * MEM_HEAD_DIM)
        qh = _rms(q[:, sl], gq) * scale
        s = _dot_nt(qh.astype(BF16), k_ref[:, sl].astype(BF16))
        m = jnp.max(s, axis=-1, keepdims=True)
        p = jnp.exp(s - m)
        l = jnp.sum(p, axis=-1, keepdims=True)
        oh = _dot(p.astype(BF16), v_ref[:, sl].astype(BF16)) / l
        outs.append(oh.astype(BF16))
    o = jnp.concatenate(outs, axis=-1)
    o_ref[...] = x + _dot(o, wo_ref[...])


def _mem_attn(x, g, wq, gq, mk, mv, wo):
    b, t, d = x.shape
    m = mk.shape[1]
    tm = min(512, t)
    return pl.pallas_call(
        _mem_kernel,
        grid=(b, t // tm),
        in_specs=[
            pl.BlockSpec((None, tm, d), lambda bi, i: (bi, i, 0)),
            pl.BlockSpec((1, d), lambda bi, i: (0, 0)),
            pl.BlockSpec((d, d), lambda bi, i: (0, 0)),
            pl.BlockSpec((1, MEM_HEAD_DIM), lambda bi, i: (0, 0)),
            pl.BlockSpec((None, m, d), lambda bi, i: (bi, 0, 0)),
            pl.BlockSpec((None, m, d), lambda bi, i: (bi, 0, 0)),
            pl.BlockSpec((d, d), lambda bi, i: (0, 0)),
        ],
        out_specs=pl.BlockSpec((None, tm, d), lambda bi, i: (bi, i, 0)),
        out_shape=jax.ShapeDtypeStruct((b, t, d), F32),
        compiler_params=_params("parallel", "parallel"),
        name="mem_attn",
    )(x, g.reshape(1, d), wq, gq.reshape(1, MEM_HEAD_DIM), mk, mv, wo)


def _diff_hist_kernel(lam_ref, g_ref, q_ref, k_ref, v_ref, o_ref, *, q_off, lam_init):
    q = q_ref[...]
    tq, tk = q.shape[0], k_ref.shape[0]
    kb = k_ref[...].astype(BF16)
    vb = v_ref[...].astype(BF16)
    vis = (lax.broadcasted_iota(jnp.int32, (tq, tk), 1) // CHUNK
           <= (q_off + lax.broadcasted_iota(jnp.int32, (tq, tk), 0)) // CHUNK)
    outs = []
    for qm in _head_pair_queries(q):
        s = jnp.where(vis, _dot_nt(qm, kb), NEG)
        p = jnp.exp2(s - jnp.max(s, axis=-1, keepdims=True))
        outs.append(_dot(p.astype(BF16), vb) / jnp.sum(p, axis=-1, keepdims=True))
    o = outs[0] - _lambda(lam_ref, lam_init) * outs[1]
    o_ref[...] = (_rms(o, g_ref[...]) * (1.0 - lam_init)).astype(o_ref.dtype)


def _diff_attn_hist(lam_vecs, subln_g, q, k, v, *, q_off, lam_init):
    b, tq, width = q.shape
    tk = k.shape[1]
    return pl.pallas_call(
        functools.partial(_diff_hist_kernel, q_off=q_off, lam_init=lam_init),
        grid=(b, width // LANES),
        in_specs=[
            pl.BlockSpec((4, HEAD_DIM), lambda bi, h: (0, 0)),
            pl.BlockSpec((1, LANES), lambda bi, h: (0, 0)),
            pl.BlockSpec((None, tq, LANES), lambda bi, h: (bi, 0, h)),
            pl.BlockSpec((None, tk, LANES), lambda bi, h: (bi, 0, h)),
            pl.BlockSpec((None, tk, LANES), lambda bi, h: (bi, 0, h)),
        ],
        out_specs=pl.BlockSpec((None, tq, LANES), lambda bi, h: (bi, 0, h)),
        out_shape=jax.ShapeDtypeStruct((b, tq, width), BF16),
        compiler_params=_params("parallel", "parallel"),
        name="diff_attn_hist",
    )(lam_vecs, subln_g.reshape(1, LANES), q, k, v)


def _diff_seq_kernel(lam_ref, g_ref, q_ref, k_ref, vt_ref, o_ref, m_ref, l_ref, acc_ref, sa_ref, sb_ref,
                     *, t_blk, lam_init):
    i = pl.program_id(2)
    q2 = jnp.concatenate(_head_pair_queries(q_ref[...]), axis=0)
    m_ref[...] = jnp.full_like(m_ref, NEG)
    l_ref[...] = jnp.zeros_like(l_ref)
    acc_ref[...] = jnp.zeros_like(acc_ref)

    def scores(j, dst_ref):
        dst_ref[...] = _dot_nt(k_ref[j], q2)

    def update(src_ref, j, masked):
        st = src_ref[...]
        if masked:
            vis = (lax.broadcasted_iota(jnp.int32, (t_blk, 2 * t_blk), 0) // CHUNK
                   <= (lax.broadcasted_iota(jnp.int32, (t_blk, 2 * t_blk), 1) % t_blk) // CHUNK)
            st = jnp.where(vis, st, NEG)
        m_old = m_ref[...]
        m_new = jnp.maximum(m_old, jnp.max(st, axis=0, keepdims=True))
        alpha = jnp.exp2(m_old - m_new)
        pt = jnp.exp2(st - m_new)
        l_ref[...] = alpha * l_ref[...] + jnp.sum(pt, axis=0, keepdims=True)
        acc_ref[...] = alpha * acc_ref[...] + _dot(vt_ref[j], pt.astype(BF16))
        m_ref[...] = m_new

    scores(0, sa_ref)

    def body(jj, carry):
        j = 2 * jj
        scores(j + 1, sb_ref)
        update(sa_ref, j, False)
        scores(j + 2, sa_ref)
        update(sb_ref, j + 1, False)
        return carry

    lax.fori_loop(0, i // 2, body, 0)

    @pl.when(i % 2 == 1)
    def _():
        scores(i, sb_ref)
        update(sa_ref, i - 1, False)
        update(sb_ref, i, True)

    @pl.when(i % 2 == 0)
    def _():
        update(sa_ref, i, True)

    o12 = acc_ref[...] / l_ref[...]
    ot = o12[:, :t_blk] - _lambda(lam_ref, lam_init) * o12[:, t_blk:]
    ot = ot * lax.rsqrt(jnp.mean(ot * ot, axis=0, keepdims=True) + RMS_EPS) * g_ref[...] * (1.0 - lam_init)
    o_ref[...] = ot.T.astype(o_ref.dtype)


def _diff_attn_seq(lam_vecs, subln_g, q, k, vt, *, t_blk, lam_init):
    b, t, width = q.shape
    n_heads = width // LANES
    n_blk = t // t_blk
    return pl.pallas_call(
        functools.partial(_diff_seq_kernel, t_blk=t_blk, lam_init=lam_init),
        grid=(b, n_heads, n_blk),
        in_specs=[
            pl.BlockSpec((4, HEAD_DIM), lambda bi, h, i: (0, 0)),
            pl.BlockSpec((LANES, 1), lambda bi, h, i: (0, 0)),
            pl.BlockSpec((None, t_blk, LANES), lambda bi, h, i: (bi, i, h)),
            pl.BlockSpec((None, n_blk, t_blk, LANES), lambda bi, h, i: (bi, 0, 0, h)),
            pl.BlockSpec((None, None, n_blk, LANES, t_blk), lambda bi, h, i: (h, bi, 0, 0, 0)),
        ],
        out_specs=pl.BlockSpec((None, t_blk, LANES), lambda bi, h, i: (bi, i, h)),
        out_shape=jax.ShapeDtypeStruct((b, t, width), BF16),
        scratch_shapes=[pltpu.VMEM((1, 2 * t_blk), F32), pltpu.VMEM((1, 2 * t_blk), F32),
                        pltpu.VMEM((LANES, 2 * t_blk), F32),
                        pltpu.VMEM((t_blk, 2 * t_blk), F32), pltpu.VMEM((t_blk, 2 * t_blk), F32)],
        compiler_params=_params("parallel", "parallel", "arbitrary"),
        name="diff_attn_seq",
    )(lam_vecs, subln_g.reshape(LANES, 1), q, k.reshape(b, n_blk, t_blk, width), vt)


def _upper(n):
    return jnp.asarray((np.arange(n)[:, None] > np.arange(n)[None, :]).astype(np.float32), dtype=BF16)


def _stick_block(qm, kb, vb, u, causal, r_old):
    z = _dot_nt(qm, kb)
    sp = jnp.maximum(z, 0.0) + jnp.log(1.0 + jnp.exp(-jnp.abs(z)))
    log_1m = -sp
    if causal is not None:
        log_1m = jnp.where(causal, log_1m, 0.0)
    hi, lo = _split_bf16(log_1m)
    after = _dot(hi, u) + _dot(lo, u)
    w = jnp.exp((z - sp) + after + r_old)
    if causal is not None:
        w = jnp.where(causal, w, 0.0)
    return _dot(w.astype(BF16), vb), after[:, 0:1] + log_1m[:, 0:1]


def _stick_hist_kernel(u_ref, q_ref, k_ref, v_ref, o_ref, *, q_off):
    q = q_ref[...]
    tq, tk = q.shape[0], k_ref.shape[0]
    kb = k_ref[...].astype(BF16)
    vb = v_ref[...].astype(BF16)
    causal = (lax.broadcasted_iota(jnp.int32, (tq, tk), 1)
              < q_off + lax.broadcasted_iota(jnp.int32, (tq, tk), 0))
    outs = [_stick_block(qm, kb, vb, u_ref[...], causal, 0.0)[0] for qm in _head_pair_queries(q)]
    o_ref[...] = jnp.where(_lane_masks(o_ref.shape), outs[0], outs[1]).astype(o_ref.dtype)


def _stick_attn_hist(q, k, v, *, q_off):
    b, tq, width = q.shape
    tk = k.shape[1]
    return pl.pallas_call(
        functools.partial(_stick_hist_kernel, q_off=q_off),
        grid=(b, width // LANES),
        in_specs=[
            pl.BlockSpec((tk, tk), lambda bi, h: (0, 0)),
            pl.BlockSpec((None, tq, LANES), lambda bi, h: (bi, 0, h)),
            pl.BlockSpec((None, tk, LANES), lambda bi, h: (bi, 0, h)),
            pl.BlockSpec((None, tk, LANES), lambda bi, h: (bi, 0, h)),
        ],
        out_specs=pl.BlockSpec((None, tq, LANES), lambda bi, h: (bi, 0, h)),
        out_shape=jax.ShapeDtypeStruct((b, tq, width), BF16),
        compiler_params=_params("parallel", "parallel"),
        name="stick_attn_hist",
    )(_upper(tk), q, k, v)


def _stick_seq_kernel(u_ref, q_ref, k_ref, v_ref, o_ref, acc_e_ref, acc_o_ref, r_e_ref, r_o_ref, *, t_blk):
    i = pl.program_id(2)
    heads = tuple(zip(_head_pair_queries(q_ref[...]), (acc_e_ref, acc_o_ref), (r_e_ref, r_o_ref)))
    for _, acc_ref, r_ref in heads:
        acc_ref[...] = jnp.zeros_like(acc_ref)
        r_ref[...] = jnp.zeros_like(r_ref)
    u = u_ref[...]

    def block(j, masked):
        kb = k_ref[j]
        vb = v_ref[j]
        causal = None
        if masked:
            causal = (lax.broadcasted_iota(jnp.int32, (t_blk, t_blk), 1)
                      < lax.broadcasted_iota(jnp.int32, (t_blk, t_blk), 0))
        r_max = None
        for qm, acc_ref, r_ref in heads:
            r_old = r_ref[...]
            wv, r_blk = _stick_block(qm, kb, vb, u, causal, r_old)
            acc_ref[...] += wv
            r_new = r_old + r_blk
            r_ref[...] = r_new
            mx = jnp.max(r_new)
            r_max = mx if r_max is None else jnp.maximum(r_max, mx)
        return r_max

    block(i, True)

    def cond(carry):
        j, done = carry
        return jnp.logical_and(j >= 0, done == 0)

    def body(carry):
        j, _ = carry
        r_max = block(j, False)
        return j - 1, (r_max < EXP_UNDERFLOW).astype(jnp.int32)

    lax.while_loop(cond, body, (i - 1, jnp.int32(0)))
    o_ref[...] = jnp.where(_lane_masks(o_ref.shape), acc_e_ref[...], acc_o_ref[...]).astype(o_ref.dtype)


def _stick_attn_seq(q, k, v, *, t_blk):
    b, t, width = q.shape
    n_blk = t // t_blk
    kv_spec = pl.BlockSpec((None, n_blk, t_blk, LANES), lambda bi, h, i: (bi, 0, 0, h))
    return pl.pallas_call(
        functools.partial(_stick_seq_kernel, t_blk=t_blk),
        grid=(b, width // LANES, n_blk),
        in_specs=[
            pl.BlockSpec((t_blk, t_blk), lambda bi, h, i: (0, 0)),
            pl.BlockSpec((None, t_blk, LANES), lambda bi, h, i: (bi, i, h)),
            kv_spec, kv_spec,
        ],
        out_specs=pl.BlockSpec((None, t_blk, LANES), lambda bi, h, i: (bi, i, h)),
        out_shape=jax.ShapeDtypeStruct((b, t, width), BF16),
        scratch_shapes=[pltpu.VMEM((t_blk, LANES), F32), pltpu.VMEM((t_blk, LANES), F32),
                        pltpu.VMEM((t_blk, 1), F32), pltpu.VMEM((t_blk, 1), F32)],
        compiler_params=_params("parallel", "parallel", "arbitrary"),
        name="stick_attn_seq",
    )(_upper(t_blk), q, k.reshape(b, n_blk, t_blk, width), v.reshape(b, n_blk, t_blk, width))


def _band_kernel(*refs, nb, tkb):
    bias_ref, q_ref = refs[0], refs[1]
    k_refs = refs[2:2 + nb]
    v_refs = refs[2 + nb:2 + 2 * nb]
    o_ref = refs[2 + 2 * nb]
    i = pl.program_id(2)
    kbs = [r[...].astype(BF16) for r in k_refs]
    vbs = [r[...].astype(BF16) for r in v_refs]
    outs = []
    for hd, qm in enumerate(_head_pair_queries(q_ref[...])):
        ss = []
        for d in range(nb):
            s = _dot_nt(qm, kbs[d]) + bias_ref[hd, :, d * tkb:(d + 1) * tkb]
            if d < nb - 1:
                s = jnp.where(i - (nb - 1) + d >= 0, s, NEG)
            ss.append(s)
        m = functools.reduce(jnp.maximum, [jnp.max(s, axis=-1, keepdims=True) for s in ss])
        ps = [jnp.exp2(s - m) for s in ss]
        l = functools.reduce(lambda a, c: a + c, [jnp.sum(p, axis=-1, keepdims=True) for p in ps])
        o = functools.reduce(lambda a, c: a + c, [_dot(p.astype(BF16), vb) for p, vb in zip(ps, vbs)])
        outs.append(o / l)
    o_ref[...] = jnp.where(_lane_masks(o_ref.shape), outs[0], outs[1]).astype(o_ref.dtype)


def _band_bias(bias_table, qpos, kpos):
    tq, tk = len(qpos), len(kpos)
    n = tq + tk
    shift = np.arange(n)
    c_minus_r = np.where(shift < tk, shift, shift - n)
    rel = np.clip(int(qpos[0] - kpos[0]) - c_minus_r, -REL_CLIP, REL_CLIP) + REL_CLIP
    vec = bias_table.astype(F32)[:, rel] * LOG2E
    toeplitz = jnp.tile(vec, (1, tq))[:, :tq * (n - 1)].reshape(-1, tq, n - 1)[:, :, :tk]
    qc, kc = qpos[:, None] // CHUNK, kpos[None, :] // CHUNK
    vis = (kpos[None, :] >= 0) & (kc <= qc) & (qc - kc <= C_PAST_CHUNKS)
    return jnp.where(jnp.asarray(vis)[None], toeplitz, NEG)


def _band_attn(bias, q, k, v, *, tq, nb, tkb):
    b, t_q, width = q.shape
    n_pairs = width // LANES

    def kv_spec(d):
        return pl.BlockSpec((None, tkb, LANES), lambda bi, h, i: (bi, jnp.maximum(i - (nb - 1) + d, 0), h))

    in_specs = [
        pl.BlockSpec((2, tq, nb * tkb), lambda bi, h, i: (h, 0, 0)),
        pl.BlockSpec((None, tq, LANES), lambda bi, h, i: (bi, i, h)),
    ]
    in_specs += [kv_spec(d) for d in range(nb)] * 2
    return pl.pallas_call(
        functools.partial(_band_kernel, nb=nb, tkb=tkb),
        grid=(b, n_pairs, t_q // tq),
        in_specs=in_specs,
        out_specs=pl.BlockSpec((None, tq, LANES), lambda bi, h, i: (bi, i, h)),
        out_shape=jax.ShapeDtypeStruct((b, t_q, width), BF16),
        compiler_params=_params("parallel", "parallel", "parallel"),
        name="band_attn",
    )(bias, q, *([k] * nb), *([v] * nb))


def _lambda_init(layer):
    return 0.8 - 0.6 * math.exp(-0.3 * layer)


def _tile_gain(g, n):
    return jnp.tile(g.astype(F32), n)


def _trunk(x, pos0, mem_k, mem_v, past_ab, past_c, p):
    b, t, d = x.shape
    rows = b * t
    depth = p['ffn1_g'].shape[0]
    seq = past_ab is None
    new_ab, new_c = [], []
    pos_rows = jnp.tile(pos0 + jnp.arange(t), b)
    x = x.reshape(rows, d)
    heads_per_tile = TILE_N // HEAD_DIM
    for l in range(depth):
        x = _ffn(x, p['ffn1_g'][l], p['ffn1_wg'][l], p['ffn1_wu'][l], p['ffn1_wd'][l])
        li = l // 2
        if l % 2 == 0:
            gains = jnp.stack([_tile_gain(p['a_gq'][li], heads_per_tile), _tile_gain(p['a_gk'][li], heads_per_tile)])
            outs = [("rows", TILE_N, F32)] * 4 + [("rows", TILE_N, BF16)] * 5
            v_dests = (("copy", 1, 0, 1.0),)
            if seq:
                outs = outs + [("heads_t", BF16)]
                v_dests += (("heads_t", 9),)
            tiles = [
                (0, True, (("copy", 4, 0, SOFTMAX_Q_SCALE),)),
                (1, True, (("copy", 0, 0, 1.0), ("copy", 5, 0, 1.0))),
                (None, False, v_dests),
                (None, False, (("copy", 6, 0, STICK_Q_SCALE),)),
                (None, False, (("copy", 2, 0, 1.0), ("copy", 7, 0, 1.0))),
                (None, False, (("copy", 3, 0, 1.0), ("copy", 8, 0, 1.0))),
            ]
            res = _proj(x, p['mix_g'][l], p['ab_w_in'][li], gains, HEAD_DIM, tiles, outs,
                        rope_tables=_rope_tables(pos_rows), name="proj_ab")
            a_k, a_v, b_k, b_v = (r.reshape(b, t, TILE_N) for r in res[:4])
            qa, ka, qb, kb, vb = (r.reshape(b, t, TILE_N) for r in res[4:9])
            new_ab.append((a_k, a_v, b_k, b_v))
            lam_vecs = jnp.stack([p['a_lq1'][li], p['a_lk1'][li], p['a_lq2'][li], p['a_lk2'][li]]).astype(F32)
            lam_init = _lambda_init(l)
            if seq:
                t_blk = min(SEQ_BLOCK, t)
                vt = res[9].reshape(TILE_N // LANES, b, t // t_blk, LANES, t_blk)
                a_out = _diff_attn_seq(lam_vecs, p['a_subln_g'][li], qa, ka, vt, t_blk=t_blk, lam_init=lam_init)
                b_out = _stick_attn_seq(qb, kb, vb, t_blk=min(STICK_BLOCK, t))
            else:
                pka, pva, pkb, pvb = (c[li].reshape(b, -1, TILE_N) for c in past_ab)
                past_len = pka.shape[1]
                a_out = _diff_attn_hist(lam_vecs, p['a_subln_g'][li], qa,
                                        jnp.concatenate([pka, a_k], axis=1), jnp.concatenate([pva, a_v], axis=1),
                                        q_off=past_len, lam_init=lam_init)
                b_out = _stick_attn_hist(qb, jnp.concatenate([pkb, b_k], axis=1),
                                         jnp.concatenate([pvb, b_v], axis=1), q_off=past_len)
            w_out = p['ab_w_out'][li]
            half = w_out.shape[0] // 2
            x = _out_proj(x, [a_out.reshape(rows, -1), b_out.reshape(rows, -1)], [w_out[:half], w_out[half:]])
        else:
            gains = jnp.stack([_tile_gain(p['c_gq'][li], heads_per_tile), _tile_gain(p['c_gk'][li], heads_per_tile)])
            outs = [("rows", d, F32)] * 2 + [("rows", d, BF16)] * 3
            tiles = ([(0, False, (("copy", 2, c, SOFTMAX_Q_SCALE),)) for c in (0, TILE_N)]
                     + [(1, False, (("copy", 0, c, 1.0), ("copy", 3, c, 1.0))) for c in (0, TILE_N)]
                     + [(None, False, (("copy", 1, c, 1.0), ("copy", 4, c, 1.0))) for c in (0, TILE_N)])
            res = _proj(x, p['mix_g'][l], p['c_w_in'][li], gains, HEAD_DIM, tiles, outs, name="proj_c")
            k_f, v_f, q_c, k_c, v_c = (r.reshape(b, t, d) for r in res)
            band = C_PAST_CHUNKS * CHUNK
            if seq:
                tq = min(256, t)
                nb = band // tq + 1
                bias = _band_bias(p['c_bias'][li], (nb - 1) * tq + np.arange(tq), np.arange(nb * tq))
                o = _band_attn(bias, q_c, k_c, v_c, tq=tq, nb=nb, tkb=tq)
                keep = min(band, t)
                new_c.append((k_f[:, t - keep:], v_f[:, t - keep:]))
            else:
                pk, pv = (c[li].reshape(b, -1, d) for c in past_c)
                pc = pk.shape[1]
                k_all = jnp.concatenate([pk, k_f], axis=1)
                v_all = jnp.concatenate([pv, v_f], axis=1)
                bias = _band_bias(p['c_bias'][li], pos0 + np.arange(t), pos0 - pc + np.arange(pc + t))
                o = _band_attn(bias, q_c, k_all, v_all, tq=t, nb=1, tkb=pc + t)
                new_c.append((k_all[:, -pc:], v_all[:, -pc:]))
            x = _out_proj(x, [o.reshape(rows, d)], [p['c_w_out'][li]])
        x = _mem_attn(x.reshape(b, t, d), p['mem_g_x'][l], p['mem_wq'][l], p['mem_gq'][l],
                      mem_k[l], mem_v[l], p['mem_wo'][l]).reshape(rows, d)
        x = _ffn(x, p['ffn2_g'][l], p['ffn2_wg'][l], p['ffn2_wu'][l], p['ffn2_wd'][l])
    return x.reshape(b, t, d), new_ab, new_c


def kernel(x_prompt, x_sample, cache_a_k, cache_a_v, cache_b_k, cache_b_v, cache_c_k, cache_c_v, cache_mem_k, cache_mem_v, mem_prompt, ffn1_g, ffn1_wg, ffn1_wu, ffn1_wd, ffn2_g, ffn2_wg, ffn2_wu, ffn2_wd, mix_g, ab_w_in, ab_w_out, a_gq, a_gk, a_lq1, a_lk1, a_lq2, a_lk2, a_subln_g, c_w_in, c_w_out, c_gq, c_gk, c_bias, mem_g_x, mem_g_m, mem_wq, mem_wk, mem_wv, mem_wo, mem_gq, mem_gk):
    depth = ffn1_g.shape[0]
    p = dict(ffn1_g=ffn1_g, ffn2_g=ffn2_g, mix_g=mix_g, a_gq=a_gq, a_gk=a_gk, a_lq1=a_lq1, a_lk1=a_lk1,
             a_lq2=a_lq2, a_lk2=a_lk2, a_subln_g=a_subln_g, c_gq=c_gq, c_gk=c_gk, c_bias=c_bias,
             mem_g_x=mem_g_x, mem_gq=mem_gq)
    for name, w in (('ffn1_wg', ffn1_wg), ('ffn1_wu', ffn1_wu), ('ffn1_wd', ffn1_wd), ('ffn2_wg', ffn2_wg),
                    ('ffn2_wu', ffn2_wu), ('ffn2_wd', ffn2_wd), ('ab_w_in', ab_w_in), ('ab_w_out', ab_w_out),
                    ('c_w_in', c_w_in), ('c_w_out', c_w_out), ('mem_wq', mem_wq), ('mem_wo', mem_wo)):
        p[name] = w.astype(BF16)

    bp, tp, d = x_prompt.shape
    bs, ts, _ = x_sample.shape
    m_len = mem_prompt.shape[1]
    past_len = cache_a_k.shape[2]

    mem_k_list, mem_v_list = [], []
    n_half = d // TILE_N
    for l in range(depth):
        w_kv = jnp.concatenate([mem_wk[l], mem_wv[l]], axis=1).astype(BF16)
        gains = _tile_gain(mem_gk[l], TILE_N // MEM_HEAD_DIM)[None]
        tiles = ([(0, False, (("copy", 0, c * TILE_N, 1.0),)) for c in range(n_half)]
                 + [(None, False, (("copy", 1, c * TILE_N, 1.0),)) for c in range(n_half)])
        mk, mv = _proj(mem_prompt.reshape(bp * m_len, d), mem_g_m[l], w_kv, gains, MEM_HEAD_DIM, tiles,
                       [("rows", d, F32)] * 2, name="proj_mem_kv")
        mem_k_list.append(mk.reshape(bp, m_len, d))
        mem_v_list.append(mv.reshape(bp, m_len, d))

    y_prompt, ab_p, c_p = _trunk(x_prompt, 0, mem_k_list, mem_v_list, None, None, p)
    cache_mem_k2 = cache_mem_k.reshape(depth, bs, m_len, d)
    cache_mem_v2 = cache_mem_v.reshape(depth, bs, m_len, d)
    y_sample, ab_s, c_s = _trunk(x_sample, past_len, cache_mem_k2, cache_mem_v2,
                                 (cache_a_k, cache_a_v, cache_b_k, cache_b_v), (cache_c_k, cache_c_v), p)

    a_heads = cache_a_v.shape[3]
    b_heads = cache_b_k.shape[3]
    c_heads = cache_c_k.shape[3]

    def stack(rows, idx, heads, width):
        return jnp.stack([r[idx].reshape(r[idx].shape[0], r[idx].shape[1], heads, width) for r in rows])

    outs = [y_prompt, y_sample]
    outs += [stack(ab_p, 0, 2 * a_heads, HEAD_DIM), stack(ab_p, 1, a_heads, 2 * HEAD_DIM),
             stack(ab_p, 2, b_heads, HEAD_DIM), stack(ab_p, 3, b_heads, HEAD_DIM),
             stack(c_p, 0, c_heads, HEAD_DIM), stack(c_p, 1, c_heads, HEAD_DIM)]
    outs += [jnp.stack(mem_k_list).reshape(depth, bp, m_len, MEM_HEADS, MEM_HEAD_DIM),
             jnp.stack(mem_v_list).reshape(depth, bp, m_len, MEM_HEADS, MEM_HEAD_DIM)]
    outs += [stack(ab_s, 0, 2 * a_heads, HEAD_DIM), stack(ab_s, 1, a_heads, 2 * HEAD_DIM),
             stack(ab_s, 2, b_heads, HEAD_DIM), stack(ab_s, 3, b_heads, HEAD_DIM),
             stack(c_s, 0, c_heads, HEAD_DIM), stack(c_s, 1, c_heads, HEAD_DIM)]
    return tuple(outs)
```

```python
import functools
import math

import jax
import jax.numpy as jnp
import numpy as np
from jax import lax
from jax.experimental import pallas as pl
from jax.experimental.pallas import tpu as pltpu

D_MODEL = 1024
CHUNK = 64
HEAD_DIM = 64
ROT_DIM = HEAD_DIM // 4
ROPE_THETA = 500000.0
C_PAST_CHUNKS = 8
REL_CLIP = 128
MEM_HEADS = 4
MEM_HEAD_DIM = D_MODEL // MEM_HEADS
RMS_EPS = 1e-6
NEG = -1e30
LOG2E = 1.4426950408889634
LANES = 128
TILE_N = 512
SEQ_BLOCK = 512
STICK_BLOCK = 256
BAND_BLOCK = 256
BAND_GROUP = 4
FFN_TILE = 256
EXP_UNDERFLOW = -104.0
VMEM_LIMIT = 56 * 1024 * 1024
SOFTMAX_Q_SCALE = HEAD_DIM ** -0.5 * LOG2E
STICK_Q_SCALE = HEAD_DIM ** -0.5

BF16 = jnp.bfloat16
F32 = jnp.float32


def _dot(a, b):
    return jnp.dot(a, b, preferred_element_type=F32)


def _dot_nt(a, b):
    return lax.dot_general(a, b, (((1,), (1,)), ((), ())), preferred_element_type=F32)


def _rms(x, g):
    return x * lax.rsqrt(jnp.mean(x * x, axis=-1, keepdims=True) + RMS_EPS) * g


def _split_bf16(x):
    hi = x.astype(BF16)
    lo = (x - hi.astype(F32)).astype(BF16)
    return hi, lo


def _params(*sem):
    return pltpu.CompilerParams(dimension_semantics=sem, vmem_limit_bytes=VMEM_LIMIT)


def _lane_masks(shape):
    lane = lax.broadcasted_iota(jnp.int32, shape, 1)
    return lane < HEAD_DIM


def _head_pair_queries(q):
    first = _lane_masks(q.shape)
    zero = jnp.zeros_like(q)
    return jnp.where(first, q, zero), jnp.where(first, zero, q)


def _lambda(lam_ref, lam_init):
    lv = lam_ref[...]
    return (jnp.exp(jnp.sum(lv[0:1] * lv[1:2], axis=-1, keepdims=True))
            - jnp.exp(jnp.sum(lv[2:3] * lv[3:4], axis=-1, keepdims=True)) + lam_init)


def _ffn_kernel(x_ref, g_ref, wg_ref, wu_ref, wd_ref, o_ref):
    x = x_ref[...]
    h = _rms(x, g_ref[...]).astype(BF16)
    acc = None
    for f in range(wg_ref.shape[1] // FFN_TILE):
        sl = slice(f * FFN_TILE, (f + 1) * FFN_TILE)
        a = _dot(h, wg_ref[:, sl])
        u = _dot(h, wu_ref[:, sl])
        act = a * (1.0 / (1.0 + jnp.exp(-a))) * u
        part = _dot(act.astype(BF16), wd_ref[sl, :])
        acc = part if acc is None else acc + part
    o_ref[...] = x + 0.5 * acc


def _ffn(x, g, wg, wu, wd):
    rows, d = x.shape
    ff = wg.shape[1]
    tm = min(512, rows)
    once = pl.Buffered(1)
    return pl.pallas_call(
        _ffn_kernel,
        grid=(rows // tm,),
        in_specs=[
            pl.BlockSpec((tm, d), lambda i: (i, 0)),
            pl.BlockSpec((1, d), lambda i: (0, 0)),
            pl.BlockSpec((d, ff), lambda i: (0, 0), pipeline_mode=once),
            pl.BlockSpec((d, ff), lambda i: (0, 0), pipeline_mode=once),
            pl.BlockSpec((ff, d), lambda i: (0, 0), pipeline_mode=once),
        ],
        out_specs=pl.BlockSpec((tm, d), lambda i: (i, 0)),
        out_shape=jax.ShapeDtypeStruct((rows, d), F32),
        compiler_params=_params("parallel"),
        name="ffn",
    )(x, g.reshape(1, d), wg, wu, wd)


def _proj_kernel(*refs, tiles, rope):
    x_ref, g_ref, w_ref, gain_ref, seg_ref = refs[:5]
    n_in = 8 if rope else 5
    out_refs = refs[n_in:]
    h = _rms(x_ref[...], g_ref[...]).astype(BF16)
    for t, (norm, use_rope, dests) in enumerate(tiles):
        y = _dot(h, w_ref[:, t * TILE_N:(t + 1) * TILE_N])
        if norm is not None:
            ms = _dot((y * y).astype(BF16), seg_ref[...])
            y = y * lax.rsqrt(ms + RMS_EPS) * gain_ref[norm:norm + 1, :]
        if use_rope:
            cos, sa, sb = refs[5][...], refs[6][...], refs[7][...]
            half = ROT_DIM // 2
            blocks = []
            for c in range(TILE_N // LANES):
                yb = y[:, c * LANES:(c + 1) * LANES]
                blocks.append(yb * cos + pltpu.roll(yb, LANES - half, 1) * sa + pltpu.roll(yb, half, 1) * sb)
            y = jnp.concatenate(blocks, axis=-1)
        for dest in dests:
            ref = out_refs[dest[1]]
            if dest[0] == "copy":
                _, _, col, scale = dest
                ref[:, col:col + TILE_N] = (y if scale == 1.0 else y * scale).astype(ref.dtype)
            else:
                ref[...] = y.T.reshape(ref.shape).astype(ref.dtype)


def _seg_matrix(seg):
    idx = np.arange(TILE_N) // seg
    return jnp.asarray((idx[:, None] == idx[None, :]).astype(np.float32) / seg, dtype=BF16)


def _proj(x, g, w, gains, seg, tiles, outs, rope_tables=None, name="proj"):
    rows, d = x.shape
    tm = min(SEQ_BLOCK, rows)
    rope = rope_tables is not None
    in_specs = [
        pl.BlockSpec((tm, d), lambda i: (i, 0)),
        pl.BlockSpec((1, d), lambda i: (0, 0)),
        pl.BlockSpec(w.shape, lambda i: (0, 0)),
        pl.BlockSpec(gains.shape, lambda i: (0, 0)),
        pl.BlockSpec((TILE_N, TILE_N), lambda i: (0, 0)),
    ]
    args = [x, g.reshape(1, d), w, gains, _seg_matrix(seg)]
    if rope:
        in_specs += [pl.BlockSpec((tm, LANES), lambda i: (i, 0))] * 3
        args += list(rope_tables)
    out_specs, out_shape = [], []
    for o in outs:
        if o[0] == "rows":
            _, width, dtype = o
            out_specs.append(pl.BlockSpec((tm, width), lambda i: (i, 0)))
            out_shape.append(jax.ShapeDtypeStruct((rows, width), dtype))
        else:
            groups = TILE_N // LANES
            out_specs.append(pl.BlockSpec((groups, None, LANES, tm), lambda i: (0, i, 0, 0)))
            out_shape.append(jax.ShapeDtypeStruct((groups, rows // tm, LANES, tm), o[1]))
    return pl.pallas_call(
        functools.partial(_proj_kernel, tiles=tuple(tiles), rope=rope),
        grid=(rows // tm,),
        in_specs=in_specs,
        out_specs=out_specs,
        out_shape=out_shape,
        compiler_params=_params("parallel"),
        name=name,
    )(*args)


def _rope_tables(pos):
    half = ROT_DIM // 2
    rows = pos.shape[0]
    inv_freq = ROPE_THETA ** (-2.0 * jnp.arange(half, dtype=F32) / ROT_DIM)
    ang = pos.astype(F32)[:, None] * inv_freq[None, :]
    c, s = jnp.cos(ang), jnp.sin(ang)
    rest = HEAD_DIM - ROT_DIM
    cos = jnp.concatenate([c, c, jnp.ones((rows, rest), F32)], axis=-1)
    sa = jnp.concatenate([-s, jnp.zeros((rows, HEAD_DIM - half), F32)], axis=-1)
    sb = jnp.concatenate([jnp.zeros((rows, half), F32), s, jnp.zeros((rows, rest), F32)], axis=-1)
    rep = LANES // HEAD_DIM
    return tuple(jnp.tile(t, (1, rep)) for t in (cos, sa, sb))


def _out_kernel(*refs, n_parts):
    x_ref = refs[0]
    o_ref = refs[1 + 2 * n_parts]
    acc = x_ref[...]
    for p in range(n_parts):
        acc = acc + _dot(refs[1 + p][...], refs[1 + n_parts + p][...])
    o_ref[...] = acc


def _out_proj(x, parts, w_parts):
    rows, d = x.shape
    tm = min(512, rows)
    n_parts = len(parts)
    in_specs = [pl.BlockSpec((tm, d), lambda i: (i, 0))]
    in_specs += [pl.BlockSpec((tm, p.shape[1]), lambda i: (i, 0)) for p in parts]
    in_specs += [pl.BlockSpec(w.shape, lambda i: (0, 0)) for w in w_parts]
    return pl.pallas_call(
        functools.partial(_out_kernel, n_parts=n_parts),
        grid=(rows // tm,),
        in_specs=in_specs,
        out_specs=pl.BlockSpec((tm, d), lambda i: (i, 0)),
        out_shape=jax.ShapeDtypeStruct((rows, d), F32),
        compiler_params=_params("parallel"),
        name="out_proj",
    )(x, *parts, *w_parts)


def _mem_kernel(x_ref, g_ref, wq_ref, gq_ref, k_ref, v_ref, wo_ref, o_ref):
    x = x_ref[...]
    h = _rms(x, g_ref[...]).astype(BF16)
    q = _dot(h, wq_ref[...])
    gq = gq_ref[...]
    scale = MEM_HEAD_DIM ** -0.5
    outs = []
    for hd in range(MEM_HEADS):
        sl = slice(hd * MEM_HEAD_DIM, (hd + 1) * MEM_HEAD_DIM)
        qh = _rms(q[:, sl], gq) * scale
        s = _dot_nt(qh.astype(BF16), k_ref[:, sl].astype(BF16))
        m = jnp.max(s, axis=-1, keepdims=True)
        p = jnp.exp(s - m)
        l = jnp.sum(p, axis=-1, keepdims=True)
        oh = _dot(p.astype(BF16), v_ref[:, sl].astype(BF16)) / l
        outs.append(oh.astype(BF16))
    o = jnp.concatenate(outs, axis=-1)
    o_ref[...] = x + _dot(o, wo_ref[...])


def _mem_attn(x, g, wq, gq, mk, mv, wo):
    b, t, d = x.shape
    m = mk.shape[1]
    tm = min(512, t)
    return pl.pallas_call(
        _mem_kernel,
        grid=(b, t // tm),
        in_specs=[
            pl.BlockSpec((None, tm, d), lambda bi, i: (bi, i, 0)),
            pl.BlockSpec((1, d), lambda bi, i: (0, 0)),
            pl.BlockSpec((d, d), lambda bi, i: (0, 0)),
            pl.BlockSpec((1, MEM_HEAD_DIM), lambda bi, i: (0, 0)),
            pl.BlockSpec((None, m, d), lambda bi, i: (bi, 0, 0)),
            pl.BlockSpec((None, m, d), lambda bi, i: (bi, 0, 0)),
            pl.BlockSpec((d, d), lambda bi, i: (0, 0)),
        ],
        out_specs=pl.BlockSpec((None, tm, d), lambda bi, i: (bi, i, 0)),
        out_shape=jax.ShapeDtypeStruct((b, t, d), F32),
        compiler_params=_params("parallel", "parallel"),
        name="mem_attn",
    )(x, g.reshape(1, d), wq, gq.reshape(1, MEM_HEAD_DIM), mk, mv, wo)


def _diff_hist_kernel(lam_ref, g_ref, q_ref, k_ref, v_ref, o_ref, *, q_off, lam_init):
    q = q_ref[...]
    tq, tk = q.shape[0], k_ref.shape[0]
    kb = k_ref[...].astype(BF16)
    vb = v_ref[...].astype(BF16)
    vis = (lax.broadcasted_iota(jnp.int32, (tq, tk), 1) // CHUNK
           <= (q_off + lax.broadcasted_iota(jnp.int32, (tq, tk), 0)) // CHUNK)
    outs = []
    for qm in _head_pair_queries(q):
        s = jnp.where(vis, _dot_nt(qm, kb), NEG)
        p = jnp.exp2(s - jnp.max(s, axis=-1, keepdims=True))
        outs.append(_dot(p.astype(BF16), vb) / jnp.sum(p, axis=-1, keepdims=True))
    o = outs[0] - _lambda(lam_ref, lam_init) * outs[1]
    o_ref[...] = (_rms(o, g_ref[...]) * (1.0 - lam_init)).astype(o_ref.dtype)


def _diff_attn_hist(lam_vecs, subln_g, q, k, v, *, q_off, lam_init):
    b, tq, width = q.shape
    tk = k.shape[1]
    return pl.pallas_call(
        functools.partial(_diff_hist_kernel, q_off=q_off, lam_init=lam_init),
        grid=(b, width // LANES),
        in_specs=[
            pl.BlockSpec((4, HEAD_DIM), lambda bi, h: (0, 0)),
            pl.BlockSpec((1, LANES), lambda bi, h: (0, 0)),
            pl.BlockSpec((None, tq, LANES), lambda bi, h: (bi, 0, h)),
            pl.BlockSpec((None, tk, LANES), lambda bi, h: (bi, 0, h)),
            pl.BlockSpec((None, tk, LANES), lambda bi, h: (bi, 0, h)),
        ],
        out_specs=pl.BlockSpec((None, tq, LANES), lambda bi, h: (bi, 0, h)),
        out_shape=jax.ShapeDtypeStruct((b, tq, width), BF16),
        compiler_params=_params("parallel", "parallel"),
        name="diff_attn_hist",
    )(lam_vecs, subln_g.reshape(1, LANES), q, k, v)


def _diff_seq_kernel(lam_ref, g_ref, q_ref, k_ref, vt_ref, o_ref, m_ref, l_ref, acc_ref, sa_ref, sb_ref,
                     *, t_blk, lam_init):
    i = pl.program_id(2)
    q2 = jnp.concatenate(_head_pair_queries(q_ref[...]), axis=0)
    q2t = q2.astype(F32).T.astype(BF16)
    m_ref[...] = jnp.full_like(m_ref, NEG)
    l_ref[...] = jnp.zeros_like(l_ref)
    acc_ref[...] = jnp.zeros_like(acc_ref)

    def scores(j, dst_ref):
        dst_ref[...] = _dot(k_ref[j], q2t)

    def update(src_ref, j, masked):
        st = src_ref[...]
        if masked:
            vis = (lax.broadcasted_iota(jnp.int32, (t_blk, 2 * t_blk), 0) // CHUNK
                   <= (lax.broadcasted_iota(jnp.int32, (t_blk, 2 * t_blk), 1) % t_blk) // CHUNK)
            st = jnp.where(vis, st, NEG)
        m_old = m_ref[...]
        m_new = jnp.maximum(m_old, jnp.max(st, axis=0, keepdims=True))
        alpha = jnp.exp2(m_old - m_new)
        pt = jnp.exp2(st - m_new)
        l_ref[...] = alpha * l_ref[...] + jnp.sum(pt, axis=0, keepdims=True)
        acc_ref[...] = alpha * acc_ref[...] + _dot(vt_ref[j], pt.astype(BF16))
        m_ref[...] = m_new

    scores(0, sa_ref)

    def body(jj, carry):
        j = 2 * jj
        scores(j + 1, sb_ref)
        update(sa_ref, j, False)
        scores(j + 2, sa_ref)
        update(sb_ref, j + 1, False)
        return carry

    lax.fori_loop(0, i // 2, body, 0)

    @pl.when(i % 2 == 1)
    def _():
        scores(i, sb_ref)
        update(sa_ref, i - 1, False)
        update(sb_ref, i, True)

    @pl.when(i % 2 == 0)
    def _():
        update(sa_ref, i, True)

    o12 = acc_ref[...] / l_ref[...]
    ot = o12[:, :t_blk] - _lambda(lam_ref, lam_init) * o12[:, t_blk:]
    ot = ot * lax.rsqrt(jnp.mean(ot * ot, axis=0, keepdims=True) + RMS_EPS) * g_ref[...] * (1.0 - lam_init)
    o_ref[...] = ot.T.astype(o_ref.dtype)


def _diff_attn_seq(lam_vecs, subln_g, q, k, vt, *, t_blk, lam_init):
    b, t, width = q.shape
    n_heads = width // LANES
    n_blk = t // t_blk
    return pl.pallas_call(
        functools.partial(_diff_seq_kernel, t_blk=t_blk, lam_init=lam_init),
        grid=(b, n_heads, n_blk),
        in_specs=[
            pl.BlockSpec((4, HEAD_DIM), lambda bi, h, i: (0, 0)),
            pl.BlockSpec((LANES, 1), lambda bi, h, i: (0, 0)),
            pl.BlockSpec((None, t_blk, LANES), lambda bi, h, i: (bi, i, h)),
            pl.BlockSpec((None, n_blk, t_blk, LANES), lambda bi, h, i: (bi, 0, 0, h)),
            pl.BlockSpec((None, None, n_blk, LANES, t_blk), lambda bi, h, i: (h, bi, 0, 0, 0)),
        ],
        out_specs=pl.BlockSpec((None, t_blk, LANES), lambda bi, h, i: (bi, i, h)),
        out_shape=jax.ShapeDtypeStruct((b, t, width), BF16),
        scratch_shapes=[pltpu.VMEM((1, 2 * t_blk), F32), pltpu.VMEM((1, 2 * t_blk), F32),
                        pltpu.VMEM((LANES, 2 * t_blk), F32),
                        pltpu.VMEM((t_blk, 2 * t_blk), F32), pltpu.VMEM((t_blk, 2 * t_blk), F32)],
        compiler_params=_params("parallel", "parallel", "arbitrary"),
        name="diff_attn_seq",
    )(lam_vecs, subln_g.reshape(LANES, 1), q, k.reshape(b, n_blk, t_blk, width), vt)


def _upper(n):
    return jnp.asarray((np.arange(n)[:, None] > np.arange(n)[None, :]).astype(np.float32), dtype=BF16)


def _stick_block(z, vb, u, causal, r_old):
    sp = jnp.maximum(z, 0.0) + jnp.log(1.0 + jnp.exp(-jnp.abs(z)))
    log_1m = -sp
    if causal is not None:
        log_1m = jnp.where(causal, log_1m, 0.0)
    hi, lo = _split_bf16(log_1m)
    after = _dot(hi, u) + _dot(lo, u)
    w = jnp.exp((z - sp) + after + r_old)
    if causal is not None:
        w = jnp.where(causal, w, 0.0)
    return _dot(w.astype(BF16), vb), after[:, 0:1] + log_1m[:, 0:1]


def _stick_hist_kernel(u_ref, q_ref, k_ref, v_ref, o_ref, *, q_off):
    q = q_ref[...]
    tq, tk = q.shape[0], k_ref.shape[0]
    kb = k_ref[...].astype(BF16)
    vb = v_ref[...].astype(BF16)
    causal = (lax.broadcasted_iota(jnp.int32, (tq, tk), 1)
              < q_off + lax.broadcasted_iota(jnp.int32, (tq, tk), 0))
    outs = [_stick_block(_dot_nt(qm, kb), vb, u_ref[...], causal, 0.0)[0] for qm in _head_pair_queries(q)]
    o_ref[...] = jnp.where(_lane_masks(o_ref.shape), outs[0], outs[1]).astype(o_ref.dtype)


def _stick_attn_hist(q, k, v, *, q_off):
    b, tq, width = q.shape
    tk = k.shape[1]
    return pl.pallas_call(
        functools.partial(_stick_hist_kernel, q_off=q_off),
        grid=(b, width // LANES),
        in_specs=[
            pl.BlockSpec((tk, tk), lambda bi, h: (0, 0)),
            pl.BlockSpec((None, tq, LANES), lambda bi, h: (bi, 0, h)),
            pl.BlockSpec((None, tk, LANES), lambda bi, h: (bi, 0, h)),
            pl.BlockSpec((None, tk, LANES), lambda bi, h: (bi, 0, h)),
        ],
        out_specs=pl.BlockSpec((None, tq, LANES), lambda bi, h: (bi, 0, h)),
        out_shape=jax.ShapeDtypeStruct((b, tq, width), BF16),
        compiler_params=_params("parallel", "parallel"),
        name="stick_attn_hist",
    )(_upper(tk), q, k, v)


def _stick_seq_kernel(u_ref, q_ref, k_ref, v_ref, o_ref, acc_ref, r_ref, z_ref, *, t_blk):
    i = pl.program_id(2)
    q2 = jnp.concatenate(_head_pair_queries(q_ref[...]), axis=0)
    acc_ref[...] = jnp.zeros_like(acc_ref)
    r_ref[...] = jnp.zeros_like(r_ref)
    u = u_ref[...]

    def block(j, masked):
        causal = None
        if masked:
            causal = (lax.broadcasted_iota(jnp.int32, (2 * t_blk, t_blk), 1)
                      < lax.broadcasted_iota(jnp.int32, (2 * t_blk, t_blk), 0) % t_blk)
        z = z_ref[...]
        z_ref[...] = _dot_nt(q2, k_ref[jnp.maximum(j - 1, 0)])
        r_old = r_ref[...]
        wv, r_blk = _stick_block(z, v_ref[j], u, causal, r_old)
        acc_ref[...] += wv
        r_new = r_old + r_blk
        r_ref[...] = r_new
        return jnp.max(r_new)

    z_ref[...] = _dot_nt(q2, k_ref[i])
    block(i, True)

    def cond(carry):
        j, done = carry
        return jnp.logical_and(j >= 0, done == 0)

    def body(carry):
        j, _ = carry
        r_max = block(j, False)
        return j - 1, (r_max < EXP_UNDERFLOW).astype(jnp.int32)

    lax.while_loop(cond, body, (i - 1, jnp.int32(0)))
    acc = acc_ref[...]
    o_ref[...] = jnp.where(_lane_masks(o_ref.shape), acc[:t_blk], acc[t_blk:]).astype(o_ref.dtype)


def _stick_attn_seq(q, k, v, *, t_blk):
    b, t, width = q.shape
    n_blk = t // t_blk
    kv_spec = pl.BlockSpec((None, n_blk, t_blk, LANES), lambda bi, h, i: (bi, 0, 0, h))
    return pl.pallas_call(
        functools.partial(_stick_seq_kernel, t_blk=t_blk),
        grid=(b, width // LANES, n_blk),
        in_specs=[
            pl.BlockSpec((t_blk, t_blk), lambda bi, h, i: (0, 0)),
            pl.BlockSpec((None, t_blk, LANES), lambda bi, h, i: (bi, i, h)),
            kv_spec, kv_spec,
        ],
        out_specs=pl.BlockSpec((None, t_blk, LANES), lambda bi, h, i: (bi, i, h)),
        out_shape=jax.ShapeDtypeStruct((b, t, width), BF16),
        scratch_shapes=[pltpu.VMEM((2 * t_blk, LANES), F32), pltpu.VMEM((2 * t_blk, 1), F32),
                        pltpu.VMEM((2 * t_blk, t_blk), F32)],
        compiler_params=_params("parallel", "parallel", "arbitrary"),
        name="stick_attn_seq",
    )(_upper(t_blk), q, k.reshape(b, n_blk, t_blk, width), v.reshape(b, n_blk, t_blk, width))


def _band_kernel(*refs, nb, tkb):
    bias_ref, q_ref = refs[0], refs[1]
    k_refs = refs[2:2 + nb]
    v_refs = refs[2 + nb:2 + 2 * nb]
    o_ref = refs[2 + 2 * nb]
    i = pl.program_id(2)
    kbs = [r[...].astype(BF16) for r in k_refs]
    vbs = [r[...].astype(BF16) for r in v_refs]
    outs = []
    for hd, qm in enumerate(_head_pair_queries(q_ref[...])):
        ss = []
        for d in range(nb):
            s = _dot_nt(qm, kbs[d]) + bias_ref[hd, :, d * tkb:(d + 1) * tkb]
            if d < nb - 1:
                s = jnp.where(i - (nb - 1) + d >= 0, s, NEG)
            ss.append(s)
        m = functools.reduce(jnp.maximum, [jnp.max(s, axis=-1, keepdims=True) for s in ss])
        ps = [jnp.exp2(s - m) for s in ss]
        l = functools.reduce(lambda a, c: a + c, [jnp.sum(p, axis=-1, keepdims=True) for p in ps])
        o = functools.reduce(lambda a, c: a + c, [_dot(p.astype(BF16), vb) for p, vb in zip(ps, vbs)])
        outs.append(o / l)
    o_ref[...] = jnp.where(_lane_masks(o_ref.shape), outs[0], outs[1]).astype(o_ref.dtype)


def _band_bias(bias_table, qpos, kpos):
    tq, tk = len(qpos), len(kpos)
    n = tq + tk
    shift = np.arange(n)
    c_minus_r = np.where(shift < tk, shift, shift - n)
    rel = np.clip(int(qpos[0] - kpos[0]) - c_minus_r, -REL_CLIP, REL_CLIP) + REL_CLIP
    vec = bias_table.astype(F32)[:, rel] * LOG2E
    toeplitz = jnp.tile(vec, (1, tq))[:, :tq * (n - 1)].reshape(-1, tq, n - 1)[:, :, :tk]
    qc, kc = qpos[:, None] // CHUNK, kpos[None, :] // CHUNK
    vis = (kpos[None, :] >= 0) & (kc <= qc) & (qc - kc <= C_PAST_CHUNKS)
    return jnp.where(jnp.asarray(vis)[None], toeplitz, NEG)


def _band_attn(bias, q, k, v, *, tq, nb, tkb):
    b, t_q, width = q.shape
    n_pairs = width // LANES

    def kv_spec(d):
        return pl.BlockSpec((None, tkb, LANES), lambda bi, h, i: (bi, jnp.maximum(i - (nb - 1) + d, 0), h))

    in_specs = [
        pl.BlockSpec((2, tq, nb * tkb), lambda bi, h, i: (h, 0, 0)),
        pl.BlockSpec((None, tq, LANES), lambda bi, h, i: (bi, i, h)),
    ]
    in_specs += [kv_spec(d) for d in range(nb)] * 2
    return pl.pallas_call(
        functools.partial(_band_kernel, nb=nb, tkb=tkb),
        grid=(b, n_pairs, t_q // tq),
        in_specs=in_specs,
        out_specs=pl.BlockSpec((None, tq, LANES), lambda bi, h, i: (bi, i, h)),
        out_shape=jax.ShapeDtypeStruct((b, t_q, width), BF16),
        compiler_params=_params("parallel", "parallel", "parallel"),
        name="band_attn",
    )(bias, q, *([k] * nb), *([v] * nb))


def _band_seq_kernel(bias_ref, q_ref, kp_ref, kc_ref, vp_ref, vc_ref, o_ref, *, tq, n_sub, n_win):
    big = n_sub * tq
    prev_ok = pl.program_id(2) > 0
    for g in range(n_sub):
        q = q_ref[g * tq:(g + 1) * tq, :]
        start = big - (n_win - 1) * tq + g * tq
        n_prev = max(big - start, 0)
        if n_prev > 0:
            kw = jnp.concatenate([kp_ref[start:big, :], kc_ref[0:n_win * tq - n_prev, :]], axis=0)
            vw = jnp.concatenate([vp_ref[start:big, :], vc_ref[0:n_win * tq - n_prev, :]], axis=0)
            before_start = jnp.logical_and(lax.broadcasted_iota(jnp.int32, (tq, n_win * tq), 1) < n_prev,
                                           jnp.logical_not(prev_ok))
        else:
            kw = kc_ref[start - big:start - big + n_win * tq, :]
            vw = vc_ref[start - big:start - big + n_win * tq, :]
        outs = []
        for hd, qm in enumerate(_head_pair_queries(q)):
            s = _dot_nt(qm, kw) + bias_ref[hd]
            if n_prev > 0:
                s = jnp.where(before_start, NEG, s)
            p = jnp.exp2(s - jnp.max(s, axis=-1, keepdims=True))
            outs.append(_dot(p.astype(BF16), vw) / jnp.sum(p, axis=-1, keepdims=True))
        o_ref[g * tq:(g + 1) * tq, :] = jnp.where(_lane_masks(q.shape), outs[0], outs[1]).astype(o_ref.dtype)


def _band_attn_seq(bias, q, k, v, *, tq, n_win, n_sub):
    b, t, width = q.shape
    big = n_sub * tq
    assert (n_win - 1) * tq <= big, "the window must fit in the previous + current block"
    cur = pl.BlockSpec((None, big, LANES), lambda bi, h, i: (bi, i, h))
    prev = pl.BlockSpec((None, big, LANES), lambda bi, h, i: (bi, jnp.maximum(i - 1, 0), h))
    return pl.pallas_call(
        functools.partial(_band_seq_kernel, tq=tq, n_sub=n_sub, n_win=n_win),
        grid=(b, width // LANES, t // big),
        in_specs=[pl.BlockSpec((2, tq, n_win * tq), lambda bi, h, i: (h, 0, 0)), cur, prev, cur, prev, cur],
        out_specs=cur,
        out_shape=jax.ShapeDtypeStruct((b, t, width), BF16),
        compiler_params=_params("parallel", "parallel", "parallel"),
        name="band_attn_seq",
    )(bias, q, k, k, v, v)


def _lambda_init(layer):
    return 0.8 - 0.6 * math.exp(-0.3 * layer)


def _tile_gain(g, n):
    return jnp.tile(g.astype(F32), n)


def _trunk(x, pos0, mem_k, mem_v, past_ab, past_c, p):
    b, t, d = x.shape
    rows = b * t
    depth = p['ffn1_g'].shape[0]
    seq = past_ab is None
    new_ab, new_c = [], []
    pos_rows = jnp.tile(pos0 + jnp.arange(t), b)
    x = x.reshape(rows, d)
    heads_per_tile = TILE_N // HEAD_DIM
    for l in range(depth):
        x = _ffn(x, p['ffn1_g'][l], p['ffn1_wg'][l], p['ffn1_wu'][l], p['ffn1_wd'][l])
        li = l // 2
        if l % 2 == 0:
            gains = jnp.stack([_tile_gain(p['a_gq'][li], heads_per_tile), _tile_gain(p['a_gk'][li], heads_per_tile)])
            outs = [("rows", TILE_N, F32)] * 4 + [("rows", TILE_N, BF16)] * 5
            v_dests = (("copy", 1, 0, 1.0),)
            if seq:
                outs = outs + [("heads_t", BF16)]
                v_dests += (("heads_t", 9),)
            tiles = [
                (0, True, (("copy", 4, 0, SOFTMAX_Q_SCALE),)),
                (1, True, (("copy", 0, 0, 1.0), ("copy", 5, 0, 1.0))),
                (None, False, v_dests),
                (None, False, (("copy", 6, 0, STICK_Q_SCALE),)),
                (None, False, (("copy", 2, 0, 1.0), ("copy", 7, 0, 1.0))),
                (None, False, (("copy", 3, 0, 1.0), ("copy", 8, 0, 1.0))),
            ]
            res = _proj(x, p['mix_g'][l], p['ab_w_in'][li], gains, HEAD_DIM, tiles, outs,
                        rope_tables=_rope_tables(pos_rows), name="proj_ab")
            a_k, a_v, b_k, b_v = (r.reshape(b, t, TILE_N) for r in res[:4])
            qa, ka, qb, kb, vb = (r.reshape(b, t, TILE_N) for r in res[4:9])
            new_ab.append((a_k, a_v, b_k, b_v))
            lam_vecs = jnp.stack([p['a_lq1'][li], p['a_lk1'][li], p['a_lq2'][li], p['a_lk2'][li]]).astype(F32)
            lam_init = _lambda_init(l)
            if seq:
                t_blk = min(SEQ_BLOCK, t)
                vt = res[9].reshape(TILE_N // LANES, b, t // t_blk, LANES, t_blk)
                a_out = _diff_attn_seq(lam_vecs, p['a_subln_g'][li], qa, ka, vt, t_blk=t_blk, lam_init=lam_init)
                b_out = _stick_attn_seq(qb, kb, vb, t_blk=min(STICK_BLOCK, t))
            else:
                pka, pva, pkb, pvb = (c[li].reshape(b, -1, TILE_N) for c in past_ab)
                past_len = pka.shape[1]
                a_out = _diff_attn_hist(lam_vecs, p['a_subln_g'][li], qa,
                                        jnp.concatenate([pka, a_k], axis=1), jnp.concatenate([pva, a_v], axis=1),
                                        q_off=past_len, lam_init=lam_init)
                b_out = _stick_attn_hist(qb, jnp.concatenate([pkb, b_k], axis=1),
                                         jnp.concatenate([pvb, b_v], axis=1), q_off=past_len)
            w_out = p['ab_w_out'][li]
            half = w_out.shape[0] // 2
            x = _out_proj(x, [a_out.reshape(rows, -1), b_out.reshape(rows, -1)], [w_out[:half], w_out[half:]])
        else:
            gains = jnp.stack([_tile_gain(p['c_gq'][li], heads_per_tile), _tile_gain(p['c_gk'][li], heads_per_tile)])
            outs = [("rows", d, F32)] * 2 + [("rows", d, BF16)] * 3
            tiles = ([(0, False, (("copy", 2, c, SOFTMAX_Q_SCALE),)) for c in (0, TILE_N)]
                     + [(1, False, (("copy", 0, c, 1.0), ("copy", 3, c, 1.0))) for c in (0, TILE_N)]
                     + [(None, False, (("copy", 1, c, 1.0), ("copy", 4, c, 1.0))) for c in (0, TILE_N)])
            res = _proj(x, p['mix_g'][l], p['c_w_in'][li], gains, HEAD_DIM, tiles, outs, name="proj_c")
            k_f, v_f, q_c, k_c, v_c = (r.reshape(b, t, d) for r in res)
            band = C_PAST_CHUNKS * CHUNK
            if seq:
                tq = min(BAND_BLOCK, t)
                nb = band // tq + 1
                bias = _band_bias(p['c_bias'][li], (nb - 1) * tq + np.arange(tq), np.arange(nb * tq))
                o = _band_attn_seq(bias, q_c, k_c, v_c, tq=tq, n_win=nb, n_sub=min(BAND_GROUP, t // tq))
                keep = min(band, t)
                new_c.append((k_f[:, t - keep:], v_f[:, t - keep:]))
            else:
                pk, pv = (c[li].reshape(b, -1, d) for c in past_c)
                pc = pk.shape[1]
                k_all = jnp.concatenate([pk, k_f], axis=1)
                v_all = jnp.concatenate([pv, v_f], axis=1)
                bias = _band_bias(p['c_bias'][li], pos0 + np.arange(t), pos0 - pc + np.arange(pc + t))
                o = _band_attn(bias, q_c, k_all, v_all, tq=t, nb=1, tkb=pc + t)
                new_c.append((k_all[:, -pc:], v_all[:, -pc:]))
            x = _out_proj(x, [o.reshape(rows, d)], [p['c_w_out'][li]])
        x = _mem_attn(x.reshape(b, t, d), p['mem_g_x'][l], p['mem_wq'][l], p['mem_gq'][l],
                      mem_k[l], mem_v[l], p['mem_wo'][l]).reshape(rows, d)
        x = _ffn(x, p['ffn2_g'][l], p['ffn2_wg'][l], p['ffn2_wu'][l], p['ffn2_wd'][l])
    return x.reshape(b, t, d), new_ab, new_c


def kernel(x_prompt, x_sample, cache_a_k, cache_a_v, cache_b_k, cache_b_v, cache_c_k, cache_c_v, cache_mem_k, cache_mem_v, mem_prompt, ffn1_g, ffn1_wg, ffn1_wu, ffn1_wd, ffn2_g, ffn2_wg, ffn2_wu, ffn2_wd, mix_g, ab_w_in, ab_w_out, a_gq, a_gk, a_lq1, a_lk1, a_lq2, a_lk2, a_subln_g, c_w_in, c_w_out, c_gq, c_gk, c_bias, mem_g_x, mem_g_m, mem_wq, mem_wk, mem_wv, mem_wo, mem_gq, mem_gk):
    depth = ffn1_g.shape[0]
    p = dict(ffn1_g=ffn1_g, ffn2_g=ffn2_g, mix_g=mix_g, a_gq=a_gq, a_gk=a_gk, a_lq1=a_lq1, a_lk1=a_lk1,
             a_lq2=a_lq2, a_lk2=a_lk2, a_subln_g=a_subln_g, c_gq=c_gq, c_gk=c_gk, c_bias=c_bias,
             mem_g_x=mem_g_x, mem_gq=mem_gq)
    for name, w in (('ffn1_wg', ffn1_wg), ('ffn1_wu', ffn1_wu), ('ffn1_wd', ffn1_wd), ('ffn2_wg', ffn2_wg),
                    ('ffn2_wu', ffn2_wu), ('ffn2_wd', ffn2_wd), ('ab_w_in', ab_w_in), ('ab_w_out', ab_w_out),
                    ('c_w_in', c_w_in), ('c_w_out', c_w_out), ('mem_wq', mem_wq), ('mem_wo', mem_wo)):
        p[name] = w.astype(BF16)

    bp, tp, d = x_prompt.shape
    bs, ts, _ = x_sample.shape
    m_len = mem_prompt.shape[1]
    past_len = cache_a_k.shape[2]

    mem_k_list, mem_v_list = [], []
    n_half = d // TILE_N
    for l in range(depth):
        w_kv = jnp.concatenate([mem_wk[l], mem_wv[l]], axis=1).astype(BF16)
        gains = _tile_gain(mem_gk[l], TILE_N // MEM_HEAD_DIM)[None]
        tiles = ([(0, False, (("copy", 0, c * TILE_N, 1.0),)) for c in range(n_half)]
                 + [(None, False, (("copy", 1, c * TILE_N, 1.0),)) for c in range(n_half)])
        mk, mv = _proj(mem_prompt.reshape(bp * m_len, d), mem_g_m[l], w_kv, gains, MEM_HEAD_DIM, tiles,
                       [("rows", d, F32)] * 2, name="proj_mem_kv")
        mem_k_list.append(mk.reshape(bp, m_len, d))
        mem_v_list.append(mv.reshape(bp, m_len, d))

    y_prompt, ab_p, c_p = _trunk(x_prompt, 0, mem_k_list, mem_v_list, None, None, p)
    cache_mem_k2 = cache_mem_k.reshape(depth, bs, m_len, d)
    cache_mem_v2 = cache_mem_v.reshape(depth, bs, m_len, d)
    y_sample, ab_s, c_s = _trunk(x_sample, past_len, cache_mem_k2, cache_mem_v2,
                                 (cache_a_k, cache_a_v, cache_b_k, cache_b_v), (cache_c_k, cache_c_v), p)

    a_heads = cache_a_v.shape[3]
    b_heads = cache_b_k.shape[3]
    c_heads = cache_c_k.shape[3]

    def stack(rows, idx, heads, width):
        return jnp.stack([r[idx].reshape(r[idx].shape[0], r[idx].shape[1], heads, width) for r in rows])

    outs = [y_prompt, y_sample]
    outs += [stack(ab_p, 0, 2 * a_heads, HEAD_DIM), stack(ab_p, 1, a_heads, 2 * HEAD_DIM),
             stack(ab_p, 2, b_heads, HEAD_DIM), stack(ab_p, 3, b_heads, HEAD_DIM),
             stack(c_p, 0, c_heads, HEAD_DIM), stack(c_p, 1, c_heads, HEAD_DIM)]
    outs += [jnp.stack(mem_k_list).reshape(depth, bp, m_len, MEM_HEADS, MEM_HEAD_DIM),
             jnp.stack(mem_v_list).reshape(depth, bp, m_len, MEM_HEADS, MEM_HEAD_DIM)]
    outs += [stack(ab_s, 0, 2 * a_heads, HEAD_DIM), stack(ab_s, 1, a_heads, 2 * HEAD_DIM),
             stack(ab_s, 2, b_heads, HEAD_DIM), stack(ab_s, 3, b_heads, HEAD_DIM),
             stack(c_s, 0, c_heads, HEAD_DIM), stack(c_s, 1, c_heads, HEAD_DIM)]
    return tuple(outs)
```

```python
import functools
import math

import jax
import jax.numpy as jnp
import numpy as np
from jax import lax
from jax.experimental import pallas as pl
from jax.experimental.pallas import tpu as pltpu

D_MODEL = 1024
CHUNK = 64
HEAD_DIM = 64
ROT_DIM = HEAD_DIM // 4
ROPE_THETA = 500000.0
C_PAST_CHUNKS = 8
REL_CLIP = 128
MEM_HEADS = 4
MEM_HEAD_DIM = D_MODEL // MEM_HEADS
RMS_EPS = 1e-6
NEG = -1e30
LOG2E = 1.4426950408889634
LANES = 128
TILE_N = 512
SEQ_BLOCK = 512
STICK_BLOCK = 256
BAND_BLOCK = 256
BAND_GROUP = 8
FFN_TILE = 256
EXP_UNDERFLOW = -104.0
VMEM_LIMIT = 56 * 1024 * 1024
SOFTMAX_Q_SCALE = HEAD_DIM ** -0.5 * LOG2E
STICK_Q_SCALE = HEAD_DIM ** -0.5

BF16 = jnp.bfloat16
F32 = jnp.float32


def _dot(a, b):
    return jnp.dot(a, b, preferred_element_type=F32)


def _dot_nt(a, b):
    return lax.dot_general(a, b, (((1,), (1,)), ((), ())), preferred_element_type=F32)


def _rms(x, g):
    return x * lax.rsqrt(jnp.mean(x * x, axis=-1, keepdims=True) + RMS_EPS) * g


def _split_bf16(x):
    hi = x.astype(BF16)
    lo = (x - hi.astype(F32)).astype(BF16)
    return hi, lo


def _params(*sem):
    return pltpu.CompilerParams(dimension_semantics=sem, vmem_limit_bytes=VMEM_LIMIT)


def _lane_masks(shape):
    lane = lax.broadcasted_iota(jnp.int32, shape, 1)
    return lane < HEAD_DIM


def _head_pair_queries(q):
    first = _lane_masks(q.shape)
    zero = jnp.zeros_like(q)
    return jnp.where(first, q, zero), jnp.where(first, zero, q)


def _lambda(lam_ref, lam_init):
    lv = lam_ref[...]
    return (jnp.exp(jnp.sum(lv[0:1] * lv[1:2], axis=-1, keepdims=True))
            - jnp.exp(jnp.sum(lv[2:3] * lv[3:4], axis=-1, keepdims=True)) + lam_init)


def _ffn_kernel(x_ref, g_ref, wg_ref, wu_ref, wd_ref, o_ref):
    x = x_ref[...]
    h = _rms(x, g_ref[...]).astype(BF16)
    acc = None
    for f in range(wg_ref.shape[1] // FFN_TILE):
        sl = slice(f * FFN_TILE, (f + 1) * FFN_TILE)
        a = _dot(h, wg_ref[:, sl])
        u = _dot(h, wu_ref[:, sl])
        act = a * (1.0 / (1.0 + jnp.exp(-a))) * u
        part = _dot(act.astype(BF16), wd_ref[sl, :])
        acc = part if acc is None else acc + part
    o_ref[...] = x + 0.5 * acc


def _ffn(x, g, wg, wu, wd):
    rows, d = x.shape
    ff = wg.shape[1]
    tm = min(512, rows)
    once = pl.Buffered(1)
    return pl.pallas_call(
        _ffn_kernel,
        grid=(rows // tm,),
        in_specs=[
            pl.BlockSpec((tm, d), lambda i: (i, 0)),
            pl.BlockSpec((1, d), lambda i: (0, 0)),
            pl.BlockSpec((d, ff), lambda i: (0, 0), pipeline_mode=once),
            pl.BlockSpec((d, ff), lambda i: (0, 0), pipeline_mode=once),
            pl.BlockSpec((ff, d), lambda i: (0, 0), pipeline_mode=once),
        ],
        out_specs=pl.BlockSpec((tm, d), lambda i: (i, 0)),
        out_shape=jax.ShapeDtypeStruct((rows, d), F32),
        compiler_params=_params("parallel"),
        name="ffn",
    )(x, g.reshape(1, d), wg, wu, wd)


def _proj_kernel(*refs, tiles, rope):
    x_ref, g_ref, w_ref, gain_ref, seg_ref = refs[:5]
    n_in = 8 if rope else 5
    out_refs = refs[n_in:]
    h = _rms(x_ref[...], g_ref[...]).astype(BF16)
    for t, (norm, use_rope, dests) in enumerate(tiles):
        y = _dot(h, w_ref[:, t * TILE_N:(t + 1) * TILE_N])
        if norm is not None:
            ms = _dot((y * y).astype(BF16), seg_ref[...])
            y = y * lax.rsqrt(ms + RMS_EPS) * gain_ref[norm:norm + 1, :]
        if use_rope:
            cos, sa, sb = refs[5][...], refs[6][...], refs[7][...]
            half = ROT_DIM // 2
            blocks = []
            for c in range(TILE_N // LANES):
                yb = y[:, c * LANES:(c + 1) * LANES]
                blocks.append(yb * cos + pltpu.roll(yb, LANES - half, 1) * sa + pltpu.roll(yb, half, 1) * sb)
            y = jnp.concatenate(blocks, axis=-1)
        for dest in dests:
            ref = out_refs[dest[1]]
            if dest[0] == "copy":
                _, _, col, scale = dest
                ref[:, col:col + TILE_N] = (y if scale == 1.0 else y * scale).astype(ref.dtype)
            else:
                groups = TILE_N // LANES
                ref[dest[2]:dest[2] + groups] = y.T.reshape((groups,) + ref.shape[1:]).astype(ref.dtype)


def _seg_matrix(seg):
    idx = np.arange(TILE_N) // seg
    return jnp.asarray((idx[:, None] == idx[None, :]).astype(np.float32) / seg, dtype=BF16)


def _proj(x, g, w, gains, seg, tiles, outs, rope_tables=None, name="proj"):
    rows, d = x.shape
    tm = min(SEQ_BLOCK, rows)
    rope = rope_tables is not None
    in_specs = [
        pl.BlockSpec((tm, d), lambda i: (i, 0)),
        pl.BlockSpec((1, d), lambda i: (0, 0)),
        pl.BlockSpec(w.shape, lambda i: (0, 0)),
        pl.BlockSpec(gains.shape, lambda i: (0, 0)),
        pl.BlockSpec((TILE_N, TILE_N), lambda i: (0, 0)),
    ]
    args = [x, g.reshape(1, d), w, gains, _seg_matrix(seg)]
    if rope:
        in_specs += [pl.BlockSpec((tm, LANES), lambda i: (i, 0))] * 3
        args += list(rope_tables)
    out_specs, out_shape = [], []
    for o in outs:
        if o[0] == "rows":
            _, width, dtype = o
            out_specs.append(pl.BlockSpec((tm, width), lambda i: (i, 0)))
            out_shape.append(jax.ShapeDtypeStruct((rows, width), dtype))
        else:
            _, groups, dtype = o
            out_specs.append(pl.BlockSpec((groups, None, LANES, tm), lambda i: (0, i, 0, 0)))
            out_shape.append(jax.ShapeDtypeStruct((groups, rows // tm, LANES, tm), dtype))
    return pl.pallas_call(
        functools.partial(_proj_kernel, tiles=tuple(tiles), rope=rope),
        grid=(rows // tm,),
        in_specs=in_specs,
        out_specs=out_specs,
        out_shape=out_shape,
        compiler_params=_params("parallel"),
        name=name,
    )(*args)


def _rope_tables(pos):
    half = ROT_DIM // 2
    rows = pos.shape[0]
    inv_freq = ROPE_THETA ** (-2.0 * jnp.arange(half, dtype=F32) / ROT_DIM)
    ang = pos.astype(F32)[:, None] * inv_freq[None, :]
    c, s = jnp.cos(ang), jnp.sin(ang)
    rest = HEAD_DIM - ROT_DIM
    cos = jnp.concatenate([c, c, jnp.ones((rows, rest), F32)], axis=-1)
    sa = jnp.concatenate([-s, jnp.zeros((rows, HEAD_DIM - half), F32)], axis=-1)
    sb = jnp.concatenate([jnp.zeros((rows, half), F32), s, jnp.zeros((rows, rest), F32)], axis=-1)
    rep = LANES // HEAD_DIM
    return tuple(jnp.tile(t, (1, rep)) for t in (cos, sa, sb))


def _mem_kernel(*refs, n_parts):
    x_ref = refs[0]
    part_refs = refs[1:1 + n_parts]
    w_part_refs = refs[1 + n_parts:1 + 2 * n_parts]
    g_ref, wq_ref, gq_ref, k_ref, v_ref, wo_ref, o_ref = refs[1 + 2 * n_parts:]
    x = x_ref[...]
    for a_ref, w_ref in zip(part_refs, w_part_refs):
        x = x + _dot(a_ref[...], w_ref[...])
    h = _rms(x, g_ref[...]).astype(BF16)
    q = _dot(h, wq_ref[...])
    gq = gq_ref[...]
    scale = MEM_HEAD_DIM ** -0.5
    outs = []
    for hd in range(MEM_HEADS):
        sl = slice(hd * MEM_HEAD_DIM, (hd + 1) * MEM_HEAD_DIM)
        qh = _rms(q[:, sl], gq) * scale
        s = _dot_nt(qh.astype(BF16), k_ref[:, sl].astype(BF16))
        m = jnp.max(s, axis=-1, keepdims=True)
        p = jnp.exp(s - m)
        l = jnp.sum(p, axis=-1, keepdims=True)
        oh = _dot(p.astype(BF16), v_ref[:, sl].astype(BF16)) / l
        outs.append(oh.astype(BF16))
    o = jnp.concatenate(outs, axis=-1)
    o_ref[...] = x + _dot(o, wo_ref[...])


def _mem_attn(x, parts, w_parts, g, wq, gq, mk, mv, wo):
    b, t, d = x.shape
    m = mk.shape[1]
    tm = min(512, t)
    row_block = lambda width: pl.BlockSpec((None, tm, width), lambda bi, i: (bi, i, 0))
    return pl.pallas_call(
        functools.partial(_mem_kernel, n_parts=len(parts)),
        grid=(b, t // tm),
        in_specs=[row_block(d)] + [row_block(a.shape[2]) for a in parts]
        + [pl.BlockSpec(w.shape, lambda bi, i: (0, 0)) for w in w_parts] + [
            pl.BlockSpec((1, d), lambda bi, i: (0, 0)),
            pl.BlockSpec((d, d), lambda bi, i: (0, 0)),
            pl.BlockSpec((1, MEM_HEAD_DIM), lambda bi, i: (0, 0)),
            pl.BlockSpec((None, m, d), lambda bi, i: (bi, 0, 0)),
            pl.BlockSpec((None, m, d), lambda bi, i: (bi, 0, 0)),
            pl.BlockSpec((d, d), lambda bi, i: (0, 0)),
        ],
        out_specs=row_block(d),
        out_shape=jax.ShapeDtypeStruct((b, t, d), F32),
        compiler_params=_params("parallel", "parallel"),
        name="mem_attn",
    )(x, *parts, *w_parts, g.reshape(1, d), wq, gq.reshape(1, MEM_HEAD_DIM), mk, mv, wo)


def _diff_hist_kernel(lam_ref, g_ref, q_ref, k_ref, v_ref, o_ref, *, q_off, lam_init):
    q = q_ref[...]
    tq, tk = q.shape[0], k_ref.shape[0]
    kb = k_ref[...].astype(BF16)
    vb = v_ref[...].astype(BF16)
    vis = (lax.broadcasted_iota(jnp.int32, (tq, tk), 1) // CHUNK
           <= (q_off + lax.broadcasted_iota(jnp.int32, (tq, tk), 0)) // CHUNK)
    outs = []
    for qm in _head_pair_queries(q):
        s = jnp.where(vis, _dot_nt(qm, kb), NEG)
        p = jnp.exp2(s - jnp.max(s, axis=-1, keepdims=True))
        outs.append(_dot(p.astype(BF16), vb) / jnp.sum(p, axis=-1, keepdims=True))
    o = outs[0] - _lambda(lam_ref, lam_init) * outs[1]
    o_ref[...] = (_rms(o, g_ref[...]) * (1.0 - lam_init)).astype(o_ref.dtype)


def _diff_attn_hist(lam_vecs, subln_g, q, k, v, *, q_off, lam_init):
    b, tq, width = q.shape
    tk = k.shape[1]
    return pl.pallas_call(
        functools.partial(_diff_hist_kernel, q_off=q_off, lam_init=lam_init),
        grid=(b, width // LANES),
        in_specs=[
            pl.BlockSpec((4, HEAD_DIM), lambda bi, h: (0, 0)),
            pl.BlockSpec((1, LANES), lambda bi, h: (0, 0)),
            pl.BlockSpec((None, tq, LANES), lambda bi, h: (bi, 0, h)),
            pl.BlockSpec((None, tk, LANES), lambda bi, h: (bi, 0, h)),
            pl.BlockSpec((None, tk, LANES), lambda bi, h: (bi, 0, h)),
        ],
        out_specs=pl.BlockSpec((None, tq, LANES), lambda bi, h: (bi, 0, h)),
        out_shape=jax.ShapeDtypeStruct((b, tq, width), BF16),
        compiler_params=_params("parallel", "parallel"),
        name="diff_attn_hist",
    )(lam_vecs, subln_g.reshape(1, LANES), q, k, v)


def _diff_seq_kernel(lam_ref, g_ref, q_ref, k_ref, vt_ref, o_ref, m_ref, l_ref, acc_ref, sa_ref, sb_ref,
                     *, t_blk, lam_init):
    i = pl.program_id(2)
    q2 = jnp.concatenate(_head_pair_queries(q_ref[...]), axis=0)
    q2t = q2.astype(F32).T.astype(BF16)
    m_ref[...] = jnp.full_like(m_ref, NEG)
    l_ref[...] = jnp.zeros_like(l_ref)
    acc_ref[...] = jnp.zeros_like(acc_ref)

    def scores(j, dst_ref):
        dst_ref[...] = _dot(k_ref[j], q2t)

    def update(src_ref, j, masked):
        st = src_ref[...]
        if masked:
            vis = (lax.broadcasted_iota(jnp.int32, (t_blk, 2 * t_blk), 0) // CHUNK
                   <= (lax.broadcasted_iota(jnp.int32, (t_blk, 2 * t_blk), 1) % t_blk) // CHUNK)
            st = jnp.where(vis, st, NEG)
        m_old = m_ref[...]
        m_new = jnp.maximum(m_old, jnp.max(st, axis=0, keepdims=True))
        alpha = jnp.exp2(m_old - m_new)
        pt = jnp.exp2(st - m_new)
        l_ref[...] = alpha * l_ref[...] + jnp.sum(pt, axis=0, keepdims=True)
        acc_ref[...] = alpha * acc_ref[...] + _dot(vt_ref[j], pt.astype(BF16))
        m_ref[...] = m_new

    scores(0, sa_ref)

    def body(jj, carry):
        j = 2 * jj
        scores(j + 1, sb_ref)
        update(sa_ref, j, False)
        scores(j + 2, sa_ref)
        update(sb_ref, j + 1, False)
        return carry

    lax.fori_loop(0, i // 2, body, 0)

    @pl.when(i % 2 == 1)
    def _():
        scores(i, sb_ref)
        update(sa_ref, i - 1, False)
        update(sb_ref, i, True)

    @pl.when(i % 2 == 0)
    def _():
        update(sa_ref, i, True)

    o12 = acc_ref[...] / l_ref[...]
    ot = o12[:, :t_blk] - _lambda(lam_ref, lam_init) * o12[:, t_blk:]
    ot = ot * lax.rsqrt(jnp.mean(ot * ot, axis=0, keepdims=True) + RMS_EPS) * g_ref[...] * (1.0 - lam_init)
    o_ref[...] = ot.T.astype(o_ref.dtype)


def _diff_attn_seq(lam_vecs, subln_g, q, k, vt, *, t_blk, lam_init):
    b, t, width = q.shape
    n_heads = width // LANES
    n_blk = t // t_blk
    return pl.pallas_call(
        functools.partial(_diff_seq_kernel, t_blk=t_blk, lam_init=lam_init),
        grid=(b, n_heads, n_blk),
        in_specs=[
            pl.BlockSpec((4, HEAD_DIM), lambda bi, h, i: (0, 0)),
            pl.BlockSpec((LANES, 1), lambda bi, h, i: (0, 0)),
            pl.BlockSpec((None, t_blk, LANES), lambda bi, h, i: (bi, i, h)),
            pl.BlockSpec((None, n_blk, t_blk, LANES), lambda bi, h, i: (bi, 0, 0, h)),
            pl.BlockSpec((None, None, n_blk, LANES, t_blk), lambda bi, h, i: (h, bi, 0, 0, 0)),
        ],
        out_specs=pl.BlockSpec((None, t_blk, LANES), lambda bi, h, i: (bi, i, h)),
        out_shape=jax.ShapeDtypeStruct((b, t, width), BF16),
        scratch_shapes=[pltpu.VMEM((1, 2 * t_blk), F32), pltpu.VMEM((1, 2 * t_blk), F32),
                        pltpu.VMEM((LANES, 2 * t_blk), F32),
                        pltpu.VMEM((t_blk, 2 * t_blk), F32), pltpu.VMEM((t_blk, 2 * t_blk), F32)],
        compiler_params=_params("parallel", "parallel", "arbitrary"),
        name="diff_attn_seq",
    )(lam_vecs, subln_g.reshape(LANES, 1), q, k.reshape(b, n_blk, t_blk, width), vt)


def _upper(n):
    return jnp.asarray((np.arange(n)[:, None] > np.arange(n)[None, :]).astype(np.float32), dtype=BF16)


def _stick_block(z, vb, u, causal, r_old):
    sp = jnp.maximum(z, 0.0) + jnp.log(1.0 + jnp.exp(-jnp.abs(z)))
    log_1m = -sp
    if causal is not None:
        log_1m = jnp.where(causal, log_1m, 0.0)
    hi, lo = _split_bf16(log_1m)
    after = _dot(hi, u) + _dot(lo, u)
    w = jnp.exp((z - sp) + after + r_old)
    if causal is not None:
        w = jnp.where(causal, w, 0.0)
    return _dot(w.astype(BF16), vb), after[:, 0:1] + log_1m[:, 0:1]


def _stick_hist_kernel(u_ref, q_ref, k_ref, v_ref, o_ref, *, q_off):
    q = q_ref[...]
    tq, tk = q.shape[0], k_ref.shape[0]
    kb = k_ref[...].astype(BF16)
    vb = v_ref[...].astype(BF16)
    causal = (lax.broadcasted_iota(jnp.int32, (tq, tk), 1)
              < q_off + lax.broadcasted_iota(jnp.int32, (tq, tk), 0))
    outs = [_stick_block(_dot_nt(qm, kb), vb, u_ref[...], causal, 0.0)[0] for qm in _head_pair_queries(q)]
    o_ref[...] = jnp.where(_lane_masks(o_ref.shape), outs[0], outs[1]).astype(o_ref.dtype)


def _stick_attn_hist(q, k, v, *, q_off):
    b, tq, width = q.shape
    tk = k.shape[1]
    return pl.pallas_call(
        functools.partial(_stick_hist_kernel, q_off=q_off),
        grid=(b, width // LANES),
        in_specs=[
            pl.BlockSpec((tk, tk), lambda bi, h: (0, 0)),
            pl.BlockSpec((None, tq, LANES), lambda bi, h: (bi, 0, h)),
            pl.BlockSpec((None, tk, LANES), lambda bi, h: (bi, 0, h)),
            pl.BlockSpec((None, tk, LANES), lambda bi, h: (bi, 0, h)),
        ],
        out_specs=pl.BlockSpec((None, tq, LANES), lambda bi, h: (bi, 0, h)),
        out_shape=jax.ShapeDtypeStruct((b, tq, width), BF16),
        compiler_params=_params("parallel", "parallel"),
        name="stick_attn_hist",
    )(_upper(tk), q, k, v)


def _stick_seq_kernel(u_ref, q_ref, k_ref, v_ref, o_ref, acc_ref, r_ref, z_ref, *, t_blk):
    i = pl.program_id(2)
    q2 = jnp.concatenate(_head_pair_queries(q_ref[...]), axis=0)
    acc_ref[...] = jnp.zeros_like(acc_ref)
    r_ref[...] = jnp.zeros_like(r_ref)
    u = u_ref[...]

    def block(j, masked):
        causal = None
        if masked:
            causal = (lax.broadcasted_iota(jnp.int32, (2 * t_blk, t_blk), 1)
                      < lax.broadcasted_iota(jnp.int32, (2 * t_blk, t_blk), 0) % t_blk)
        z = z_ref[...]
        z_ref[...] = _dot_nt(q2, k_ref[jnp.maximum(j - 1, 0)])
        r_old = r_ref[...]
        wv, r_blk = _stick_block(z, v_ref[j], u, causal, r_old)
        acc_ref[...] += wv
        r_new = r_old + r_blk
        r_ref[...] = r_new
        return jnp.max(r_new)

    z_ref[...] = _dot_nt(q2, k_ref[i])
    block(i, True)

    def cond(carry):
        j, done = carry
        return jnp.logical_and(j >= 0, done == 0)

    def body(carry):
        j, _ = carry
        r_max = block(j, False)
        return j - 1, (r_max < EXP_UNDERFLOW).astype(jnp.int32)

    lax.while_loop(cond, body, (i - 1, jnp.int32(0)))
    acc = acc_ref[...]
    o_ref[...] = jnp.where(_lane_masks(o_ref.shape), acc[:t_blk], acc[t_blk:]).astype(o_ref.dtype)


def _stick_attn_seq(q, k, v, *, t_blk):
    b, t, width = q.shape
    n_blk = t // t_blk
    kv_spec = pl.BlockSpec((None, n_blk, t_blk, LANES), lambda bi, h, i: (bi, 0, 0, h))
    return pl.pallas_call(
        functools.partial(_stick_seq_kernel, t_blk=t_blk),
        grid=(b, width // LANES, n_blk),
        in_specs=[
            pl.BlockSpec((t_blk, t_blk), lambda bi, h, i: (0, 0)),
            pl.BlockSpec((None, t_blk, LANES), lambda bi, h, i: (bi, i, h)),
            kv_spec, kv_spec,
        ],
        out_specs=pl.BlockSpec((None, t_blk, LANES), lambda bi, h, i: (bi, i, h)),
        out_shape=jax.ShapeDtypeStruct((b, t, width), BF16),
        scratch_shapes=[pltpu.VMEM((2 * t_blk, LANES), F32), pltpu.VMEM((2 * t_blk, 1), F32),
                        pltpu.VMEM((2 * t_blk, t_blk), F32)],
        compiler_params=_params("parallel", "parallel", "arbitrary"),
        name="stick_attn_seq",
    )(_upper(t_blk), q, k.reshape(b, n_blk, t_blk, width), v.reshape(b, n_blk, t_blk, width))


def _band_kernel(*refs, nb, tkb):
    bias_ref, q_ref = refs[0], refs[1]
    k_refs = refs[2:2 + nb]
    v_refs = refs[2 + nb:2 + 2 * nb]
    o_ref = refs[2 + 2 * nb]
    i = pl.program_id(2)
    kbs = [r[...].astype(BF16) for r in k_refs]
    vbs = [r[...].astype(BF16) for r in v_refs]
    outs = []
    for hd, qm in enumerate(_head_pair_queries(q_ref[...])):
        ss = []
        for d in range(nb):
            s = _dot_nt(qm, kbs[d]) + bias_ref[hd, :, d * tkb:(d + 1) * tkb]
            if d < nb - 1:
                s = jnp.where(i - (nb - 1) + d >= 0, s, NEG)
            ss.append(s)
        m = functools.reduce(jnp.maximum, [jnp.max(s, axis=-1, keepdims=True) for s in ss])
        ps = [jnp.exp2(s - m) for s in ss]
        l = functools.reduce(lambda a, c: a + c, [jnp.sum(p, axis=-1, keepdims=True) for p in ps])
        o = functools.reduce(lambda a, c: a + c, [_dot(p.astype(BF16), vb) for p, vb in zip(ps, vbs)])
        outs.append(o / l)
    o_ref[...] = jnp.where(_lane_masks(o_ref.shape), outs[0], outs[1]).astype(o_ref.dtype)


def _band_bias(bias_table, qpos, kpos):
    tq, tk = len(qpos), len(kpos)
    n = tq + tk
    shift = np.arange(n)
    c_minus_r = np.where(shift < tk, shift, shift - n)
    rel = np.clip(int(qpos[0] - kpos[0]) - c_minus_r, -REL_CLIP, REL_CLIP) + REL_CLIP
    vec = bias_table.astype(F32)[:, rel] * LOG2E
    toeplitz = jnp.tile(vec, (1, tq))[:, :tq * (n - 1)].reshape(-1, tq, n - 1)[:, :, :tk]
    qc, kc = qpos[:, None] // CHUNK, kpos[None, :] // CHUNK
    vis = (kpos[None, :] >= 0) & (kc <= qc) & (qc - kc <= C_PAST_CHUNKS)
    return jnp.where(jnp.asarray(vis)[None], toeplitz, NEG)


def _band_attn(bias, q, k, v, *, tq, nb, tkb):
    b, t_q, width = q.shape
    n_pairs = width // LANES

    def kv_spec(d):
        return pl.BlockSpec((None, tkb, LANES), lambda bi, h, i: (bi, jnp.maximum(i - (nb - 1) + d, 0), h))

    in_specs = [
        pl.BlockSpec((2, tq, nb * tkb), lambda bi, h, i: (h, 0, 0)),
        pl.BlockSpec((None, tq, LANES), lambda bi, h, i: (bi, i, h)),
    ]
    in_specs += [kv_spec(d) for d in range(nb)] * 2
    return pl.pallas_call(
        functools.partial(_band_kernel, nb=nb, tkb=tkb),
        grid=(b, n_pairs, t_q // tq),
        in_specs=in_specs,
        out_specs=pl.BlockSpec((None, tq, LANES), lambda bi, h, i: (bi, i, h)),
        out_shape=jax.ShapeDtypeStruct((b, t_q, width), BF16),
        compiler_params=_params("parallel", "parallel", "parallel"),
        name="band_attn",
    )(bias, q, *([k] * nb), *([v] * nb))


def _band_seq_kernel(bias_ref, q_ref, kp_ref, kc_ref, vp_ref, vc_ref, o_ref, *, tq, n_sub, n_win):
    big = n_sub * tq
    win = n_win * tq
    half = vp_ref.shape[2]
    prev_ok = pl.program_id(2) > 0
    top = lax.broadcasted_iota(jnp.int32, (LANES, tq), 0) < HEAD_DIM

    def value_piece(ref, off):
        return ref[off // half][:, off % half:off % half + tq]

    def scores(g):
        q2 = jnp.concatenate(_head_pair_queries(q_ref[g * tq:(g + 1) * tq, :]), axis=0)
        q2t = q2.astype(F32).T.astype(BF16)
        start = big - (n_win - 1) * tq + g * tq
        n_prev = max(big - start, 0)
        k_pieces, v_pieces = [], []
        for d in range(n_win):
            off = start + d * tq
            if off < big:
                k_pieces.append(kp_ref[off:off + tq, :])
                v_pieces.append(value_piece(vp_ref, off))
            else:
                k_pieces.append(kc_ref[off - big:off - big + tq, :])
                v_pieces.append(value_piece(vc_ref, off - big))
        st = _dot(jnp.concatenate(k_pieces, axis=0), q2t) + bias_ref[...]
        if n_prev > 0:
            before_start = jnp.logical_and(lax.broadcasted_iota(jnp.int32, (win, 2 * tq), 0) < n_prev,
                                           jnp.logical_not(prev_ok))
            st = jnp.where(before_start, NEG, st)
        return st, jnp.concatenate(v_pieces, axis=1)

    def finish(g, st, vwt):
        pt = jnp.exp2(st - jnp.max(st, axis=0, keepdims=True))
        ot = _dot(vwt, pt.astype(BF16)) / jnp.sum(pt, axis=0, keepdims=True)
        o_ref[g * tq:(g + 1) * tq, :] = jnp.where(top, ot[:, :tq], ot[:, tq:]).T.astype(o_ref.dtype)

    nxt = scores(0)
    for g in range(n_sub):
        cur_scores = nxt
        if g + 1 < n_sub:
            nxt = scores(g + 1)
        finish(g, *cur_scores)


def _band_attn_seq(bias_t, q, k, vt, *, tq, n_win, n_sub):
    b, t, width = q.shape
    big = n_sub * tq
    half = vt.shape[3]
    n_half = big // half
    assert (n_win - 1) * tq <= big, "the window must fit in the previous + current block"
    assert half % tq == 0, "a window piece must not straddle two transposed value blocks"
    cur = pl.BlockSpec((None, big, LANES), lambda bi, h, i: (bi, i, h))
    prev = pl.BlockSpec((None, big, LANES), lambda bi, h, i: (bi, jnp.maximum(i - 1, 0), h))
    per_b = t // big
    vcur = pl.BlockSpec((None, n_half, LANES, half), lambda bi, h, i: (h, bi * per_b + i, 0, 0))
    vprev = pl.BlockSpec((None, n_half, LANES, half), lambda bi, h, i: (h, bi * per_b + jnp.maximum(i - 1, 0), 0, 0))
    return pl.pallas_call(
        functools.partial(_band_seq_kernel, tq=tq, n_sub=n_sub, n_win=n_win),
        grid=(b, width // LANES, t // big),
        in_specs=[pl.BlockSpec((None, n_win * tq, 2 * tq), lambda bi, h, i: (h, 0, 0)), cur, prev, cur, vprev, vcur],
        out_specs=cur,
        out_shape=jax.ShapeDtypeStruct((b, t, width), BF16),
        compiler_params=_params("parallel", "parallel", "parallel"),
        name="band_attn_seq",
    )(bias_t, q, k, k, vt, vt)


def _lambda_init(layer):
    return 0.8 - 0.6 * math.exp(-0.3 * layer)


def _tile_gain(g, n):
    return jnp.tile(g.astype(F32), n)


def _trunk(x, pos0, mem_k, mem_v, past_ab, past_c, p):
    b, t, d = x.shape
    rows = b * t
    depth = p['ffn1_g'].shape[0]
    seq = past_ab is None
    new_ab, new_c = [], []
    pos_rows = jnp.tile(pos0 + jnp.arange(t), b)
    x = x.reshape(rows, d)
    heads_per_tile = TILE_N // HEAD_DIM
    for l in range(depth):
        x = _ffn(x, p['ffn1_g'][l], p['ffn1_wg'][l], p['ffn1_wu'][l], p['ffn1_wd'][l])
        li = l // 2
        if l % 2 == 0:
            gains = jnp.stack([_tile_gain(p['a_gq'][li], heads_per_tile), _tile_gain(p['a_gk'][li], heads_per_tile)])
            outs = [("rows", TILE_N, F32)] * 4 + [("rows", TILE_N, BF16)] * 5
            v_dests = (("copy", 1, 0, 1.0),)
            if seq:
                outs = outs + [("heads_t", TILE_N // LANES, BF16)]
                v_dests += (("heads_t", 9, 0),)
            tiles = [
                (0, True, (("copy", 4, 0, SOFTMAX_Q_SCALE),)),
                (1, True, (("copy", 0, 0, 1.0), ("copy", 5, 0, 1.0))),
                (None, False, v_dests),
                (None, False, (("copy", 6, 0, STICK_Q_SCALE),)),
                (None, False, (("copy", 2, 0, 1.0), ("copy", 7, 0, 1.0))),
                (None, False, (("copy", 3, 0, 1.0), ("copy", 8, 0, 1.0))),
            ]
            res = _proj(x, p['mix_g'][l], p['ab_w_in'][li], gains, HEAD_DIM, tiles, outs,
                        rope_tables=_rope_tables(pos_rows), name="proj_ab")
            a_k, a_v, b_k, b_v = (r.reshape(b, t, TILE_N) for r in res[:4])
            qa, ka, qb, kb, vb = (r.reshape(b, t, TILE_N) for r in res[4:9])
            new_ab.append((a_k, a_v, b_k, b_v))
            lam_vecs = jnp.stack([p['a_lq1'][li], p['a_lk1'][li], p['a_lq2'][li], p['a_lk2'][li]]).astype(F32)
            lam_init = _lambda_init(l)
            if seq:
                t_blk = min(SEQ_BLOCK, t)
                vt = res[9].reshape(TILE_N // LANES, b, t // t_blk, LANES, t_blk)
                a_out = _diff_attn_seq(lam_vecs, p['a_subln_g'][li], qa, ka, vt, t_blk=t_blk, lam_init=lam_init)
                b_out = _stick_attn_seq(qb, kb, vb, t_blk=min(STICK_BLOCK, t))
            else:
                pka, pva, pkb, pvb = (c[li].reshape(b, -1, TILE_N) for c in past_ab)
                past_len = pka.shape[1]
                a_out = _diff_attn_hist(lam_vecs, p['a_subln_g'][li], qa,
                                        jnp.concatenate([pka, a_k], axis=1), jnp.concatenate([pva, a_v], axis=1),
                                        q_off=past_len, lam_init=lam_init)
                b_out = _stick_attn_hist(qb, jnp.concatenate([pkb, b_k], axis=1),
                                         jnp.concatenate([pvb, b_v], axis=1), q_off=past_len)
            w_out = p['ab_w_out'][li]
            half = w_out.shape[0] // 2
            mixed, w_parts = [a_out, b_out], [w_out[:half], w_out[half:]]
        else:
            gains = jnp.stack([_tile_gain(p['c_gq'][li], heads_per_tile), _tile_gain(p['c_gk'][li], heads_per_tile)])
            groups = TILE_N // LANES
            outs = [("rows", d, F32)] * 2 + [("rows", d, BF16)] * 2
            if seq:
                outs = outs + [("heads_t", d // LANES, BF16)]
            tiles = ([(0, False, (("copy", 2, c, SOFTMAX_Q_SCALE),)) for c in (0, TILE_N)]
                     + [(1, False, (("copy", 0, c, 1.0), ("copy", 3, c, 1.0))) for c in (0, TILE_N)]
                     + [(None, False, (("copy", 1, c * TILE_N, 1.0),) + ((("heads_t", 4, c * groups),) if seq else ()))
                        for c in range(d // TILE_N)])
            res = _proj(x, p['mix_g'][l], p['c_w_in'][li], gains, HEAD_DIM, tiles, outs, name="proj_c")
            k_f, v_f, q_c, k_c = (r.reshape(b, t, d) for r in res[:4])
            band = C_PAST_CHUNKS * CHUNK
            if seq:
                tq = min(BAND_BLOCK, t)
                nb = band // tq + 1
                bias = _band_bias(p['c_bias'][li], (nb - 1) * tq + np.arange(tq), np.arange(nb * tq))
                bias_t = jnp.transpose(bias.reshape(-1, 2, tq, nb * tq), (0, 3, 1, 2)).reshape(-1, nb * tq, 2 * tq)
                o = _band_attn_seq(bias_t, q_c, k_c, res[4], tq=tq, n_win=nb, n_sub=min(BAND_GROUP, t // tq))
                keep = min(band, t)
                new_c.append((k_f[:, t - keep:], v_f[:, t - keep:]))
            else:
                pk, pv = (c[li].reshape(b, -1, d) for c in past_c)
                pc = pk.shape[1]
                k_all = jnp.concatenate([pk, k_f], axis=1)
                v_all = jnp.concatenate([pv, v_f], axis=1)
                bias = _band_bias(p['c_bias'][li], pos0 + np.arange(t), pos0 - pc + np.arange(pc + t))
                o = _band_attn(bias, q_c, k_all, v_all, tq=t, nb=1, tkb=pc + t)
                new_c.append((k_all[:, -pc:], v_all[:, -pc:]))
            mixed, w_parts = [o], [p['c_w_out'][li]]
        x = _mem_attn(x.reshape(b, t, d), mixed, w_parts, p['mem_g_x'][l], p['mem_wq'][l], p['mem_gq'][l],
                      mem_k[l], mem_v[l], p['mem_wo'][l]).reshape(rows, d)
        x = _ffn(x, p['ffn2_g'][l], p['ffn2_wg'][l], p['ffn2_wu'][l], p['ffn2_wd'][l])
    return x.reshape(b, t, d), new_ab, new_c


def kernel(x_prompt, x_sample, cache_a_k, cache_a_v, cache_b_k, cache_b_v, cache_c_k, cache_c_v, cache_mem_k, cache_mem_v, mem_prompt, ffn1_g, ffn1_wg, ffn1_wu, ffn1_wd, ffn2_g, ffn2_wg, ffn2_wu, ffn2_wd, mix_g, ab_w_in, ab_w_out, a_gq, a_gk, a_lq1, a_lk1, a_lq2, a_lk2, a_subln_g, c_w_in, c_w_out, c_gq, c_gk, c_bias, mem_g_x, mem_g_m, mem_wq, mem_wk, mem_wv, mem_wo, mem_gq, mem_gk):
    depth = ffn1_g.shape[0]
    p = dict(ffn1_g=ffn1_g, ffn2_g=ffn2_g, mix_g=mix_g, a_gq=a_gq, a_gk=a_gk, a_lq1=a_lq1, a_lk1=a_lk1,
             a_lq2=a_lq2, a_lk2=a_lk2, a_subln_g=a_subln_g, c_gq=c_gq, c_gk=c_gk, c_bias=c_bias,
             mem_g_x=mem_g_x, mem_gq=mem_gq)
    for name, w in (('ffn1_wg', ffn1_wg), ('ffn1_wu', ffn1_wu), ('ffn1_wd', ffn1_wd), ('ffn2_wg', ffn2_wg),
                    ('ffn2_wu', ffn2_wu), ('ffn2_wd', ffn2_wd), ('ab_w_in', ab_w_in), ('ab_w_out', ab_w_out),
                    ('c_w_in', c_w_in), ('c_w_out', c_w_out), ('mem_wq', mem_wq), ('mem_wo', mem_wo)):
        p[name] = w.astype(BF16)

    bp, tp, d = x_prompt.shape
    bs, ts, _ = x_sample.shape
    m_len = mem_prompt.shape[1]
    past_len = cache_a_k.shape[2]

    mem_k_list, mem_v_list = [], []
    n_half = d // TILE_N
    for l in range(depth):
        w_kv = jnp.concatenate([mem_wk[l], mem_wv[l]], axis=1).astype(BF16)
        gains = _tile_gain(mem_gk[l], TILE_N // MEM_HEAD_DIM)[None]
        tiles = ([(0, False, (("copy", 0, c * TILE_N, 1.0),)) for c in range(n_half)]
                 + [(None, False, (("copy", 1, c * TILE_N, 1.0),)) for c in range(n_half)])
        mk, mv = _proj(mem_prompt.reshape(bp * m_len, d), mem_g_m[l], w_kv, gains, MEM_HEAD_DIM, tiles,
                       [("rows", d, F32)] * 2, name="proj_mem_kv")
        mem_k_list.append(mk.reshape(bp, m_len, d))
        mem_v_list.append(mv.reshape(bp, m_len, d))

    y_prompt, ab_p, c_p = _trunk(x_prompt, 0, mem_k_list, mem_v_list, None, None, p)
    cache_mem_k2 = cache_mem_k.reshape(depth, bs, m_len, d)
    cache_mem_v2 = cache_mem_v.reshape(depth, bs, m_len, d)
    y_sample, ab_s, c_s = _trunk(x_sample, past_len, cache_mem_k2, cache_mem_v2,
                                 (cache_a_k, cache_a_v, cache_b_k, cache_b_v), (cache_c_k, cache_c_v), p)

    a_heads = cache_a_v.shape[3]
    b_heads = cache_b_k.shape[3]
    c_heads = cache_c_k.shape[3]

    def stack(rows, idx, heads, width):
        return jnp.stack([r[idx].reshape(r[idx].shape[0], r[idx].shape[1], heads, width) for r in rows])

    outs = [y_prompt, y_sample]
    outs += [stack(ab_p, 0, 2 * a_heads, HEAD_DIM), stack(ab_p, 1, a_heads, 2 * HEAD_DIM),
             stack(ab_p, 2, b_heads, HEAD_DIM), stack(ab_p, 3, b_heads, HEAD_DIM),
             stack(c_p, 0, c_heads, HEAD_DIM), stack(c_p, 1, c_heads, HEAD_DIM)]
    outs += [jnp.stack(mem_k_list).reshape(depth, bp, m_len, MEM_HEADS, MEM_HEAD_DIM),
             jnp.stack(mem_v_list).reshape(depth, bp, m_len, MEM_HEADS, MEM_HEAD_DIM)]
    outs += [stack(ab_s, 0, 2 * a_heads, HEAD_DIM), stack(ab_s, 1, a_heads, 2 * HEAD_DIM),
             stack(ab_s, 2, b_heads, HEAD_DIM), stack(ab_s, 3, b_heads, HEAD_DIM),
             stack(c_s, 0, c_heads, HEAD_DIM), stack(c_s, 1, c_heads, HEAD_DIM)]
    return tuple(outs)
```

```python
import functools
import math

import jax
import jax.numpy as jnp
import numpy as np
from jax import lax
from jax.experimental import pallas as pl
from jax.experimental.pallas import tpu as pltpu

D_MODEL = 1024
CHUNK = 64
HEAD_DIM = 64
ROT_DIM = HEAD_DIM // 4
ROPE_THETA = 500000.0
C_PAST_CHUNKS = 8
REL_CLIP = 128
MEM_HEADS = 4
MEM_HEAD_DIM = D_MODEL // MEM_HEADS
RMS_EPS = 1e-6
NEG = -1e30
LOG2E = 1.4426950408889634
LANES = 128
TILE_N = 512
SEQ_BLOCK = 512
STICK_BLOCK = 256
STICK_PAIRS = 2
BAND_BLOCK = 256
BAND_GROUP = 8
FFN_TILE = 256
SUM_ROWS = 16
EXP_UNDERFLOW = -104.0
VMEM_LIMIT = 56 * 1024 * 1024
SOFTMAX_Q_SCALE = HEAD_DIM ** -0.5 * LOG2E
STICK_Q_SCALE = HEAD_DIM ** -0.5

BF16 = jnp.bfloat16
F32 = jnp.float32


def _dot(a, b):
    return jnp.dot(a, b, preferred_element_type=F32)


def _dot_nt(a, b):
    return lax.dot_general(a, b, (((1,), (1,)), ((), ())), preferred_element_type=F32)


def _rms(x, g):
    return x * lax.rsqrt(jnp.mean(x * x, axis=-1, keepdims=True) + RMS_EPS) * g


def _split_bf16(x):
    hi = x.astype(BF16)
    lo = (x - hi.astype(F32)).astype(BF16)
    return hi, lo


def _params(*sem):
    return pltpu.CompilerParams(dimension_semantics=sem, vmem_limit_bytes=VMEM_LIMIT)


def _lane_masks(shape):
    lane = lax.broadcasted_iota(jnp.int32, shape, 1)
    return lane < HEAD_DIM


def _head_pair_queries(q):
    first = _lane_masks(q.shape)
    zero = jnp.zeros_like(q)
    return jnp.where(first, q, zero), jnp.where(first, zero, q)


def _lambda(lam_ref, lam_init):
    lv = lam_ref[...]
    return (jnp.exp(jnp.sum(lv[0:1] * lv[1:2], axis=-1, keepdims=True))
            - jnp.exp(jnp.sum(lv[2:3] * lv[3:4], axis=-1, keepdims=True)) + lam_init)


def _ffn_kernel(x_ref, g_ref, wg_ref, wu_ref, wd_ref, o_ref):
    x = x_ref[...]
    h = _rms(x, g_ref[...]).astype(BF16)
    acc = None
    for f in range(wg_ref.shape[1] // FFN_TILE):
        sl = slice(f * FFN_TILE, (f + 1) * FFN_TILE)
        a = _dot(h, wg_ref[:, sl])
        u = _dot(h, wu_ref[:, sl])
        act = a * (1.0 / (1.0 + jnp.exp(-a))) * u
        part = _dot(act.astype(BF16), wd_ref[sl, :])
        acc = part if acc is None else acc + part
    o_ref[...] = x + 0.5 * acc


def _ffn(x, g, wg, wu, wd):
    rows, d = x.shape
    ff = wg.shape[1]
    tm = min(512, rows)
    once = pl.Buffered(1)
    return pl.pallas_call(
        _ffn_kernel,
        grid=(rows // tm,),
        in_specs=[
            pl.BlockSpec((tm, d), lambda i: (i, 0)),
            pl.BlockSpec((1, d), lambda i: (0, 0)),
            pl.BlockSpec((d, ff), lambda i: (0, 0), pipeline_mode=once),
            pl.BlockSpec((d, ff), lambda i: (0, 0), pipeline_mode=once),
            pl.BlockSpec((ff, d), lambda i: (0, 0), pipeline_mode=once),
        ],
        out_specs=pl.BlockSpec((tm, d), lambda i: (i, 0)),
        out_shape=jax.ShapeDtypeStruct((rows, d), F32),
        compiler_params=_params("parallel"),
        name="ffn",
    )(x, g.reshape(1, d), wg, wu, wd)


def _proj_kernel(*refs, tiles, rope):
    x_ref, g_ref, w_ref, gain_ref, seg_ref = refs[:5]
    n_in = 8 if rope else 5
    out_refs = refs[n_in:]
    h = _rms(x_ref[...], g_ref[...]).astype(BF16)
    for t, (norm, use_rope, dests) in enumerate(tiles):
        y = _dot(h, w_ref[:, t * TILE_N:(t + 1) * TILE_N])
        if norm is not None:
            ms = _dot((y * y).astype(BF16), seg_ref[...])
            y = y * lax.rsqrt(ms + RMS_EPS) * gain_ref[norm:norm + 1, :]
        if use_rope:
            cos, sa, sb = refs[5][...], refs[6][...], refs[7][...]
            half = ROT_DIM // 2
            blocks = []
            for c in range(TILE_N // LANES):
                yb = y[:, c * LANES:(c + 1) * LANES]
                blocks.append(yb * cos + pltpu.roll(yb, LANES - half, 1) * sa + pltpu.roll(yb, half, 1) * sb)
            y = jnp.concatenate(blocks, axis=-1)
        for dest in dests:
            ref = out_refs[dest[1]]
            if dest[0] == "copy":
                _, _, col, scale = dest
                ref[:, col:col + TILE_N] = (y if scale == 1.0 else y * scale).astype(ref.dtype)
            else:
                groups = TILE_N // LANES
                ref[dest[2]:dest[2] + groups] = y.T.reshape((groups,) + ref.shape[1:]).astype(ref.dtype)


def _seg_matrix(seg):
    idx = np.arange(TILE_N) // seg
    return jnp.asarray((idx[:, None] == idx[None, :]).astype(np.float32) / seg, dtype=BF16)


def _proj(x, g, w, gains, seg, tiles, outs, rope_tables=None, name="proj"):
    rows, d = x.shape
    tm = min(SEQ_BLOCK, rows)
    rope = rope_tables is not None
    in_specs = [
        pl.BlockSpec((tm, d), lambda i: (i, 0)),
        pl.BlockSpec((1, d), lambda i: (0, 0)),
        pl.BlockSpec(w.shape, lambda i: (0, 0)),
        pl.BlockSpec(gains.shape, lambda i: (0, 0)),
        pl.BlockSpec((TILE_N, TILE_N), lambda i: (0, 0)),
    ]
    args = [x, g.reshape(1, d), w, gains, _seg_matrix(seg)]
    if rope:
        in_specs += [pl.BlockSpec((tm, LANES), lambda i: (i, 0))] * 3
        args += list(rope_tables)
    out_specs, out_shape = [], []
    for o in outs:
        if o[0] == "rows":
            _, width, dtype = o
            out_specs.append(pl.BlockSpec((tm, width), lambda i: (i, 0)))
            out_shape.append(jax.ShapeDtypeStruct((rows, width), dtype))
        else:
            _, groups, dtype = o
            out_specs.append(pl.BlockSpec((groups, None, LANES, tm), lambda i: (0, i, 0, 0)))
            out_shape.append(jax.ShapeDtypeStruct((groups, rows // tm, LANES, tm), dtype))
    return pl.pallas_call(
        functools.partial(_proj_kernel, tiles=tuple(tiles), rope=rope),
        grid=(rows // tm,),
        in_specs=in_specs,
        out_specs=out_specs,
        out_shape=out_shape,
        compiler_params=_params("parallel"),
        name=name,
    )(*args)


def _rope_tables(pos):
    half = ROT_DIM // 2
    rows = pos.shape[0]
    inv_freq = ROPE_THETA ** (-2.0 * jnp.arange(half, dtype=F32) / ROT_DIM)
    ang = pos.astype(F32)[:, None] * inv_freq[None, :]
    c, s = jnp.cos(ang), jnp.sin(ang)
    rest = HEAD_DIM - ROT_DIM
    cos = jnp.concatenate([c, c, jnp.ones((rows, rest), F32)], axis=-1)
    sa = jnp.concatenate([-s, jnp.zeros((rows, HEAD_DIM - half), F32)], axis=-1)
    sb = jnp.concatenate([jnp.zeros((rows, half), F32), s, jnp.zeros((rows, rest), F32)], axis=-1)
    rep = LANES // HEAD_DIM
    return tuple(jnp.tile(t, (1, rep)) for t in (cos, sa, sb))


def _mem_kernel(*refs, n_parts):
    x_ref = refs[0]
    part_refs = refs[1:1 + n_parts]
    w_part_refs = refs[1 + n_parts:1 + 2 * n_parts]
    g_ref, wq_ref, gq_ref, k_ref, v_ref, wo_ref, o_ref = refs[1 + 2 * n_parts:]
    x = x_ref[...]
    for a_ref, w_ref in zip(part_refs, w_part_refs):
        x = x + _dot(a_ref[...], w_ref[...])
    h = _rms(x, g_ref[...]).astype(BF16)
    q = _dot(h, wq_ref[...])
    gq = gq_ref[...]
    scale = MEM_HEAD_DIM ** -0.5
    outs = []
    for hd in range(MEM_HEADS):
        sl = slice(hd * MEM_HEAD_DIM, (hd + 1) * MEM_HEAD_DIM)
        qh = _rms(q[:, sl], gq) * scale
        s = _dot_nt(qh.astype(BF16), k_ref[:, sl].astype(BF16))
        m = jnp.max(s, axis=-1, keepdims=True)
        p = jnp.exp(s - m)
        l = jnp.sum(p, axis=-1, keepdims=True)
        oh = _dot(p.astype(BF16), v_ref[:, sl].astype(BF16)) / l
        outs.append(oh.astype(BF16))
    o = jnp.concatenate(outs, axis=-1)
    o_ref[...] = x + _dot(o, wo_ref[...])


def _mem_attn(x, parts, w_parts, g, wq, gq, mk, mv, wo):
    b, t, d = x.shape
    m = mk.shape[1]
    tm = min(512, t)
    row_block = lambda width: pl.BlockSpec((None, tm, width), lambda bi, i: (bi, i, 0))
    return pl.pallas_call(
        functools.partial(_mem_kernel, n_parts=len(parts)),
        grid=(b, t // tm),
        in_specs=[row_block(d)] + [row_block(a.shape[2]) for a in parts]
        + [pl.BlockSpec(w.shape, lambda bi, i: (0, 0)) for w in w_parts] + [
            pl.BlockSpec((1, d), lambda bi, i: (0, 0)),
            pl.BlockSpec((d, d), lambda bi, i: (0, 0)),
            pl.BlockSpec((1, MEM_HEAD_DIM), lambda bi, i: (0, 0)),
            pl.BlockSpec((None, m, d), lambda bi, i: (bi, 0, 0)),
            pl.BlockSpec((None, m, d), lambda bi, i: (bi, 0, 0)),
            pl.BlockSpec((d, d), lambda bi, i: (0, 0)),
        ],
        out_specs=row_block(d),
        out_shape=jax.ShapeDtypeStruct((b, t, d), F32),
        compiler_params=_params("parallel", "parallel"),
        name="mem_attn",
    )(x, *parts, *w_parts, g.reshape(1, d), wq, gq.reshape(1, MEM_HEAD_DIM), mk, mv, wo)


def _history_rows(past_ref, new_ref):
    return jnp.concatenate([past_ref[...].astype(BF16), new_ref[...].astype(BF16)], axis=0)


def _hist_specs(tq, tp):
    def rows(n):
        return pl.BlockSpec((None, n, LANES), lambda bi, h: (bi, 0, h))
    return [rows(tq), rows(tp), rows(tq), rows(tp), rows(tq)]


def _diff_hist_kernel(lam_ref, g_ref, q_ref, kp_ref, kn_ref, vp_ref, vn_ref, o_ref, *, lam_init):
    q = q_ref[...]
    kb = _history_rows(kp_ref, kn_ref)
    vb = _history_rows(vp_ref, vn_ref)
    tq, tk = q.shape[0], kb.shape[0]
    vis = (lax.broadcasted_iota(jnp.int32, (tq, tk), 1) // CHUNK
           <= (kp_ref.shape[0] + lax.broadcasted_iota(jnp.int32, (tq, tk), 0)) // CHUNK)
    outs = []
    for qm in _head_pair_queries(q):
        s = jnp.where(vis, _dot_nt(qm, kb), NEG)
        p = jnp.exp2(s - jnp.max(s, axis=-1, keepdims=True))
        outs.append(_dot(p.astype(BF16), vb) / jnp.sum(p, axis=-1, keepdims=True))
    o = outs[0] - _lambda(lam_ref, lam_init) * outs[1]
    o_ref[...] = (_rms(o, g_ref[...]) * (1.0 - lam_init)).astype(o_ref.dtype)


def _diff_attn_hist(lam_vecs, subln_g, q, k_past, k_new, v_past, v_new, *, lam_init):
    b, tq, width = q.shape
    tp = k_past.shape[1]
    return pl.pallas_call(
        functools.partial(_diff_hist_kernel, lam_init=lam_init),
        grid=(b, width // LANES),
        in_specs=[
            pl.BlockSpec((4, HEAD_DIM), lambda bi, h: (0, 0)),
            pl.BlockSpec((1, LANES), lambda bi, h: (0, 0)),
        ] + _hist_specs(tq, tp),
        out_specs=pl.BlockSpec((None, tq, LANES), lambda bi, h: (bi, 0, h)),
        out_shape=jax.ShapeDtypeStruct((b, tq, width), BF16),
        compiler_params=_params("parallel", "parallel"),
        name="diff_attn_hist",
    )(lam_vecs, subln_g.reshape(1, LANES), q, k_past, k_new, v_past, v_new)


def _diff_seq_kernel(lam_ref, g_ref, q_ref, k_ref, vt_ref, o_ref, m_ref, acc_ref, sa_ref, sb_ref,
                     *, t_blk, lam_init):
    i = pl.program_id(2)
    q2 = jnp.concatenate(_head_pair_queries(q_ref[...]), axis=0)
    q2t = q2.astype(F32).T.astype(BF16)
    m_ref[...] = jnp.full_like(m_ref, NEG)
    acc_ref[...] = jnp.zeros_like(acc_ref)

    def scores(j, dst_ref):
        dst_ref[...] = _dot(k_ref[j], q2t)

    def update(src_ref, j, masked):
        st = src_ref[...]
        if masked:
            vis = (lax.broadcasted_iota(jnp.int32, (t_blk, 2 * t_blk), 0) // CHUNK
                   <= (lax.broadcasted_iota(jnp.int32, (t_blk, 2 * t_blk), 1) % t_blk) // CHUNK)
            st = jnp.where(vis, st, NEG)
        m_old = m_ref[...]
        m_new = jnp.maximum(m_old, jnp.max(st, axis=0, keepdims=True))
        alpha = jnp.exp2(m_old - m_new)
        pt = jnp.exp2(st - m_new)
        vext = jnp.concatenate([vt_ref[j], jnp.ones((SUM_ROWS, t_blk), BF16)], axis=0)
        acc_ref[...] = alpha * acc_ref[...] + _dot(vext, pt.astype(BF16))
        m_ref[...] = m_new

    scores(0, sa_ref)

    def body(jj, carry):
        j = 2 * jj
        scores(j + 1, sb_ref)
        update(sa_ref, j, False)
        scores(j + 2, sa_ref)
        update(sb_ref, j + 1, False)
        return carry

    lax.fori_loop(0, i // 2, body, 0)

    @pl.when(i % 2 == 1)
    def _():
        scores(i, sb_ref)
        update(sa_ref, i - 1, False)
        update(sb_ref, i, True)

    @pl.when(i % 2 == 0)
    def _():
        update(sa_ref, i, True)

    o12 = acc_ref[0:LANES, :] / acc_ref[LANES:LANES + 1, :]
    ot = o12[:, :t_blk] - _lambda(lam_ref, lam_init) * o12[:, t_blk:]
    ot = ot * lax.rsqrt(jnp.mean(ot * ot, axis=0, keepdims=True) + RMS_EPS) * g_ref[...] * (1.0 - lam_init)
    o_ref[...] = ot.T.astype(o_ref.dtype)


def _diff_attn_seq(lam_vecs, subln_g, q, k, vt, *, t_blk, lam_init):
    b, t, width = q.shape
    n_heads = width // LANES
    n_blk = t // t_blk
    return pl.pallas_call(
        functools.partial(_diff_seq_kernel, t_blk=t_blk, lam_init=lam_init),
        grid=(b, n_heads, n_blk),
        in_specs=[
            pl.BlockSpec((4, HEAD_DIM), lambda bi, h, i: (0, 0)),
            pl.BlockSpec((LANES, 1), lambda bi, h, i: (0, 0)),
            pl.BlockSpec((None, t_blk, LANES), lambda bi, h, i: (bi, i, h)),
            pl.BlockSpec((None, n_blk, t_blk, LANES), lambda bi, h, i: (bi, 0, 0, h)),
            pl.BlockSpec((None, None, n_blk, LANES, t_blk), lambda bi, h, i: (h, bi, 0, 0, 0)),
        ],
        out_specs=pl.BlockSpec((None, t_blk, LANES), lambda bi, h, i: (bi, i, h)),
        out_shape=jax.ShapeDtypeStruct((b, t, width), BF16),
        scratch_shapes=[pltpu.VMEM((1, 2 * t_blk), F32), pltpu.VMEM((LANES + SUM_ROWS, 2 * t_blk), F32),
                        pltpu.VMEM((t_blk, 2 * t_blk), F32), pltpu.VMEM((t_blk, 2 * t_blk), F32)],
        compiler_params=_params("parallel", "parallel", "arbitrary"),
        name="diff_attn_seq",
    )(lam_vecs, subln_g.reshape(LANES, 1), q, k.reshape(b, n_blk, t_blk, width), vt)


def _upper(n):
    return jnp.asarray((np.arange(n)[:, None] > np.arange(n)[None, :]).astype(np.float32), dtype=BF16)


def _stick_stages(zs, vbs, u, causal, r_olds):
    sps = [jnp.maximum(z, 0.0) + jnp.log(1.0 + jnp.exp(-jnp.abs(z))) for z in zs]
    log_1ms = [-sp if causal is None else jnp.where(causal, -sp, 0.0) for sp in sps]
    splits = [_split_bf16(log_1m) for log_1m in log_1ms]
    afters = [_dot(hi, u) + _dot(lo, u) for hi, lo in splits]
    ws = [jnp.exp((z - sp) + after + r_old) for z, sp, after, r_old in zip(zs, sps, afters, r_olds)]
    if causal is not None:
        ws = [jnp.where(causal, w, 0.0) for w in ws]
    wvs = [_dot(w.astype(BF16), vb) for w, vb in zip(ws, vbs)]
    return wvs, [after[:, 0:1] + log_1m[:, 0:1] for after, log_1m in zip(afters, log_1ms)]


def _stick_hist_kernel(u_ref, q_ref, kp_ref, kn_ref, vp_ref, vn_ref, o_ref):
    q = q_ref[...]
    kb = _history_rows(kp_ref, kn_ref)
    vb = _history_rows(vp_ref, vn_ref)
    tq, tk = q.shape[0], kb.shape[0]
    causal = (lax.broadcasted_iota(jnp.int32, (tq, tk), 1)
              < kp_ref.shape[0] + lax.broadcasted_iota(jnp.int32, (tq, tk), 0))
    zs = [_dot_nt(qm, kb) for qm in _head_pair_queries(q)]
    outs, _ = _stick_stages(zs, [vb, vb], u_ref[...], causal, [0.0, 0.0])
    o_ref[...] = jnp.where(_lane_masks(o_ref.shape), outs[0], outs[1]).astype(o_ref.dtype)


def _stick_attn_hist(q, k_past, k_new, v_past, v_new):
    b, tq, width = q.shape
    tp = k_past.shape[1]
    return pl.pallas_call(
        _stick_hist_kernel,
        grid=(b, width // LANES),
        in_specs=[pl.BlockSpec((tp + tq, tp + tq), lambda bi, h: (0, 0))] + _hist_specs(tq, tp),
        out_specs=pl.BlockSpec((None, tq, LANES), lambda bi, h: (bi, 0, h)),
        out_shape=jax.ShapeDtypeStruct((b, tq, width), BF16),
        compiler_params=_params("parallel", "parallel"),
        name="stick_attn_hist",
    )(_upper(tp + tq), q, k_past, k_new, v_past, v_new)


def _stick_seq_kernel(u_ref, q_ref, k_ref, v_ref, o_ref, acc_ref, r_ref, z_ref, *, t_blk, n_pairs):
    i = pl.program_id(2)
    lanes = [slice(p * LANES, (p + 1) * LANES) for p in range(n_pairs)]
    q2s = [jnp.concatenate(_head_pair_queries(q_ref[:, sl]), axis=0) for sl in lanes]
    acc_ref[...] = jnp.zeros_like(acc_ref)
    r_ref[...] = jnp.zeros_like(r_ref)
    u = u_ref[...]

    def block(j, masked):
        causal = None
        if masked:
            causal = (lax.broadcasted_iota(jnp.int32, (2 * t_blk, t_blk), 1)
                      < lax.broadcasted_iota(jnp.int32, (2 * t_blk, t_blk), 0) % t_blk)
        zs = [z_ref[p] for p in range(n_pairs)]
        k_next = k_ref[jnp.maximum(j - 1, 0)]
        for p in range(n_pairs):
            z_ref[p] = _dot_nt(q2s[p], k_next[:, lanes[p]])
        vb = v_ref[j]
        r_olds = [r_ref[p] for p in range(n_pairs)]
        wvs, r_blks = _stick_stages(zs, [vb[:, sl] for sl in lanes], u, causal, r_olds)
        r_max = None
        for p in range(n_pairs):
            acc_ref[p] += wvs[p]
            r_new = r_olds[p] + r_blks[p]
            r_ref[p] = r_new
            mx = jnp.max(r_new)
            r_max = mx if r_max is None else jnp.maximum(r_max, mx)
        return r_max

    k_diag = k_ref[i]
    for p in range(n_pairs):
        z_ref[p] = _dot_nt(q2s[p], k_diag[:, lanes[p]])
    block(i, True)

    def cond(carry):
        j, done = carry
        return jnp.logical_and(j >= 0, done == 0)

    def body(carry):
        j, _ = carry
        r_max = block(j, False)
        return j - 1, (r_max < EXP_UNDERFLOW).astype(jnp.int32)

    lax.while_loop(cond, body, (i - 1, jnp.int32(0)))
    for p in range(n_pairs):
        acc = acc_ref[p]
        o_ref[:, lanes[p]] = jnp.where(_lane_masks((t_blk, LANES)), acc[:t_blk], acc[t_blk:]).astype(o_ref.dtype)


def _stick_attn_seq(q, k, v, *, t_blk):
    b, t, width = q.shape
    n_blk = t // t_blk
    n_pairs = STICK_PAIRS
    w = n_pairs * LANES
    kv_spec = pl.BlockSpec((None, n_blk, t_blk, w), lambda bi, h, i: (bi, 0, 0, h))
    return pl.pallas_call(
        functools.partial(_stick_seq_kernel, t_blk=t_blk, n_pairs=n_pairs),
        grid=(b, width // w, n_blk),
        in_specs=[
            pl.BlockSpec((t_blk, t_blk), lambda bi, h, i: (0, 0)),
            pl.BlockSpec((None, t_blk, w), lambda bi, h, i: (bi, i, h)),
            kv_spec, kv_spec,
        ],
        out_specs=pl.BlockSpec((None, t_blk, w), lambda bi, h, i: (bi, i, h)),
        out_shape=jax.ShapeDtypeStruct((b, t, width), BF16),
        scratch_shapes=[pltpu.VMEM((n_pairs, 2 * t_blk, LANES), F32), pltpu.VMEM((n_pairs, 2 * t_blk, 1), F32),
                        pltpu.VMEM((n_pairs, 2 * t_blk, t_blk), F32)],
        compiler_params=_params("parallel", "parallel", "arbitrary"),
        name="stick_attn_seq",
    )(_upper(t_blk), q, k.reshape(b, n_blk, t_blk, width), v.reshape(b, n_blk, t_blk, width))


def _band_hist_kernel(bias_ref, q_ref, kp_ref, kn_ref, vp_ref, vn_ref, o_ref):
    kb = _history_rows(kp_ref, kn_ref)
    vb = _history_rows(vp_ref, vn_ref)
    outs = []
    for hd, qm in enumerate(_head_pair_queries(q_ref[...])):
        s = _dot_nt(qm, kb) + bias_ref[hd]
        p = jnp.exp2(s - jnp.max(s, axis=-1, keepdims=True))
        outs.append(_dot(p.astype(BF16), vb) / jnp.sum(p, axis=-1, keepdims=True))
    o_ref[...] = jnp.where(_lane_masks(o_ref.shape), outs[0], outs[1]).astype(o_ref.dtype)


def _band_bias(bias_table, qpos, kpos):
    tq, tk = len(qpos), len(kpos)
    n = tq + tk
    shift = np.arange(n)
    c_minus_r = np.where(shift < tk, shift, shift - n)
    rel = np.clip(int(qpos[0] - kpos[0]) - c_minus_r, -REL_CLIP, REL_CLIP) + REL_CLIP
    vec = bias_table.astype(F32)[:, rel] * LOG2E
    toeplitz = jnp.tile(vec, (1, tq))[:, :tq * (n - 1)].reshape(-1, tq, n - 1)[:, :, :tk]
    qc, kc = qpos[:, None] // CHUNK, kpos[None, :] // CHUNK
    vis = (kpos[None, :] >= 0) & (kc <= qc) & (qc - kc <= C_PAST_CHUNKS)
    return jnp.where(jnp.asarray(vis)[None], toeplitz, NEG)


def _band_attn_hist(bias, q, k_past, k_new, v_past, v_new):
    b, tq, width = q.shape
    tp = k_past.shape[1]
    return pl.pallas_call(
        _band_hist_kernel,
        grid=(b, width // LANES),
        in_specs=[pl.BlockSpec((2, tq, tp + tq), lambda bi, h: (h, 0, 0))] + _hist_specs(tq, tp),
        out_specs=pl.BlockSpec((None, tq, LANES), lambda bi, h: (bi, 0, h)),
        out_shape=jax.ShapeDtypeStruct((b, tq, width), BF16),
        compiler_params=_params("parallel", "parallel"),
        name="band_attn_hist",
    )(bias, q, k_past, k_new, v_past, v_new)


def _band_seq_kernel(bias_ref, q_ref, kp_ref, kc_ref, vp_ref, vc_ref, o_ref, *, tq, n_sub, n_win):
    big = n_sub * tq
    win = n_win * tq
    half = vp_ref.shape[2]
    prev_ok = pl.program_id(2) > 0
    top = lax.broadcasted_iota(jnp.int32, (LANES, tq), 0) < HEAD_DIM

    def value_piece(ref, off):
        return ref[off // half][:, off % half:off % half + tq]

    def scores(g):
        q2 = jnp.concatenate(_head_pair_queries(q_ref[g * tq:(g + 1) * tq, :]), axis=0)
        q2t = q2.astype(F32).T.astype(BF16)
        start = big - (n_win - 1) * tq + g * tq
        n_prev = max(big - start, 0)
        k_pieces, v_pieces = [], []
        for d in range(n_win):
            off = start + d * tq
            if off < big:
                k_pieces.append(kp_ref[off:off + tq, :])
                v_pieces.append(value_piece(vp_ref, off))
            else:
                k_pieces.append(kc_ref[off - big:off - big + tq, :])
                v_pieces.append(value_piece(vc_ref, off - big))
        st = _dot(jnp.concatenate(k_pieces, axis=0), q2t) + bias_ref[...]
        if n_prev > 0:
            before_start = jnp.logical_and(lax.broadcasted_iota(jnp.int32, (win, 2 * tq), 0) < n_prev,
                                           jnp.logical_not(prev_ok))
            st = jnp.where(before_start, NEG, st)
        return st, jnp.concatenate(v_pieces, axis=1)

    def finish(g, st, vwt):
        pt = jnp.exp2(st - jnp.max(st, axis=0, keepdims=True))
        ot = _dot(vwt, pt.astype(BF16)) / jnp.sum(pt, axis=0, keepdims=True)
        o_ref[g * tq:(g + 1) * tq, :] = jnp.where(top, ot[:, :tq], ot[:, tq:]).T.astype(o_ref.dtype)

    nxt = scores(0)
    for g in range(n_sub):
        cur_scores = nxt
        if g + 1 < n_sub:
            nxt = scores(g + 1)
        finish(g, *cur_scores)


def _band_attn_seq(bias_t, q, k, vt, *, tq, n_win, n_sub):
    b, t, width = q.shape
    big = n_sub * tq
    half = vt.shape[3]
    n_half = big // half
    assert (n_win - 1) * tq <= big, "the window must fit in the previous + current block"
    assert half % tq == 0, "a window piece must not straddle two transposed value blocks"
    cur = pl.BlockSpec((None, big, LANES), lambda bi, h, i: (bi, i, h))
    prev = pl.BlockSpec((None, big, LANES), lambda bi, h, i: (bi, jnp.maximum(i - 1, 0), h))
    per_b = t // big
    vcur = pl.BlockSpec((None, n_half, LANES, half), lambda bi, h, i: (h, bi * per_b + i, 0, 0))
    vprev = pl.BlockSpec((None, n_half, LANES, half), lambda bi, h, i: (h, bi * per_b + jnp.maximum(i - 1, 0), 0, 0))
    return pl.pallas_call(
        functools.partial(_band_seq_kernel, tq=tq, n_sub=n_sub, n_win=n_win),
        grid=(b, width // LANES, t // big),
        in_specs=[pl.BlockSpec((None, n_win * tq, 2 * tq), lambda bi, h, i: (h, 0, 0)), cur, prev, cur, vprev, vcur],
        out_specs=cur,
        out_shape=jax.ShapeDtypeStruct((b, t, width), BF16),
        compiler_params=_params("parallel", "parallel", "parallel"),
        name="band_attn_seq",
    )(bias_t, q, k, k, vt, vt)


def _lambda_init(layer):
    return 0.8 - 0.6 * math.exp(-0.3 * layer)


def _tile_gain(g, n):
    return jnp.tile(g.astype(F32), n)


def _trunk(x, pos0, mem_k, mem_v, past_ab, past_c, p):
    b, t, d = x.shape
    rows = b * t
    depth = p['ffn1_g'].shape[0]
    seq = past_ab is None
    new_ab, new_c = [], []
    pos_rows = jnp.tile(pos0 + jnp.arange(t), b)
    x = x.reshape(rows, d)
    heads_per_tile = TILE_N // HEAD_DIM
    for l in range(depth):
        x = _ffn(x, p['ffn1_g'][l], p['ffn1_wg'][l], p['ffn1_wu'][l], p['ffn1_wd'][l])
        li = l // 2
        if l % 2 == 0:
            gains = jnp.stack([_tile_gain(p['a_gq'][li], heads_per_tile), _tile_gain(p['a_gk'][li], heads_per_tile)])
            outs = [("rows", TILE_N, F32)] * 4 + [("rows", TILE_N, BF16)] * 5
            v_dests = (("copy", 1, 0, 1.0),)
            if seq:
                outs = outs + [("heads_t", TILE_N // LANES, BF16)]
                v_dests += (("heads_t", 9, 0),)
            tiles = [
                (0, True, (("copy", 4, 0, SOFTMAX_Q_SCALE),)),
                (1, True, (("copy", 0, 0, 1.0), ("copy", 5, 0, 1.0))),
                (None, False, v_dests),
                (None, False, (("copy", 6, 0, STICK_Q_SCALE),)),
                (None, False, (("copy", 2, 0, 1.0), ("copy", 7, 0, 1.0))),
                (None, False, (("copy", 3, 0, 1.0), ("copy", 8, 0, 1.0))),
            ]
            res = _proj(x, p['mix_g'][l], p['ab_w_in'][li], gains, HEAD_DIM, tiles, outs,
                        rope_tables=_rope_tables(pos_rows), name="proj_ab")
            a_k, a_v, b_k, b_v = (r.reshape(b, t, TILE_N) for r in res[:4])
            qa, ka, qb, kb, vb = (r.reshape(b, t, TILE_N) for r in res[4:9])
            new_ab.append((a_k, a_v, b_k, b_v))
            lam_vecs = jnp.stack([p['a_lq1'][li], p['a_lk1'][li], p['a_lq2'][li], p['a_lk2'][li]]).astype(F32)
            lam_init = _lambda_init(l)
            if seq:
                t_blk = min(SEQ_BLOCK, t)
                vt = res[9].reshape(TILE_N // LANES, b, t // t_blk, LANES, t_blk)
                a_out = _diff_attn_seq(lam_vecs, p['a_subln_g'][li], qa, ka, vt, t_blk=t_blk, lam_init=lam_init)
                b_out = _stick_attn_seq(qb, kb, vb, t_blk=min(STICK_BLOCK, t))
            else:
                pka, pva, pkb, pvb = (c[li].reshape(b, -1, TILE_N) for c in past_ab)
                a_out = _diff_attn_hist(lam_vecs, p['a_subln_g'][li], qa, pka, a_k, pva, a_v, lam_init=lam_init)
                b_out = _stick_attn_hist(qb, pkb, b_k, pvb, b_v)
            w_out = p['ab_w_out'][li]
            half = w_out.shape[0] // 2
            mixed, w_parts = [a_out, b_out], [w_out[:half], w_out[half:]]
        else:
            gains = jnp.stack([_tile_gain(p['c_gq'][li], heads_per_tile), _tile_gain(p['c_gk'][li], heads_per_tile)])
            groups = TILE_N // LANES
            outs = [("rows", d, F32)] * 2 + [("rows", d, BF16)] * 2
            if seq:
                outs = outs + [("heads_t", d // LANES, BF16)]
            tiles = ([(0, False, (("copy", 2, c, SOFTMAX_Q_SCALE),)) for c in (0, TILE_N)]
                     + [(1, False, (("copy", 0, c, 1.0), ("copy", 3, c, 1.0))) for c in (0, TILE_N)]
                     + [(None, False, (("copy", 1, c * TILE_N, 1.0),) + ((("heads_t", 4, c * groups),) if seq else ()))
                        for c in range(d // TILE_N)])
            res = _proj(x, p['mix_g'][l], p['c_w_in'][li], gains, HEAD_DIM, tiles, outs, name="proj_c")
            k_f, v_f, q_c, k_c = (r.reshape(b, t, d) for r in res[:4])
            band = C_PAST_CHUNKS * CHUNK
            if seq:
                tq = min(BAND_BLOCK, t)
                nb = band // tq + 1
                bias = _band_bias(p['c_bias'][li], (nb - 1) * tq + np.arange(tq), np.arange(nb * tq))
                bias_t = jnp.transpose(bias.reshape(-1, 2, tq, nb * tq), (0, 3, 1, 2)).reshape(-1, nb * tq, 2 * tq)
                o = _band_attn_seq(bias_t, q_c, k_c, res[4], tq=tq, n_win=nb, n_sub=min(BAND_GROUP, t // tq))
                keep = min(band, t)
                new_c.append((k_f[:, t - keep:], v_f[:, t - keep:]))
            else:
                pk, pv = (c[li].reshape(b, -1, d) for c in past_c)
                pc = pk.shape[1]
                bias = _band_bias(p['c_bias'][li], pos0 + np.arange(t), pos0 - pc + np.arange(pc + t))
                o = _band_attn_hist(bias, q_c, pk, k_f, pv, v_f)
                assert t <= pc, "the rolling band buffer keeps the newest pc rows"
                new_c.append((jnp.concatenate([pk[:, t:], k_f], axis=1), jnp.concatenate([pv[:, t:], v_f], axis=1)))
            mixed, w_parts = [o], [p['c_w_out'][li]]
        x = _mem_attn(x.reshape(b, t, d), mixed, w_parts, p['mem_g_x'][l], p['mem_wq'][l], p['mem_gq'][l],
                      mem_k[l], mem_v[l], p['mem_wo'][l]).reshape(rows, d)
        x = _ffn(x, p['ffn2_g'][l], p['ffn2_wg'][l], p['ffn2_wu'][l], p['ffn2_wd'][l])
    return x.reshape(b, t, d), new_ab, new_c


def kernel(x_prompt, x_sample, cache_a_k, cache_a_v, cache_b_k, cache_b_v, cache_c_k, cache_c_v, cache_mem_k, cache_mem_v, mem_prompt, ffn1_g, ffn1_wg, ffn1_wu, ffn1_wd, ffn2_g, ffn2_wg, ffn2_wu, ffn2_wd, mix_g, ab_w_in, ab_w_out, a_gq, a_gk, a_lq1, a_lk1, a_lq2, a_lk2, a_subln_g, c_w_in, c_w_out, c_gq, c_gk, c_bias, mem_g_x, mem_g_m, mem_wq, mem_wk, mem_wv, mem_wo, mem_gq, mem_gk):
    depth = ffn1_g.shape[0]
    p = dict(ffn1_g=ffn1_g, ffn2_g=ffn2_g, mix_g=mix_g, a_gq=a_gq, a_gk=a_gk, a_lq1=a_lq1, a_lk1=a_lk1,
             a_lq2=a_lq2, a_lk2=a_lk2, a_subln_g=a_subln_g, c_gq=c_gq, c_gk=c_gk, c_bias=c_bias,
             mem_g_x=mem_g_x, mem_gq=mem_gq)
    for name, w in (('ffn1_wg', ffn1_wg), ('ffn1_wu', ffn1_wu), ('ffn1_wd', ffn1_wd), ('ffn2_wg', ffn2_wg),
                    ('ffn2_wu', ffn2_wu), ('ffn2_wd', ffn2_wd), ('ab_w_in', ab_w_in), ('ab_w_out', ab_w_out),
                    ('c_w_in', c_w_in), ('c_w_out', c_w_out), ('mem_wq', mem_wq), ('mem_wo', mem_wo)):
        p[name] = w.astype(BF16)

    bp, tp, d = x_prompt.shape
    bs, ts, _ = x_sample.shape
    m_len = mem_prompt.shape[1]
    past_len = cache_a_k.shape[2]

    mem_k_list, mem_v_list = [], []
    n_half = d // TILE_N
    for l in range(depth):
        w_kv = jnp.concatenate([mem_wk[l], mem_wv[l]], axis=1).astype(BF16)
        gains = _tile_gain(mem_gk[l], TILE_N // MEM_HEAD_DIM)[None]
        tiles = ([(0, False, (("copy", 0, c * TILE_N, 1.0),)) for c in range(n_half)]
                 + [(None, False, (("copy", 1, c * TILE_N, 1.0),)) for c in range(n_half)])
        mk, mv = _proj(mem_prompt.reshape(bp * m_len, d), mem_g_m[l], w_kv, gains, MEM_HEAD_DIM, tiles,
                       [("rows", d, F32)] * 2, name="proj_mem_kv")
        mem_k_list.append(mk.reshape(bp, m_len, d))
        mem_v_list.append(mv.reshape(bp, m_len, d))

    y_prompt, ab_p, c_p = _trunk(x_prompt, 0, mem_k_list, mem_v_list, None, None, p)
    cache_mem_k2 = cache_mem_k.reshape(depth, bs, m_len, d)
    cache_mem_v2 = cache_mem_v.reshape(depth, bs, m_len, d)
    y_sample, ab_s, c_s = _trunk(x_sample, past_len, cache_mem_k2, cache_mem_v2,
                                 (cache_a_k, cache_a_v, cache_b_k, cache_b_v), (cache_c_k, cache_c_v), p)

    a_heads = cache_a_v.shape[3]
    b_heads = cache_b_k.shape[3]
    c_heads = cache_c_k.shape[3]

    def stack(rows, idx, heads, width):
        return jnp.stack([r[idx].reshape(r[idx].shape[0], r[idx].shape[1], heads, width) for r in rows])

    outs = [y_prompt, y_sample]
    outs += [stack(ab_p, 0, 2 * a_heads, HEAD_DIM), stack(ab_p, 1, a_heads, 2 * HEAD_DIM),
             stack(ab_p, 2, b_heads, HEAD_DIM), stack(ab_p, 3, b_heads, HEAD_DIM),
             stack(c_p, 0, c_heads, HEAD_DIM), stack(c_p, 1, c_heads, HEAD_DIM)]
    outs += [jnp.stack(mem_k_list).reshape(depth, bp, m_len, MEM_HEADS, MEM_HEAD_DIM),
             jnp.stack(mem_v_list).reshape(depth, bp, m_len, MEM_HEADS, MEM_HEAD_DIM)]
    outs += [stack(ab_s, 0, 2 * a_heads, HEAD_DIM), stack(ab_s, 1, a_heads, 2 * HEAD_DIM),
             stack(ab_s, 2, b_heads, HEAD_DIM), stack(ab_s, 3, b_heads, HEAD_DIM),
             stack(c_s, 0, c_heads, HEAD_DIM), stack(c_s, 1, c_heads, HEAD_DIM)]
    return tuple(outs)
```

```python
import functools
import math

import jax
import jax.numpy as jnp
import numpy as np
from jax import lax
from jax.experimental import pallas as pl
from jax.experimental.pallas import tpu as pltpu

D_MODEL = 1024
CHUNK = 64
HEAD_DIM = 64
ROT_DIM = HEAD_DIM // 4
ROPE_THETA = 500000.0
C_PAST_CHUNKS = 8
REL_CLIP = 128
MEM_HEADS = 4
MEM_HEAD_DIM = D_MODEL // MEM_HEADS
RMS_EPS = 1e-6
NEG = -1e30
LOG2E = 1.4426950408889634
LANES = 128
TILE_N = 512
SEQ_BLOCK = 512
STICK_BLOCK = 256
STICK_PAIRS = 2
BAND_BLOCK = 256
BAND_GROUP = 8
FFN_TILE = 256
SUM_ROWS = 16
EXP_UNDERFLOW = -104.0
VMEM_LIMIT = 56 * 1024 * 1024
SOFTMAX_Q_SCALE = HEAD_DIM ** -0.5 * LOG2E
STICK_Q_SCALE = HEAD_DIM ** -0.5

BF16 = jnp.bfloat16
F32 = jnp.float32


def _dot(a, b):
    return jnp.dot(a, b, preferred_element_type=F32)


def _dot_nt(a, b):
    return lax.dot_general(a, b, (((1,), (1,)), ((), ())), preferred_element_type=F32)


def _rms(x, g):
    return x * lax.rsqrt(jnp.mean(x * x, axis=-1, keepdims=True) + RMS_EPS) * g


def _split_bf16(x):
    hi = x.astype(BF16)
    lo = (x - hi.astype(F32)).astype(BF16)
    return hi, lo


def _params(*sem):
    return pltpu.CompilerParams(dimension_semantics=sem, vmem_limit_bytes=VMEM_LIMIT)


def _lane_masks(shape):
    lane = lax.broadcasted_iota(jnp.int32, shape, 1)
    return lane < HEAD_DIM


def _head_pair_queries(q):
    first = _lane_masks(q.shape)
    zero = jnp.zeros_like(q)
    return jnp.where(first, q, zero), jnp.where(first, zero, q)


def _lambda(lam_ref, lam_init):
    lv = lam_ref[...]
    return (jnp.exp(jnp.sum(lv[0:1] * lv[1:2], axis=-1, keepdims=True))
            - jnp.exp(jnp.sum(lv[2:3] * lv[3:4], axis=-1, keepdims=True)) + lam_init)


def _ffn_kernel(x_ref, g_ref, wg_ref, wu_ref, wd_ref, o_ref):
    x = x_ref[...]
    h = _rms(x, g_ref[...]).astype(BF16)
    acc = None
    for f in range(wg_ref.shape[1] // FFN_TILE):
        sl = slice(f * FFN_TILE, (f + 1) * FFN_TILE)
        a = _dot(h, wg_ref[:, sl])
        u = _dot(h, wu_ref[:, sl])
        act = a * (1.0 / (1.0 + jnp.exp(-a))) * u
        part = _dot(act.astype(BF16), wd_ref[sl, :])
        acc = part if acc is None else acc + part
    o_ref[...] = x + 0.5 * acc


def _ffn(x, g, wg, wu, wd):
    rows, d = x.shape
    ff = wg.shape[1]
    tm = min(512, rows)
    once = pl.Buffered(1)
    return pl.pallas_call(
        _ffn_kernel,
        grid=(rows // tm,),
        in_specs=[
            pl.BlockSpec((tm, d), lambda i: (i, 0)),
            pl.BlockSpec((1, d), lambda i: (0, 0)),
            pl.BlockSpec((d, ff), lambda i: (0, 0), pipeline_mode=once),
            pl.BlockSpec((d, ff), lambda i: (0, 0), pipeline_mode=once),
            pl.BlockSpec((ff, d), lambda i: (0, 0), pipeline_mode=once),
        ],
        out_specs=pl.BlockSpec((tm, d), lambda i: (i, 0)),
        out_shape=jax.ShapeDtypeStruct((rows, d), F32),
        compiler_params=_params("parallel"),
        name="ffn",
    )(x, g.reshape(1, d), wg, wu, wd)


def _proj_kernel(*refs, tiles, rope):
    x_ref, g_ref, w_ref, gain_ref, seg_ref = refs[:5]
    n_in = 8 if rope else 5
    out_refs = refs[n_in:]
    h = _rms(x_ref[...], g_ref[...]).astype(BF16)
    for t, (norm, use_rope, dests) in enumerate(tiles):
        y = _dot(h, w_ref[:, t * TILE_N:(t + 1) * TILE_N])
        if norm is not None:
            ms = _dot((y * y).astype(BF16), seg_ref[...])
            y = y * lax.rsqrt(ms + RMS_EPS) * gain_ref[norm:norm + 1, :]
        if use_rope:
            cos, sa, sb = refs[5][...], refs[6][...], refs[7][...]
            half = ROT_DIM // 2
            blocks = []
            for c in range(TILE_N // LANES):
                yb = y[:, c * LANES:(c + 1) * LANES]
                blocks.append(yb * cos + pltpu.roll(yb, LANES - half, 1) * sa + pltpu.roll(yb, half, 1) * sb)
            y = jnp.concatenate(blocks, axis=-1)
        for dest in dests:
            ref = out_refs[dest[1]]
            if dest[0] == "copy":
                _, _, col, scale = dest
                ref[:, col:col + TILE_N] = (y if scale == 1.0 else y * scale).astype(ref.dtype)
            else:
                groups = TILE_N // LANES
                ref[dest[2]:dest[2] + groups] = y.T.reshape((groups,) + ref.shape[1:]).astype(ref.dtype)


def _seg_matrix(seg):
    idx = np.arange(TILE_N) // seg
    return jnp.asarray((idx[:, None] == idx[None, :]).astype(np.float32) / seg, dtype=BF16)


def _proj(x, g, w, gains, seg, tiles, outs, rope_tables=None, name="proj"):
    rows, d = x.shape
    tm = min(SEQ_BLOCK, rows)
    rope = rope_tables is not None
    in_specs = [
        pl.BlockSpec((tm, d), lambda i: (i, 0)),
        pl.BlockSpec((1, d), lambda i: (0, 0)),
        pl.BlockSpec(w.shape, lambda i: (0, 0)),
        pl.BlockSpec(gains.shape, lambda i: (0, 0)),
        pl.BlockSpec((TILE_N, TILE_N), lambda i: (0, 0)),
    ]
    args = [x, g.reshape(1, d), w, gains, _seg_matrix(seg)]
    if rope:
        in_specs += [pl.BlockSpec((tm, LANES), lambda i: (i, 0))] * 3
        args += list(rope_tables)
    out_specs, out_shape = [], []
    for o in outs:
        if o[0] == "rows":
            _, width, dtype = o
            out_specs.append(pl.BlockSpec((tm, width), lambda i: (i, 0)))
            out_shape.append(jax.ShapeDtypeStruct((rows, width), dtype))
        else:
            _, groups, dtype = o
            out_specs.append(pl.BlockSpec((groups, None, LANES, tm), lambda i: (0, i, 0, 0)))
            out_shape.append(jax.ShapeDtypeStruct((groups, rows // tm, LANES, tm), dtype))
    return pl.pallas_call(
        functools.partial(_proj_kernel, tiles=tuple(tiles), rope=rope),
        grid=(rows // tm,),
        in_specs=in_specs,
        out_specs=out_specs,
        out_shape=out_shape,
        compiler_params=_params("parallel"),
        name=name,
    )(*args)


def _rope_tables(pos):
    half = ROT_DIM // 2
    rows = pos.shape[0]
    inv_freq = ROPE_THETA ** (-2.0 * jnp.arange(half, dtype=F32) / ROT_DIM)
    ang = pos.astype(F32)[:, None] * inv_freq[None, :]
    c, s = jnp.cos(ang), jnp.sin(ang)
    rest = HEAD_DIM - ROT_DIM
    cos = jnp.concatenate([c, c, jnp.ones((rows, rest), F32)], axis=-1)
    sa = jnp.concatenate([-s, jnp.zeros((rows, HEAD_DIM - half), F32)], axis=-1)
    sb = jnp.concatenate([jnp.zeros((rows, half), F32), s, jnp.zeros((rows, rest), F32)], axis=-1)
    rep = LANES // HEAD_DIM
    return tuple(jnp.tile(t, (1, rep)) for t in (cos, sa, sb))


def _mem_kernel(*refs, n_parts):
    x_ref = refs[0]
    part_refs = refs[1:1 + n_parts]
    w_part_refs = refs[1 + n_parts:1 + 2 * n_parts]
    g_ref, wq_ref, gq_ref, k_ref, v_ref, wo_ref, o_ref = refs[1 + 2 * n_parts:]
    x = x_ref[...]
    for a_ref, w_ref in zip(part_refs, w_part_refs):
        x = x + _dot(a_ref[...], w_ref[...])
    h = _rms(x, g_ref[...]).astype(BF16)
    q = _dot(h, wq_ref[...])
    gq = gq_ref[...]
    scale = MEM_HEAD_DIM ** -0.5
    outs = []
    for hd in range(MEM_HEADS):
        sl = slice(hd * MEM_HEAD_DIM, (hd + 1) * MEM_HEAD_DIM)
        qh = _rms(q[:, sl], gq) * scale
        s = _dot_nt(qh.astype(BF16), k_ref[:, sl].astype(BF16))
        m = jnp.max(s, axis=-1, keepdims=True)
        p = jnp.exp(s - m)
        l = jnp.sum(p, axis=-1, keepdims=True)
        oh = _dot(p.astype(BF16), v_ref[:, sl].astype(BF16)) / l
        outs.append(oh.astype(BF16))
    o = jnp.concatenate(outs, axis=-1)
    o_ref[...] = x + _dot(o, wo_ref[...])


def _mem_attn(x, parts, w_parts, g, wq, gq, mk, mv, wo, *, mem_row0=0):
    b, t, d = x.shape
    m = mk.shape[1]
    tm = min(512, t)
    row_block = lambda width: pl.BlockSpec((None, tm, width), lambda bi, i: (bi, i, 0))
    mem_block = pl.BlockSpec((None, m, d), lambda bi, i: (mem_row0 + bi, 0, 0))
    return pl.pallas_call(
        functools.partial(_mem_kernel, n_parts=len(parts)),
        grid=(b, t // tm),
        in_specs=[row_block(d)] + [row_block(a.shape[2]) for a in parts]
        + [pl.BlockSpec(w.shape, lambda bi, i: (0, 0)) for w in w_parts] + [
            pl.BlockSpec((1, d), lambda bi, i: (0, 0)),
            pl.BlockSpec((d, d), lambda bi, i: (0, 0)),
            pl.BlockSpec((1, MEM_HEAD_DIM), lambda bi, i: (0, 0)),
            mem_block, mem_block,
            pl.BlockSpec((d, d), lambda bi, i: (0, 0)),
        ],
        out_specs=row_block(d),
        out_shape=jax.ShapeDtypeStruct((b, t, d), F32),
        compiler_params=_params("parallel", "parallel"),
        name="mem_attn",
    )(x, *parts, *w_parts, g.reshape(1, d), wq, gq.reshape(1, MEM_HEAD_DIM), mk, mv, wo)


def _history_rows(past_ref, new_ref, lanes):
    return jnp.concatenate([past_ref[:, lanes].astype(BF16), new_ref[:, lanes].astype(BF16)], axis=0)


def _lane_blocks(width):
    return [slice(p * LANES, (p + 1) * LANES) for p in range(width // LANES)]


def _hist_specs(tq, tp, width, layer_row0):
    new = pl.BlockSpec((None, tq, width), lambda bi: (bi, 0, 0))
    past = pl.BlockSpec((None, tp, width), lambda bi: (layer_row0 + bi, 0, 0))
    return [new, past, new, past, new]


def _diff_hist_kernel(lam_ref, g_ref, q_ref, kp_ref, kn_ref, vp_ref, vn_ref, o_ref, *, lam_init):
    tq, tp = q_ref.shape[0], kp_ref.shape[0]
    vis = (lax.broadcasted_iota(jnp.int32, (tq, tp + tq), 1) // CHUNK
           <= (tp + lax.broadcasted_iota(jnp.int32, (tq, tp + tq), 0)) // CHUNK)
    lam = _lambda(lam_ref, lam_init)
    for lanes in _lane_blocks(q_ref.shape[1]):
        kb = _history_rows(kp_ref, kn_ref, lanes)
        vb = _history_rows(vp_ref, vn_ref, lanes)
        outs = []
        for qm in _head_pair_queries(q_ref[:, lanes]):
            s = jnp.where(vis, _dot_nt(qm, kb), NEG)
            p = jnp.exp2(s - jnp.max(s, axis=-1, keepdims=True))
            outs.append(_dot(p.astype(BF16), vb) / jnp.sum(p, axis=-1, keepdims=True))
        o = outs[0] - lam * outs[1]
        o_ref[:, lanes] = (_rms(o, g_ref[...]) * (1.0 - lam_init)).astype(o_ref.dtype)


def _diff_attn_hist(lam_vecs, subln_g, q, k_past, k_new, v_past, v_new, *, layer_row0, lam_init):
    b, tq, width = q.shape
    tp = k_past.shape[1]
    return pl.pallas_call(
        functools.partial(_diff_hist_kernel, lam_init=lam_init),
        grid=(b,),
        in_specs=[
            pl.BlockSpec((4, HEAD_DIM), lambda bi: (0, 0)),
            pl.BlockSpec((1, LANES), lambda bi: (0, 0)),
        ] + _hist_specs(tq, tp, width, layer_row0),
        out_specs=pl.BlockSpec((None, tq, width), lambda bi: (bi, 0, 0)),
        out_shape=jax.ShapeDtypeStruct((b, tq, width), BF16),
        compiler_params=_params("parallel"),
        name="diff_attn_hist",
    )(lam_vecs, subln_g.reshape(1, LANES), q, k_past, k_new, v_past, v_new)


def _diff_seq_kernel(lam_ref, g_ref, q_ref, k_ref, vt_ref, o_ref, m_ref, acc_ref, sa_ref, sb_ref,
                     *, t_blk, lam_init):
    i = pl.program_id(2)
    q2 = jnp.concatenate(_head_pair_queries(q_ref[...]), axis=0)
    q2t = q2.astype(F32).T.astype(BF16)
    m_ref[...] = jnp.full_like(m_ref, NEG)
    acc_ref[...] = jnp.zeros_like(acc_ref)

    def scores(j, dst_ref):
        dst_ref[...] = _dot(k_ref[j], q2t)

    def update(src_ref, j, masked):
        st = src_ref[...]
        if masked:
            vis = (lax.broadcasted_iota(jnp.int32, (t_blk, 2 * t_blk), 0) // CHUNK
                   <= (lax.broadcasted_iota(jnp.int32, (t_blk, 2 * t_blk), 1) % t_blk) // CHUNK)
            st = jnp.where(vis, st, NEG)
        m_old = m_ref[...]
        m_new = jnp.maximum(m_old, jnp.max(st, axis=0, keepdims=True))
        alpha = jnp.exp2(m_old - m_new)
        pt = jnp.exp2(st - m_new)
        vext = jnp.concatenate([vt_ref[j], jnp.ones((SUM_ROWS, t_blk), BF16)], axis=0)
        acc_ref[...] = alpha * acc_ref[...] + _dot(vext, pt.astype(BF16))
        m_ref[...] = m_new

    scores(0, sa_ref)

    def body(jj, carry):
        j = 2 * jj
        scores(j + 1, sb_ref)
        update(sa_ref, j, False)
        scores(j + 2, sa_ref)
        update(sb_ref, j + 1, False)
        return carry

    lax.fori_loop(0, i // 2, body, 0)

    @pl.when(i % 2 == 1)
    def _():
        scores(i, sb_ref)
        update(sa_ref, i - 1, False)
        update(sb_ref, i, True)

    @pl.when(i % 2 == 0)
    def _():
        update(sa_ref, i, True)

    o12 = acc_ref[0:LANES, :] / acc_ref[LANES:LANES + 1, :]
    ot = o12[:, :t_blk] - _lambda(lam_ref, lam_init) * o12[:, t_blk:]
    ot = ot * lax.rsqrt(jnp.mean(ot * ot, axis=0, keepdims=True) + RMS_EPS) * g_ref[...] * (1.0 - lam_init)
    o_ref[...] = ot.T.astype(o_ref.dtype)


def _diff_attn_seq(lam_vecs, subln_g, q, k, vt, *, t_blk, lam_init):
    b, t, width = q.shape
    n_heads = width // LANES
    n_blk = t // t_blk
    return pl.pallas_call(
        functools.partial(_diff_seq_kernel, t_blk=t_blk, lam_init=lam_init),
        grid=(b, n_heads, n_blk),
        in_specs=[
            pl.BlockSpec((4, HEAD_DIM), lambda bi, h, i: (0, 0)),
            pl.BlockSpec((LANES, 1), lambda bi, h, i: (0, 0)),
            pl.BlockSpec((None, t_blk, LANES), lambda bi, h, i: (bi, i, h)),
            pl.BlockSpec((None, n_blk, t_blk, LANES), lambda bi, h, i: (bi, 0, 0, h)),
            pl.BlockSpec((None, None, n_blk, LANES, t_blk), lambda bi, h, i: (h, bi, 0, 0, 0)),
        ],
        out_specs=pl.BlockSpec((None, t_blk, LANES), lambda bi, h, i: (bi, i, h)),
        out_shape=jax.ShapeDtypeStruct((b, t, width), BF16),
        scratch_shapes=[pltpu.VMEM((1, 2 * t_blk), F32), pltpu.VMEM((LANES + SUM_ROWS, 2 * t_blk), F32),
                        pltpu.VMEM((t_blk, 2 * t_blk), F32), pltpu.VMEM((t_blk, 2 * t_blk), F32)],
        compiler_params=_params("parallel", "parallel", "arbitrary"),
        name="diff_attn_seq",
    )(lam_vecs, subln_g.reshape(LANES, 1), q, k.reshape(b, n_blk, t_blk, width), vt)


def _upper(n):
    return jnp.asarray((np.arange(n)[:, None] > np.arange(n)[None, :]).astype(np.float32), dtype=BF16)


def _stick_stages(zs, vbs, u, causal, r_olds):
    sps = [jnp.maximum(z, 0.0) + jnp.log(1.0 + jnp.exp(-jnp.abs(z))) for z in zs]
    log_1ms = [-sp if causal is None else jnp.where(causal, -sp, 0.0) for sp in sps]
    splits = [_split_bf16(log_1m) for log_1m in log_1ms]
    afters = [_dot(hi, u) + _dot(lo, u) for hi, lo in splits]
    ws = [jnp.exp((z - sp) + after + r_old) for z, sp, after, r_old in zip(zs, sps, afters, r_olds)]
    if causal is not None:
        ws = [jnp.where(causal, w, 0.0) for w in ws]
    wvs = [_dot(w.astype(BF16), vb) for w, vb in zip(ws, vbs)]
    return wvs, [after[:, 0:1] + log_1m[:, 0:1] for after, log_1m in zip(afters, log_1ms)]


def _stick_hist_kernel(u_ref, q_ref, kp_ref, kn_ref, vp_ref, vn_ref, o_ref):
    tq, tp = q_ref.shape[0], kp_ref.shape[0]
    causal = (lax.broadcasted_iota(jnp.int32, (tq, tp + tq), 1)
              < tp + lax.broadcasted_iota(jnp.int32, (tq, tp + tq), 0))
    for lanes in _lane_blocks(q_ref.shape[1]):
        kb = _history_rows(kp_ref, kn_ref, lanes)
        vb = _history_rows(vp_ref, vn_ref, lanes)
        zs = [_dot_nt(qm, kb) for qm in _head_pair_queries(q_ref[:, lanes])]
        outs, _ = _stick_stages(zs, [vb, vb], u_ref[...], causal, [0.0, 0.0])
        o_ref[:, lanes] = jnp.where(_lane_masks((tq, LANES)), outs[0], outs[1]).astype(o_ref.dtype)


def _stick_attn_hist(q, k_past, k_new, v_past, v_new, *, layer_row0):
    b, tq, width = q.shape
    tp = k_past.shape[1]
    return pl.pallas_call(
        _stick_hist_kernel,
        grid=(b,),
        in_specs=[pl.BlockSpec((tp + tq, tp + tq), lambda bi: (0, 0))] + _hist_specs(tq, tp, width, layer_row0),
        out_specs=pl.BlockSpec((None, tq, width), lambda bi: (bi, 0, 0)),
        out_shape=jax.ShapeDtypeStruct((b, tq, width), BF16),
        compiler_params=_params("parallel"),
        name="stick_attn_hist",
    )(_upper(tp + tq), q, k_past, k_new, v_past, v_new)


def _stick_seq_kernel(u_ref, q_ref, k_ref, v_ref, o_ref, acc_ref, r_ref, z_ref, *, t_blk, n_pairs):
    i = pl.program_id(2)
    lanes = [slice(p * LANES, (p + 1) * LANES) for p in range(n_pairs)]
    q2s = [jnp.concatenate(_head_pair_queries(q_ref[:, sl]), axis=0) for sl in lanes]
    acc_ref[...] = jnp.zeros_like(acc_ref)
    r_ref[...] = jnp.zeros_like(r_ref)
    u = u_ref[...]

    def block(j, masked):
        causal = None
        if masked:
            causal = (lax.broadcasted_iota(jnp.int32, (2 * t_blk, t_blk), 1)
                      < lax.broadcasted_iota(jnp.int32, (2 * t_blk, t_blk), 0) % t_blk)
        zs = [z_ref[p] for p in range(n_pairs)]
        k_next = k_ref[jnp.maximum(j - 1, 0)]
        for p in range(n_pairs):
            z_ref[p] = _dot_nt(q2s[p], k_next[:, lanes[p]])
        vb = v_ref[j]
        r_olds = [r_ref[p] for p in range(n_pairs)]
        wvs, r_blks = _stick_stages(zs, [vb[:, sl] for sl in lanes], u, causal, r_olds)
        r_max = None
        for p in range(n_pairs):
            acc_ref[p] += wvs[p]
            r_new = r_olds[p] + r_blks[p]
            r_ref[p] = r_new
            mx = jnp.max(r_new)
            r_max = mx if r_max is None else jnp.maximum(r_max, mx)
        return r_max

    k_diag = k_ref[i]
    for p in range(n_pairs):
        z_ref[p] = _dot_nt(q2s[p], k_diag[:, lanes[p]])
    block(i, True)

    def cond(carry):
        j, done = carry
        return jnp.logical_and(j >= 0, done == 0)

    def body(carry):
        j, _ = carry
        r_max = block(j, False)
        return j - 1, (r_max < EXP_UNDERFLOW).astype(jnp.int32)

    lax.while_loop(cond, body, (i - 1, jnp.int32(0)))
    for p in range(n_pairs):
        acc = acc_ref[p]
        o_ref[:, lanes[p]] = jnp.where(_lane_masks((t_blk, LANES)), acc[:t_blk], acc[t_blk:]).astype(o_ref.dtype)


def _stick_attn_seq(q, k, v, *, t_blk):
    b, t, width = q.shape
    n_blk = t // t_blk
    n_pairs = STICK_PAIRS
    w = n_pairs * LANES
    kv_spec = pl.BlockSpec((None, n_blk, t_blk, w), lambda bi, h, i: (bi, 0, 0, h))
    return pl.pallas_call(
        functools.partial(_stick_seq_kernel, t_blk=t_blk, n_pairs=n_pairs),
        grid=(b, width // w, n_blk),
        in_specs=[
            pl.BlockSpec((t_blk, t_blk), lambda bi, h, i: (0, 0)),
            pl.BlockSpec((None, t_blk, w), lambda bi, h, i: (bi, i, h)),
            kv_spec, kv_spec,
        ],
        out_specs=pl.BlockSpec((None, t_blk, w), lambda bi, h, i: (bi, i, h)),
        out_shape=jax.ShapeDtypeStruct((b, t, width), BF16),
        scratch_shapes=[pltpu.VMEM((n_pairs, 2 * t_blk, LANES), F32), pltpu.VMEM((n_pairs, 2 * t_blk, 1), F32),
                        pltpu.VMEM((n_pairs, 2 * t_blk, t_blk), F32)],
        compiler_params=_params("parallel", "parallel", "arbitrary"),
        name="stick_attn_seq",
    )(_upper(t_blk), q, k.reshape(b, n_blk, t_blk, width), v.reshape(b, n_blk, t_blk, width))


def _band_hist_kernel(bias_ref, q_ref, kp_ref, kn_ref, vp_ref, vn_ref, o_ref):
    tq = q_ref.shape[0]
    for pair, lanes in enumerate(_lane_blocks(q_ref.shape[1])):
        kb = _history_rows(kp_ref, kn_ref, lanes)
        vb = _history_rows(vp_ref, vn_ref, lanes)
        outs = []
        for hd, qm in enumerate(_head_pair_queries(q_ref[:, lanes])):
            s = _dot_nt(qm, kb) + bias_ref[2 * pair + hd]
            p = jnp.exp2(s - jnp.max(s, axis=-1, keepdims=True))
            outs.append(_dot(p.astype(BF16), vb) / jnp.sum(p, axis=-1, keepdims=True))
        o_ref[:, lanes] = jnp.where(_lane_masks((tq, LANES)), outs[0], outs[1]).astype(o_ref.dtype)


def _band_bias(bias_table, qpos, kpos):
    tq, tk = len(qpos), len(kpos)
    n = tq + tk
    shift = np.arange(n)
    c_minus_r = np.where(shift < tk, shift, shift - n)
    rel = np.clip(int(qpos[0] - kpos[0]) - c_minus_r, -REL_CLIP, REL_CLIP) + REL_CLIP
    vec = bias_table.astype(F32)[:, rel] * LOG2E
    toeplitz = jnp.tile(vec, (1, tq))[:, :tq * (n - 1)].reshape(-1, tq, n - 1)[:, :, :tk]
    qc, kc = qpos[:, None] // CHUNK, kpos[None, :] // CHUNK
    vis = (kpos[None, :] >= 0) & (kc <= qc) & (qc - kc <= C_PAST_CHUNKS)
    return jnp.where(jnp.asarray(vis)[None], toeplitz, NEG)


def _band_attn_hist(bias, q, k_past, k_new, v_past, v_new, *, layer_row0):
    b, tq, width = q.shape
    tp = k_past.shape[1]
    return pl.pallas_call(
        _band_hist_kernel,
        grid=(b,),
        in_specs=[pl.BlockSpec(bias.shape, lambda bi: (0, 0, 0))] + _hist_specs(tq, tp, width, layer_row0),
        out_specs=pl.BlockSpec((None, tq, width), lambda bi: (bi, 0, 0)),
        out_shape=jax.ShapeDtypeStruct((b, tq, width), BF16),
        compiler_params=_params("parallel"),
        name="band_attn_hist",
    )(bias, q, k_past, k_new, v_past, v_new)


def _band_seq_kernel(bias_ref, q_ref, kp_ref, kc_ref, vp_ref, vc_ref, o_ref, *, tq, n_sub, n_win):
    big = n_sub * tq
    win = n_win * tq
    half = vp_ref.shape[2]
    prev_ok = pl.program_id(2) > 0
    top = lax.broadcasted_iota(jnp.int32, (LANES, tq), 0) < HEAD_DIM

    def value_piece(ref, off):
        return ref[off // half][:, off % half:off % half + tq]

    def scores(g):
        q2 = jnp.concatenate(_head_pair_queries(q_ref[g * tq:(g + 1) * tq, :]), axis=0)
        q2t = q2.astype(F32).T.astype(BF16)
        start = big - (n_win - 1) * tq + g * tq
        n_prev = max(big - start, 0)
        k_pieces, v_pieces = [], []
        for d in range(n_win):
            off = start + d * tq
            if off < big:
                k_pieces.append(kp_ref[off:off + tq, :])
                v_pieces.append(value_piece(vp_ref, off))
            else:
                k_pieces.append(kc_ref[off - big:off - big + tq, :])
                v_pieces.append(value_piece(vc_ref, off - big))
        st = _dot(jnp.concatenate(k_pieces, axis=0), q2t) + bias_ref[...]
        if n_prev > 0:
            before_start = jnp.logical_and(lax.broadcasted_iota(jnp.int32, (win, 2 * tq), 0) < n_prev,
                                           jnp.logical_not(prev_ok))
            st = jnp.where(before_start, NEG, st)
        return st, jnp.concatenate(v_pieces, axis=1)

    def finish(g, st, vwt):
        pt = jnp.exp2(st - jnp.max(st, axis=0, keepdims=True))
        ot = _dot(vwt, pt.astype(BF16)) / jnp.sum(pt, axis=0, keepdims=True)
        o_ref[g * tq:(g + 1) * tq, :] = jnp.where(top, ot[:, :tq], ot[:, tq:]).T.astype(o_ref.dtype)

    nxt = scores(0)
    for g in range(n_sub):
        cur_scores = nxt
        if g + 1 < n_sub:
            nxt = scores(g + 1)
        finish(g, *cur_scores)


def _band_attn_seq(bias_t, q, k, vt, *, tq, n_win, n_sub):
    b, t, width = q.shape
    big = n_sub * tq
    half = vt.shape[3]
    n_half = big // half
    assert (n_win - 1) * tq <= big, "the window must fit in the previous + current block"
    assert half % tq == 0, "a window piece must not straddle two transposed value blocks"
    cur = pl.BlockSpec((None, big, LANES), lambda bi, h, i: (bi, i, h))
    prev = pl.BlockSpec((None, big, LANES), lambda bi, h, i: (bi, jnp.maximum(i - 1, 0), h))
    per_b = t // big
    vcur = pl.BlockSpec((None, n_half, LANES, half), lambda bi, h, i: (h, bi * per_b + i, 0, 0))
    vprev = pl.BlockSpec((None, n_half, LANES, half), lambda bi, h, i: (h, bi * per_b + jnp.maximum(i - 1, 0), 0, 0))
    return pl.pallas_call(
        functools.partial(_band_seq_kernel, tq=tq, n_sub=n_sub, n_win=n_win),
        grid=(b, width // LANES, t // big),
        in_specs=[pl.BlockSpec((None, n_win * tq, 2 * tq), lambda bi, h, i: (h, 0, 0)), cur, prev, cur, vprev, vcur],
        out_specs=cur,
        out_shape=jax.ShapeDtypeStruct((b, t, width), BF16),
        compiler_params=_params("parallel", "parallel", "parallel"),
        name="band_attn_seq",
    )(bias_t, q, k, k, vt, vt)


def _lambda_init(layer):
    return 0.8 - 0.6 * math.exp(-0.3 * layer)


def _tile_gain(g, n):
    return jnp.tile(g.astype(F32), n)


def _trunk(x, pos0, mem_k, mem_v, past_ab, past_c, p):
    b, t, d = x.shape
    rows = b * t
    depth = p['ffn1_g'].shape[0]
    seq = past_ab is None
    new_ab, new_c = [], []
    pos_rows = jnp.tile(pos0 + jnp.arange(t), b)
    x = x.reshape(rows, d)
    heads_per_tile = TILE_N // HEAD_DIM
    for l in range(depth):
        x = _ffn(x, p['ffn1_g'][l], p['ffn1_wg'][l], p['ffn1_wu'][l], p['ffn1_wd'][l])
        li = l // 2
        if l % 2 == 0:
            gains = jnp.stack([_tile_gain(p['a_gq'][li], heads_per_tile), _tile_gain(p['a_gk'][li], heads_per_tile)])
            outs = [("rows", TILE_N, F32)] * 4 + [("rows", TILE_N, BF16)] * 5
            v_dests = (("copy", 1, 0, 1.0),)
            if seq:
                outs = outs + [("heads_t", TILE_N // LANES, BF16)]
                v_dests += (("heads_t", 9, 0),)
            tiles = [
                (0, True, (("copy", 4, 0, SOFTMAX_Q_SCALE),)),
                (1, True, (("copy", 0, 0, 1.0), ("copy", 5, 0, 1.0))),
                (None, False, v_dests),
                (None, False, (("copy", 6, 0, STICK_Q_SCALE),)),
                (None, False, (("copy", 2, 0, 1.0), ("copy", 7, 0, 1.0))),
                (None, False, (("copy", 3, 0, 1.0), ("copy", 8, 0, 1.0))),
            ]
            res = _proj(x, p['mix_g'][l], p['ab_w_in'][li], gains, HEAD_DIM, tiles, outs,
                        rope_tables=_rope_tables(pos_rows), name="proj_ab")
            a_k, a_v, b_k, b_v = (r.reshape(b, t, TILE_N) for r in res[:4])
            qa, ka, qb, kb, vb = (r.reshape(b, t, TILE_N) for r in res[4:9])
            new_ab.append((a_k, a_v, b_k, b_v))
            lam_vecs = jnp.stack([p['a_lq1'][li], p['a_lk1'][li], p['a_lq2'][li], p['a_lk2'][li]]).astype(F32)
            lam_init = _lambda_init(l)
            if seq:
                t_blk = min(SEQ_BLOCK, t)
                vt = res[9].reshape(TILE_N // LANES, b, t // t_blk, LANES, t_blk)
                a_out = _diff_attn_seq(lam_vecs, p['a_subln_g'][li], qa, ka, vt, t_blk=t_blk, lam_init=lam_init)
                b_out = _stick_attn_seq(qb, kb, vb, t_blk=min(STICK_BLOCK, t))
            else:
                pka, pva, pkb, pvb = (c.reshape(-1, c.shape[2], TILE_N) for c in past_ab)
                a_out = _diff_attn_hist(lam_vecs, p['a_subln_g'][li], qa, pka, a_k, pva, a_v,
                                        layer_row0=li * b, lam_init=lam_init)
                b_out = _stick_attn_hist(qb, pkb, b_k, pvb, b_v, layer_row0=li * b)
            w_out = p['ab_w_out'][li]
            half = w_out.shape[0] // 2
            mixed, w_parts = [a_out, b_out], [w_out[:half], w_out[half:]]
        else:
            gains = jnp.stack([_tile_gain(p['c_gq'][li], heads_per_tile), _tile_gain(p['c_gk'][li], heads_per_tile)])
            groups = TILE_N // LANES
            outs = [("rows", d, F32)] * 2 + [("rows", d, BF16)] * 2
            if seq:
                outs = outs + [("heads_t", d // LANES, BF16)]
            tiles = ([(0, False, (("copy", 2, c, SOFTMAX_Q_SCALE),)) for c in (0, TILE_N)]
                     + [(1, False, (("copy", 0, c, 1.0), ("copy", 3, c, 1.0))) for c in (0, TILE_N)]
                     + [(None, False, (("copy", 1, c * TILE_N, 1.0),) + ((("heads_t", 4, c * groups),) if seq else ()))
                        for c in range(d // TILE_N)])
            res = _proj(x, p['mix_g'][l], p['c_w_in'][li], gains, HEAD_DIM, tiles, outs, name="proj_c")
            k_f, v_f, q_c, k_c = (r.reshape(b, t, d) for r in res[:4])
            band = C_PAST_CHUNKS * CHUNK
            if seq:
                tq = min(BAND_BLOCK, t)
                nb = band // tq + 1
                bias = _band_bias(p['c_bias'][li], (nb - 1) * tq + np.arange(tq), np.arange(nb * tq))
                bias_t = jnp.transpose(bias.reshape(-1, 2, tq, nb * tq), (0, 3, 1, 2)).reshape(-1, nb * tq, 2 * tq)
                o = _band_attn_seq(bias_t, q_c, k_c, res[4], tq=tq, n_win=nb, n_sub=min(BAND_GROUP, t // tq))
                keep = min(band, t)
                new_c.append((k_f[:, t - keep:], v_f[:, t - keep:]))
            else:
                pk, pv = (c.reshape(-1, c.shape[2], d) for c in past_c)
                pc = pk.shape[1]
                bias = _band_bias(p['c_bias'][li], pos0 + np.arange(t), pos0 - pc + np.arange(pc + t))
                o = _band_attn_hist(bias, q_c, pk, k_f, pv, v_f, layer_row0=li * b)
                assert t <= pc, "the rolling band buffer keeps the newest pc rows"
                new_c.append(tuple(jnp.concatenate([c[li * b:(li + 1) * b, t:], f], axis=1)
                                   for c, f in ((pk, k_f), (pv, v_f))))
            mixed, w_parts = [o], [p['c_w_out'][li]]
        if seq:
            mk, mv, mem_row0 = mem_k[l], mem_v[l], 0
        else:
            mk, mv, mem_row0 = mem_k, mem_v, l * b
        x = _mem_attn(x.reshape(b, t, d), mixed, w_parts, p['mem_g_x'][l], p['mem_wq'][l], p['mem_gq'][l],
                      mk, mv, p['mem_wo'][l], mem_row0=mem_row0).reshape(rows, d)
        x = _ffn(x, p['ffn2_g'][l], p['ffn2_wg'][l], p['ffn2_wu'][l], p['ffn2_wd'][l])
    return x.reshape(b, t, d), new_ab, new_c


def kernel(x_prompt, x_sample, cache_a_k, cache_a_v, cache_b_k, cache_b_v, cache_c_k, cache_c_v, cache_mem_k, cache_mem_v, mem_prompt, ffn1_g, ffn1_wg, ffn1_wu, ffn1_wd, ffn2_g, ffn2_wg, ffn2_wu, ffn2_wd, mix_g, ab_w_in, ab_w_out, a_gq, a_gk, a_lq1, a_lk1, a_lq2, a_lk2, a_subln_g, c_w_in, c_w_out, c_gq, c_gk, c_bias, mem_g_x, mem_g_m, mem_wq, mem_wk, mem_wv, mem_wo, mem_gq, mem_gk):
    depth = ffn1_g.shape[0]
    p = dict(ffn1_g=ffn1_g, ffn2_g=ffn2_g, mix_g=mix_g, a_gq=a_gq, a_gk=a_gk, a_lq1=a_lq1, a_lk1=a_lk1,
             a_lq2=a_lq2, a_lk2=a_lk2, a_subln_g=a_subln_g, c_gq=c_gq, c_gk=c_gk, c_bias=c_bias,
             mem_g_x=mem_g_x, mem_gq=mem_gq)
    for name, w in (('ffn1_wg', ffn1_wg), ('ffn1_wu', ffn1_wu), ('ffn1_wd', ffn1_wd), ('ffn2_wg', ffn2_wg),
                    ('ffn2_wu', ffn2_wu), ('ffn2_wd', ffn2_wd), ('ab_w_in', ab_w_in), ('ab_w_out', ab_w_out),
                    ('c_w_in', c_w_in), ('c_w_out', c_w_out), ('mem_wq', mem_wq), ('mem_wo', mem_wo)):
        p[name] = w.astype(BF16)

    bp, tp, d = x_prompt.shape
    bs, ts, _ = x_sample.shape
    m_len = mem_prompt.shape[1]
    past_len = cache_a_k.shape[2]

    mem_k_list, mem_v_list = [], []
    n_half = d // TILE_N
    for l in range(depth):
        w_kv = jnp.concatenate([mem_wk[l], mem_wv[l]], axis=1).astype(BF16)
        gains = _tile_gain(mem_gk[l], TILE_N // MEM_HEAD_DIM)[None]
        tiles = ([(0, False, (("copy", 0, c * TILE_N, 1.0),)) for c in range(n_half)]
                 + [(None, False, (("copy", 1, c * TILE_N, 1.0),)) for c in range(n_half)])
        mk, mv = _proj(mem_prompt.reshape(bp * m_len, d), mem_g_m[l], w_kv, gains, MEM_HEAD_DIM, tiles,
                       [("rows", d, F32)] * 2, name="proj_mem_kv")
        mem_k_list.append(mk.reshape(bp, m_len, d))
        mem_v_list.append(mv.reshape(bp, m_len, d))

    y_prompt, ab_p, c_p = _trunk(x_prompt, 0, mem_k_list, mem_v_list, None, None, p)
    cache_mem_k2 = cache_mem_k.reshape(depth * bs, m_len, d)
    cache_mem_v2 = cache_mem_v.reshape(depth * bs, m_len, d)
    y_sample, ab_s, c_s = _trunk(x_sample, past_len, cache_mem_k2, cache_mem_v2,
                                 (cache_a_k, cache_a_v, cache_b_k, cache_b_v), (cache_c_k, cache_c_v), p)

    a_heads = cache_a_v.shape[3]
    b_heads = cache_b_k.shape[3]
    c_heads = cache_c_k.shape[3]

    def stack(rows, idx, heads, width):
        return jnp.stack([r[idx].reshape(r[idx].shape[0], r[idx].shape[1], heads, width) for r in rows])

    outs = [y_prompt, y_sample]
    outs += [stack(ab_p, 0, 2 * a_heads, HEAD_DIM), stack(ab_p, 1, a_heads, 2 * HEAD_DIM),
             stack(ab_p, 2, b_heads, HEAD_DIM), stack(ab_p, 3, b_heads, HEAD_DIM),
             stack(c_p, 0, c_heads, HEAD_DIM), stack(c_p, 1, c_heads, HEAD_DIM)]
    outs += [jnp.stack(mem_k_list).reshape(depth, bp, m_len, MEM_HEADS, MEM_HEAD_DIM),
             jnp.stack(mem_v_list).reshape(depth, bp, m_len, MEM_HEADS, MEM_HEAD_DIM)]
    outs += [stack(ab_s, 0, 2 * a_heads, HEAD_DIM), stack(ab_s, 1, a_heads, 2 * HEAD_DIM),
             stack(ab_s, 2, b_heads, HEAD_DIM), stack(ab_s, 3, b_heads, HEAD_DIM),
             stack(c_s, 0, c_heads, HEAD_DIM), stack(c_s, 1, c_heads, HEAD_DIM)]
    return tuple(outs)
```

```python
import functools
import math

import jax
import jax.numpy as jnp
import numpy as np
from jax import lax
from jax.experimental import pallas as pl
from jax.experimental.pallas import tpu as pltpu

D_MODEL = 1024
CHUNK = 64
HEAD_DIM = 64
ROT_DIM = HEAD_DIM // 4
ROPE_THETA = 500000.0
C_PAST_CHUNKS = 8
REL_CLIP = 128
MEM_HEADS = 4
MEM_HEAD_DIM = D_MODEL // MEM_HEADS
RMS_EPS = 1e-6
NEG = -1e30
LOG2E = 1.4426950408889634
LANES = 128
TILE_N = 512
SEQ_BLOCK = 512
STICK_BLOCK = 256
STICK_PAIRS = 2
BAND_BLOCK = 256
BAND_GROUP = 8
FFN_TILE = 256
SUM_ROWS = 16
EXP_UNDERFLOW = -104.0
VMEM_LIMIT = 56 * 1024 * 1024
SOFTMAX_Q_SCALE = HEAD_DIM ** -0.5 * LOG2E
STICK_Q_SCALE = HEAD_DIM ** -0.5

BF16 = jnp.bfloat16
F32 = jnp.float32


def _dot(a, b):
    return jnp.dot(a, b, preferred_element_type=F32)


def _dot_nt(a, b):
    return lax.dot_general(a, b, (((1,), (1,)), ((), ())), preferred_element_type=F32)


def _rms(x, g):
    return x * lax.rsqrt(jnp.mean(x * x, axis=-1, keepdims=True) + RMS_EPS) * g


def _split_bf16(x):
    hi = x.astype(BF16)
    lo = (x - hi.astype(F32)).astype(BF16)
    return hi, lo


def _params(*sem):
    return pltpu.CompilerParams(dimension_semantics=sem, vmem_limit_bytes=VMEM_LIMIT)


def _lane_masks(shape):
    lane = lax.broadcasted_iota(jnp.int32, shape, 1)
    return lane < HEAD_DIM


def _head_pair_queries(q):
    first = _lane_masks(q.shape)
    zero = jnp.zeros_like(q)
    return jnp.where(first, q, zero), jnp.where(first, zero, q)


def _lambda(lam_ref, lam_init):
    lv = lam_ref[...]
    return (jnp.exp(jnp.sum(lv[0:1] * lv[1:2], axis=-1, keepdims=True))
            - jnp.exp(jnp.sum(lv[2:3] * lv[3:4], axis=-1, keepdims=True)) + lam_init)


def _ffn_kernel(x_ref, g_ref, wg_ref, wu_ref, wd_ref, o_ref):
    x = x_ref[...]
    h = _rms(x, g_ref[...]).astype(BF16)
    acc = None
    for f in range(wg_ref.shape[1] // FFN_TILE):
        sl = slice(f * FFN_TILE, (f + 1) * FFN_TILE)
        a = _dot(h, wg_ref[:, sl])
        u = _dot(h, wu_ref[:, sl])
        act = a * (1.0 / (1.0 + jnp.exp(-a))) * u
        part = _dot(act.astype(BF16), wd_ref[sl, :])
        acc = part if acc is None else acc + part
    o_ref[...] = x + 0.5 * acc


def _ffn(x, g, wg, wu, wd):
    rows, d = x.shape
    ff = wg.shape[1]
    tm = min(512, rows)
    once = pl.Buffered(1)
    return pl.pallas_call(
        _ffn_kernel,
        grid=(rows // tm,),
        in_specs=[
            pl.BlockSpec((tm, d), lambda i: (i, 0)),
            pl.BlockSpec((1, d), lambda i: (0, 0)),
            pl.BlockSpec((d, ff), lambda i: (0, 0), pipeline_mode=once),
            pl.BlockSpec((d, ff), lambda i: (0, 0), pipeline_mode=once),
            pl.BlockSpec((ff, d), lambda i: (0, 0), pipeline_mode=once),
        ],
        out_specs=pl.BlockSpec((tm, d), lambda i: (i, 0)),
        out_shape=jax.ShapeDtypeStruct((rows, d), F32),
        compiler_params=_params("parallel"),
        name="ffn",
    )(x, g.reshape(1, d), wg, wu, wd)


def _proj_kernel(*refs, tiles, rope):
    x_ref, g_ref, w_ref, gain_ref, seg_ref = refs[:5]
    n_in = 8 if rope else 5
    out_refs = refs[n_in:]
    h = _rms(x_ref[...], g_ref[...]).astype(BF16)
    for t, (norm, use_rope, dests) in enumerate(tiles):
        y = _dot(h, w_ref[:, t * TILE_N:(t + 1) * TILE_N])
        if norm is not None:
            ms = _dot((y * y).astype(BF16), seg_ref[...])
            y = y * lax.rsqrt(ms + RMS_EPS) * gain_ref[norm:norm + 1, :]
        if use_rope:
            cos, sa, sb = refs[5][...], refs[6][...], refs[7][...]
            half = ROT_DIM // 2
            blocks = []
            for c in range(TILE_N // LANES):
                yb = y[:, c * LANES:(c + 1) * LANES]
                blocks.append(yb * cos + pltpu.roll(yb, LANES - half, 1) * sa + pltpu.roll(yb, half, 1) * sb)
            y = jnp.concatenate(blocks, axis=-1)
        for dest in dests:
            ref = out_refs[dest[1]]
            if dest[0] == "copy":
                _, _, col, scale = dest
                ref[:, col:col + TILE_N] = (y if scale == 1.0 else y * scale).astype(ref.dtype)
            else:
                groups = TILE_N // LANES
                ref[dest[2]:dest[2] + groups] = y.T.reshape((groups,) + ref.shape[1:]).astype(ref.dtype)


def _seg_matrix(seg):
    idx = np.arange(TILE_N) // seg
    return jnp.asarray((idx[:, None] == idx[None, :]).astype(np.float32) / seg, dtype=BF16)


def _proj(x, g, w, gains, seg, tiles, outs, rope_tables=None, name="proj"):
    rows, d = x.shape
    tm = min(SEQ_BLOCK, rows)
    rope = rope_tables is not None
    in_specs = [
        pl.BlockSpec((tm, d), lambda i: (i, 0)),
        pl.BlockSpec((1, d), lambda i: (0, 0)),
        pl.BlockSpec(w.shape, lambda i: (0, 0)),
        pl.BlockSpec(gains.shape, lambda i: (0, 0)),
        pl.BlockSpec((TILE_N, TILE_N), lambda i: (0, 0)),
    ]
    args = [x, g.reshape(1, d), w, gains, _seg_matrix(seg)]
    if rope:
        in_specs += [pl.BlockSpec((tm, LANES), lambda i: (i, 0))] * 3
        args += list(rope_tables)
    out_specs, out_shape = [], []
    for o in outs:
        if o[0] == "rows":
            _, width, dtype = o
            out_specs.append(pl.BlockSpec((tm, width), lambda i: (i, 0)))
            out_shape.append(jax.ShapeDtypeStruct((rows, width), dtype))
        else:
            _, groups, dtype = o
            out_specs.append(pl.BlockSpec((groups, None, LANES, tm), lambda i: (0, i, 0, 0)))
            out_shape.append(jax.ShapeDtypeStruct((groups, rows // tm, LANES, tm), dtype))
    return pl.pallas_call(
        functools.partial(_proj_kernel, tiles=tuple(tiles), rope=rope),
        grid=(rows // tm,),
        in_specs=in_specs,
        out_specs=out_specs,
        out_shape=out_shape,
        compiler_params=_params("parallel"),
        name=name,
    )(*args)


def _rope_tables(pos):
    half = ROT_DIM // 2
    rows = pos.shape[0]
    inv_freq = ROPE_THETA ** (-2.0 * jnp.arange(half, dtype=F32) / ROT_DIM)
    ang = pos.astype(F32)[:, None] * inv_freq[None, :]
    c, s = jnp.cos(ang), jnp.sin(ang)
    rest = HEAD_DIM - ROT_DIM
    cos = jnp.concatenate([c, c, jnp.ones((rows, rest), F32)], axis=-1)
    sa = jnp.concatenate([-s, jnp.zeros((rows, HEAD_DIM - half), F32)], axis=-1)
    sb = jnp.concatenate([jnp.zeros((rows, half), F32), s, jnp.zeros((rows, rest), F32)], axis=-1)
    rep = LANES // HEAD_DIM
    return tuple(jnp.tile(t, (1, rep)) for t in (cos, sa, sb))


def _mem_kernel(*refs, n_parts):
    x_ref = refs[0]
    part_refs = refs[1:1 + n_parts]
    w_part_refs = refs[1 + n_parts:1 + 2 * n_parts]
    g_ref, wq_ref, gq_ref, k_ref, v_ref, wo_ref, o_ref = refs[1 + 2 * n_parts:]
    x = x_ref[...]
    for a_ref, w_ref in zip(part_refs, w_part_refs):
        x = x + _dot(a_ref[...], w_ref[...])
    h = _rms(x, g_ref[...]).astype(BF16)
    q = _dot(h, wq_ref[...])
    gq = gq_ref[...]
    scale = MEM_HEAD_DIM ** -0.5
    outs = []
    for hd in range(MEM_HEADS):
        sl = slice(hd * MEM_HEAD_DIM, (hd + 1) * MEM_HEAD_DIM)
        qh = _rms(q[:, sl], gq) * scale
        s = _dot_nt(qh.astype(BF16), k_ref[:, sl].astype(BF16))
        m = jnp.max(s, axis=-1, keepdims=True)
        p = jnp.exp(s - m)
        l = jnp.sum(p, axis=-1, keepdims=True)
        oh = _dot(p.astype(BF16), v_ref[:, sl].astype(BF16)) / l
        outs.append(oh.astype(BF16))
    o = jnp.concatenate(outs, axis=-1)
    o_ref[...] = x + _dot(o, wo_ref[...])


def _mem_attn(x, parts, w_parts, g, wq, gq, mk, mv, wo, *, mem_row0=0):
    b, t, d = x.shape
    m = mk.shape[1]
    tm = min(512, t)
    row_block = lambda width: pl.BlockSpec((None, tm, width), lambda bi, i: (bi, i, 0))
    mem_block = pl.BlockSpec((None, m, d), lambda bi, i: (mem_row0 + bi, 0, 0))
    return pl.pallas_call(
        functools.partial(_mem_kernel, n_parts=len(parts)),
        grid=(b, t // tm),
        in_specs=[row_block(d)] + [row_block(a.shape[2]) for a in parts]
        + [pl.BlockSpec(w.shape, lambda bi, i: (0, 0)) for w in w_parts] + [
            pl.BlockSpec((1, d), lambda bi, i: (0, 0)),
            pl.BlockSpec((d, d), lambda bi, i: (0, 0)),
            pl.BlockSpec((1, MEM_HEAD_DIM), lambda bi, i: (0, 0)),
            mem_block, mem_block,
            pl.BlockSpec((d, d), lambda bi, i: (0, 0)),
        ],
        out_specs=row_block(d),
        out_shape=jax.ShapeDtypeStruct((b, t, d), F32),
        compiler_params=_params("parallel", "parallel"),
        name="mem_attn",
    )(x, *parts, *w_parts, g.reshape(1, d), wq, gq.reshape(1, MEM_HEAD_DIM), mk, mv, wo)


def _history_rows(past_ref, new_ref, lanes):
    return jnp.concatenate([past_ref[:, lanes].astype(BF16), new_ref[:, lanes].astype(BF16)], axis=0)


def _lane_blocks(width):
    return [slice(p * LANES, (p + 1) * LANES) for p in range(width // LANES)]


def _hist_specs(tq, tp, width, layer_row0):
    new = pl.BlockSpec((None, tq, width), lambda bi: (bi, 0, 0))
    past = pl.BlockSpec((None, tp, width), lambda bi: (layer_row0 + bi, 0, 0))
    return [new, past, new, past, new]


def _diff_hist_kernel(lam_ref, g_ref, q_ref, kp_ref, kn_ref, vp_ref, vn_ref, o_ref, *, lam_init):
    tq, tp = q_ref.shape[0], kp_ref.shape[0]
    vis = (lax.broadcasted_iota(jnp.int32, (tq, tp + tq), 1) // CHUNK
           <= (tp + lax.broadcasted_iota(jnp.int32, (tq, tp + tq), 0)) // CHUNK)
    lam = _lambda(lam_ref, lam_init)
    for lanes in _lane_blocks(q_ref.shape[1]):
        kb = _history_rows(kp_ref, kn_ref, lanes)
        vb = _history_rows(vp_ref, vn_ref, lanes)
        outs = []
        for qm in _head_pair_queries(q_ref[:, lanes]):
            s = jnp.where(vis, _dot_nt(qm, kb), NEG)
            p = jnp.exp2(s - jnp.max(s, axis=-1, keepdims=True))
            outs.append(_dot(p.astype(BF16), vb) / jnp.sum(p, axis=-1, keepdims=True))
        o = outs[0] - lam * outs[1]
        o_ref[:, lanes] = (_rms(o, g_ref[...]) * (1.0 - lam_init)).astype(o_ref.dtype)


def _diff_attn_hist(lam_vecs, subln_g, q, k_past, k_new, v_past, v_new, *, layer_row0, lam_init):
    b, tq, width = q.shape
    tp = k_past.shape[1]
    return pl.pallas_call(
        functools.partial(_diff_hist_kernel, lam_init=lam_init),
        grid=(b,),
        in_specs=[
            pl.BlockSpec((4, HEAD_DIM), lambda bi: (0, 0)),
            pl.BlockSpec((1, LANES), lambda bi: (0, 0)),
        ] + _hist_specs(tq, tp, width, layer_row0),
        out_specs=pl.BlockSpec((None, tq, width), lambda bi: (bi, 0, 0)),
        out_shape=jax.ShapeDtypeStruct((b, tq, width), BF16),
        compiler_params=_params("parallel"),
        name="diff_attn_hist",
    )(lam_vecs, subln_g.reshape(1, LANES), q, k_past, k_new, v_past, v_new)


def _diff_seq_kernel(lam_ref, g_ref, q_ref, k_ref, vt_ref, o_ref, m_ref, acc_ref, sa_ref, sb_ref,
                     *, t_blk, lam_init):
    i = pl.program_id(2)
    q2 = jnp.concatenate(_head_pair_queries(q_ref[...]), axis=0)
    q2t = q2.astype(F32).T.astype(BF16)
    m_ref[...] = jnp.full_like(m_ref, NEG)
    acc_ref[...] = jnp.zeros_like(acc_ref)

    def scores(j, dst_ref):
        dst_ref[...] = _dot(k_ref[j], q2t)

    def update(src_ref, j, masked):
        st = src_ref[...]
        if masked:
            vis = (lax.broadcasted_iota(jnp.int32, (t_blk, 2 * t_blk), 0) // CHUNK
                   <= (lax.broadcasted_iota(jnp.int32, (t_blk, 2 * t_blk), 1) % t_blk) // CHUNK)
            st = jnp.where(vis, st, NEG)
        m_old = m_ref[...]
        m_new = jnp.maximum(m_old, jnp.max(st, axis=0, keepdims=True))
        alpha = jnp.exp2(m_old - m_new)
        pt = jnp.exp2(st - m_new)
        vext = jnp.concatenate([vt_ref[j], jnp.ones((SUM_ROWS, t_blk), BF16)], axis=0)
        acc_ref[...] = alpha * acc_ref[...] + _dot(vext, pt.astype(BF16))
        m_ref[...] = m_new

    scores(0, sa_ref)

    def body(jj, carry):
        j = 2 * jj
        scores(j + 1, sb_ref)
        update(sa_ref, j, False)
        scores(j + 2, sa_ref)
        update(sb_ref, j + 1, False)
        return carry

    lax.fori_loop(0, i // 2, body, 0)

    @pl.when(i % 2 == 1)
    def _():
        scores(i, sb_ref)
        update(sa_ref, i - 1, False)
        update(sb_ref, i, True)

    @pl.when(i % 2 == 0)
    def _():
        update(sa_ref, i, True)

    o12 = acc_ref[0:LANES, :] / acc_ref[LANES:LANES + 1, :]
    ot = o12[:, :t_blk] - _lambda(lam_ref, lam_init) * o12[:, t_blk:]
    ot = ot * lax.rsqrt(jnp.mean(ot * ot, axis=0, keepdims=True) + RMS_EPS) * g_ref[...] * (1.0 - lam_init)
    o_ref[...] = ot.T.astype(o_ref.dtype)


def _diff_attn_seq(lam_vecs, subln_g, q, k, vt, *, t_blk, lam_init):
    b, t, width = q.shape
    n_heads = width // LANES
    n_blk = t // t_blk
    return pl.pallas_call(
        functools.partial(_diff_seq_kernel, t_blk=t_blk, lam_init=lam_init),
        grid=(b, n_heads, n_blk),
        in_specs=[
            pl.BlockSpec((4, HEAD_DIM), lambda bi, h, i: (0, 0)),
            pl.BlockSpec((LANES, 1), lambda bi, h, i: (0, 0)),
            pl.BlockSpec((None, t_blk, LANES), lambda bi, h, i: (bi, i, h)),
            pl.BlockSpec((None, n_blk, t_blk, LANES), lambda bi, h, i: (bi, 0, 0, h)),
            pl.BlockSpec((None, None, n_blk, LANES, t_blk), lambda bi, h, i: (h, bi, 0, 0, 0)),
        ],
        out_specs=pl.BlockSpec((None, t_blk, LANES), lambda bi, h, i: (bi, i, h)),
        out_shape=jax.ShapeDtypeStruct((b, t, width), BF16),
        scratch_shapes=[pltpu.VMEM((1, 2 * t_blk), F32), pltpu.VMEM((LANES + SUM_ROWS, 2 * t_blk), F32),
                        pltpu.VMEM((t_blk, 2 * t_blk), F32), pltpu.VMEM((t_blk, 2 * t_blk), F32)],
        compiler_params=_params("parallel", "parallel", "arbitrary"),
        name="diff_attn_seq",
    )(lam_vecs, subln_g.reshape(LANES, 1), q, k.reshape(b, n_blk, t_blk, width), vt)


def _upper(n):
    return jnp.asarray((np.arange(n)[:, None] > np.arange(n)[None, :]).astype(np.float32), dtype=BF16)


def _stick_stages(zs, vbs, u, causal, r_olds):
    sps = [jnp.maximum(z, 0.0) + jnp.log(1.0 + jnp.exp(-jnp.abs(z))) for z in zs]
    log_1ms = [-sp if causal is None else jnp.where(causal, -sp, 0.0) for sp in sps]
    splits = [_split_bf16(log_1m) for log_1m in log_1ms]
    afters = [_dot(hi, u) + _dot(lo, u) for hi, lo in splits]
    ws = [jnp.exp((z - sp) + after + r_old) for z, sp, after, r_old in zip(zs, sps, afters, r_olds)]
    if causal is not None:
        ws = [jnp.where(causal, w, 0.0) for w in ws]
    wvs = [_dot(w.astype(BF16), vb) for w, vb in zip(ws, vbs)]
    return wvs, [after[:, 0:1] + log_1m[:, 0:1] for after, log_1m in zip(afters, log_1ms)]


def _stick_hist_kernel(u_ref, q_ref, kp_ref, kn_ref, vp_ref, vn_ref, o_ref):
    tq, tp = q_ref.shape[0], kp_ref.shape[0]
    causal = (lax.broadcasted_iota(jnp.int32, (tq, tp + tq), 1)
              < tp + lax.broadcasted_iota(jnp.int32, (tq, tp + tq), 0))
    for lanes in _lane_blocks(q_ref.shape[1]):
        kb = _history_rows(kp_ref, kn_ref, lanes)
        vb = _history_rows(vp_ref, vn_ref, lanes)
        zs = [_dot_nt(qm, kb) for qm in _head_pair_queries(q_ref[:, lanes])]
        outs, _ = _stick_stages(zs, [vb, vb], u_ref[...], causal, [0.0, 0.0])
        o_ref[:, lanes] = jnp.where(_lane_masks((tq, LANES)), outs[0], outs[1]).astype(o_ref.dtype)


def _stick_attn_hist(q, k_past, k_new, v_past, v_new, *, layer_row0):
    b, tq, width = q.shape
    tp = k_past.shape[1]
    return pl.pallas_call(
        _stick_hist_kernel,
        grid=(b,),
        in_specs=[pl.BlockSpec((tp + tq, tp + tq), lambda bi: (0, 0))] + _hist_specs(tq, tp, width, layer_row0),
        out_specs=pl.BlockSpec((None, tq, width), lambda bi: (bi, 0, 0)),
        out_shape=jax.ShapeDtypeStruct((b, tq, width), BF16),
        compiler_params=_params("parallel"),
        name="stick_attn_hist",
    )(_upper(tp + tq), q, k_past, k_new, v_past, v_new)


def _stick_seq_kernel(u_ref, q_ref, k_ref, v_ref, o_ref, acc_ref, r_ref, z_ref, *, t_blk, n_pairs):
    i = pl.program_id(2)
    lanes = [slice(p * LANES, (p + 1) * LANES) for p in range(n_pairs)]
    q2s = [jnp.concatenate(_head_pair_queries(q_ref[:, sl]), axis=0) for sl in lanes]
    acc_ref[...] = jnp.zeros_like(acc_ref)
    r_ref[...] = jnp.zeros_like(r_ref)
    u = u_ref[...]

    def block(j, masked):
        causal = None
        if masked:
            causal = (lax.broadcasted_iota(jnp.int32, (2 * t_blk, t_blk), 1)
                      < lax.broadcasted_iota(jnp.int32, (2 * t_blk, t_blk), 0) % t_blk)
        zs = [z_ref[p] for p in range(n_pairs)]
        k_next = k_ref[jnp.maximum(j - 1, 0)]
        for p in range(n_pairs):
            z_ref[p] = _dot_nt(q2s[p], k_next[:, lanes[p]])
        vb = v_ref[j]
        r_olds = [r_ref[p] for p in range(n_pairs)]
        wvs, r_blks = _stick_stages(zs, [vb[:, sl] for sl in lanes], u, causal, r_olds)
        r_max = None
        for p in range(n_pairs):
            acc_ref[p] += wvs[p]
            r_new = r_olds[p] + r_blks[p]
            r_ref[p] = r_new
            mx = jnp.max(r_new)
            r_max = mx if r_max is None else jnp.maximum(r_max, mx)
        return r_max

    k_diag = k_ref[i]
    for p in range(n_pairs):
        z_ref[p] = _dot_nt(q2s[p], k_diag[:, lanes[p]])
    block(i, True)

    def cond(carry):
        j, done = carry
        return jnp.logical_and(j >= 0, done == 0)

    def body(carry):
        j, _ = carry
        r_max = block(j, False)
        return j - 1, (r_max < EXP_UNDERFLOW).astype(jnp.int32)

    lax.while_loop(cond, body, (i - 1, jnp.int32(0)))
    for p in range(n_pairs):
        acc = acc_ref[p]
        o_ref[:, lanes[p]] = jnp.where(_lane_masks((t_blk, LANES)), acc[:t_blk], acc[t_blk:]).astype(o_ref.dtype)


def _stick_attn_seq(q, k, v, *, t_blk):
    b, t, width = q.shape
    n_blk = t // t_blk
    n_pairs = STICK_PAIRS
    w = n_pairs * LANES
    kv_spec = pl.BlockSpec((None, n_blk, t_blk, w), lambda bi, h, i: (bi, 0, 0, h))
    return pl.pallas_call(
        functools.partial(_stick_seq_kernel, t_blk=t_blk, n_pairs=n_pairs),
        grid=(b, width // w, n_blk),
        in_specs=[
            pl.BlockSpec((t_blk, t_blk), lambda bi, h, i: (0, 0)),
            pl.BlockSpec((None, t_blk, w), lambda bi, h, i: (bi, i, h)),
            kv_spec, kv_spec,
        ],
        out_specs=pl.BlockSpec((None, t_blk, w), lambda bi, h, i: (bi, i, h)),
        out_shape=jax.ShapeDtypeStruct((b, t, width), BF16),
        scratch_shapes=[pltpu.VMEM((n_pairs, 2 * t_blk, LANES), F32), pltpu.VMEM((n_pairs, 2 * t_blk, 1), F32),
                        pltpu.VMEM((n_pairs, 2 * t_blk, t_blk), F32)],
        compiler_params=_params("parallel", "parallel", "arbitrary"),
        name="stick_attn_seq",
    )(_upper(t_blk), q, k.reshape(b, n_blk, t_blk, width), v.reshape(b, n_blk, t_blk, width))


def _band_hist_kernel(bias_ref, q_ref, kp_ref, kn_ref, vp_ref, vn_ref, o_ref, k_roll_ref, v_roll_ref):
    tq, tp = q_ref.shape[0], kp_ref.shape[0]
    for past_ref, new_ref, roll_ref in ((kp_ref, kn_ref, k_roll_ref), (vp_ref, vn_ref, v_roll_ref)):
        roll_ref[0:tp - tq, :] = past_ref[tq:tp, :]
        roll_ref[tp - tq:tp, :] = new_ref[...]
    for pair, lanes in enumerate(_lane_blocks(q_ref.shape[1])):
        kb = _history_rows(kp_ref, kn_ref, lanes)
        vb = _history_rows(vp_ref, vn_ref, lanes)
        outs = []
        for hd, qm in enumerate(_head_pair_queries(q_ref[:, lanes])):
            s = _dot_nt(qm, kb) + bias_ref[2 * pair + hd]
            p = jnp.exp2(s - jnp.max(s, axis=-1, keepdims=True))
            outs.append(_dot(p.astype(BF16), vb) / jnp.sum(p, axis=-1, keepdims=True))
        o_ref[:, lanes] = jnp.where(_lane_masks((tq, LANES)), outs[0], outs[1]).astype(o_ref.dtype)


def _band_bias(bias_table, qpos, kpos):
    tq, tk = len(qpos), len(kpos)
    n = tq + tk
    shift = np.arange(n)
    c_minus_r = np.where(shift < tk, shift, shift - n)
    rel = np.clip(int(qpos[0] - kpos[0]) - c_minus_r, -REL_CLIP, REL_CLIP) + REL_CLIP
    vec = bias_table.astype(F32)[:, rel] * LOG2E
    toeplitz = jnp.tile(vec, (1, tq))[:, :tq * (n - 1)].reshape(-1, tq, n - 1)[:, :, :tk]
    qc, kc = qpos[:, None] // CHUNK, kpos[None, :] // CHUNK
    vis = (kpos[None, :] >= 0) & (kc <= qc) & (qc - kc <= C_PAST_CHUNKS)
    return jnp.where(jnp.asarray(vis)[None], toeplitz, NEG)


def _band_attn_hist(bias, q, k_past, k_new, v_past, v_new, *, layer_row0):
    b, tq, width = q.shape
    tp = k_past.shape[1]
    assert tq <= tp, "the rolling band buffer keeps the newest Tp rows"
    roll_spec = pl.BlockSpec((None, tp, width), lambda bi: (bi, 0, 0))
    return pl.pallas_call(
        _band_hist_kernel,
        grid=(b,),
        in_specs=[pl.BlockSpec(bias.shape, lambda bi: (0, 0, 0))] + _hist_specs(tq, tp, width, layer_row0),
        out_specs=[pl.BlockSpec((None, tq, width), lambda bi: (bi, 0, 0)), roll_spec, roll_spec],
        out_shape=[jax.ShapeDtypeStruct((b, tq, width), BF16),
                   jax.ShapeDtypeStruct((b, tp, width), k_past.dtype), jax.ShapeDtypeStruct((b, tp, width), v_past.dtype)],
        compiler_params=_params("parallel"),
        name="band_attn_hist",
    )(bias, q, k_past, k_new, v_past, v_new)


def _band_seq_kernel(bias_ref, q_ref, kp_ref, kc_ref, vp_ref, vc_ref, o_ref, *, tq, n_sub, n_win):
    big = n_sub * tq
    win = n_win * tq
    half = vp_ref.shape[2]
    prev_ok = pl.program_id(2) > 0
    top = lax.broadcasted_iota(jnp.int32, (LANES, tq), 0) < HEAD_DIM

    def value_piece(ref, off):
        return ref[off // half][:, off % half:off % half + tq]

    def scores(g):
        q2 = jnp.concatenate(_head_pair_queries(q_ref[g * tq:(g + 1) * tq, :]), axis=0)
        q2t = q2.astype(F32).T.astype(BF16)
        start = big - (n_win - 1) * tq + g * tq
        n_prev = max(big - start, 0)
        k_pieces, v_pieces = [], []
        for d in range(n_win):
            off = start + d * tq
            if off < big:
                k_pieces.append(kp_ref[off:off + tq, :])
                v_pieces.append(value_piece(vp_ref, off))
            else:
                k_pieces.append(kc_ref[off - big:off - big + tq, :])
                v_pieces.append(value_piece(vc_ref, off - big))
        st = _dot(jnp.concatenate(k_pieces, axis=0), q2t) + bias_ref[...]
        if n_prev > 0:
            before_start = jnp.logical_and(lax.broadcasted_iota(jnp.int32, (win, 2 * tq), 0) < n_prev,
                                           jnp.logical_not(prev_ok))
            st = jnp.where(before_start, NEG, st)
        return st, jnp.concatenate(v_pieces, axis=1)

    def finish(g, st, vwt):
        pt = jnp.exp2(st - jnp.max(st, axis=0, keepdims=True))
        ot = _dot(vwt, pt.astype(BF16)) / jnp.sum(pt, axis=0, keepdims=True)
        o_ref[g * tq:(g + 1) * tq, :] = jnp.where(top, ot[:, :tq], ot[:, tq:]).T.astype(o_ref.dtype)

    nxt = scores(0)
    for g in range(n_sub):
        cur_scores = nxt
        if g + 1 < n_sub:
            nxt = scores(g + 1)
        finish(g, *cur_scores)


def _band_attn_seq(bias_t, q, k, vt, *, tq, n_win, n_sub):
    b, t, width = q.shape
    big = n_sub * tq
    half = vt.shape[3]
    n_half = big // half
    assert (n_win - 1) * tq <= big, "the window must fit in the previous + current block"
    assert half % tq == 0, "a window piece must not straddle two transposed value blocks"
    cur = pl.BlockSpec((None, big, LANES), lambda bi, h, i: (bi, i, h))
    prev = pl.BlockSpec((None, big, LANES), lambda bi, h, i: (bi, jnp.maximum(i - 1, 0), h))
    per_b = t // big
    vcur = pl.BlockSpec((None, n_half, LANES, half), lambda bi, h, i: (h, bi * per_b + i, 0, 0))
    vprev = pl.BlockSpec((None, n_half, LANES, half), lambda bi, h, i: (h, bi * per_b + jnp.maximum(i - 1, 0), 0, 0))
    return pl.pallas_call(
        functools.partial(_band_seq_kernel, tq=tq, n_sub=n_sub, n_win=n_win),
        grid=(b, width // LANES, t // big),
        in_specs=[pl.BlockSpec((None, n_win * tq, 2 * tq), lambda bi, h, i: (h, 0, 0)), cur, prev, cur, vprev, vcur],
        out_specs=cur,
        out_shape=jax.ShapeDtypeStruct((b, t, width), BF16),
        compiler_params=_params("parallel", "parallel", "parallel"),
        name="band_attn_seq",
    )(bias_t, q, k, k, vt, vt)


def _lambda_init(layer):
    return 0.8 - 0.6 * math.exp(-0.3 * layer)


def _tile_gain(g, n):
    return jnp.tile(g.astype(F32), n)


def _trunk(x, pos0, mem_k, mem_v, past_ab, past_c, p):
    b, t, d = x.shape
    rows = b * t
    depth = p['ffn1_g'].shape[0]
    seq = past_ab is None
    new_ab, new_c = [], []
    pos_rows = jnp.tile(pos0 + jnp.arange(t), b)
    x = x.reshape(rows, d)
    heads_per_tile = TILE_N // HEAD_DIM
    for l in range(depth):
        x = _ffn(x, p['ffn1_g'][l], p['ffn1_wg'][l], p['ffn1_wu'][l], p['ffn1_wd'][l])
        li = l // 2
        if l % 2 == 0:
            gains = jnp.stack([_tile_gain(p['a_gq'][li], heads_per_tile), _tile_gain(p['a_gk'][li], heads_per_tile)])
            outs = [("rows", TILE_N, F32)] * 4 + [("rows", TILE_N, BF16)] * 5
            v_dests = (("copy", 1, 0, 1.0),)
            if seq:
                outs = outs + [("heads_t", TILE_N // LANES, BF16)]
                v_dests += (("heads_t", 9, 0),)
            tiles = [
                (0, True, (("copy", 4, 0, SOFTMAX_Q_SCALE),)),
                (1, True, (("copy", 0, 0, 1.0), ("copy", 5, 0, 1.0))),
                (None, False, v_dests),
                (None, False, (("copy", 6, 0, STICK_Q_SCALE),)),
                (None, False, (("copy", 2, 0, 1.0), ("copy", 7, 0, 1.0))),
                (None, False, (("copy", 3, 0, 1.0), ("copy", 8, 0, 1.0))),
            ]
            res = _proj(x, p['mix_g'][l], p['ab_w_in'][li], gains, HEAD_DIM, tiles, outs,
                        rope_tables=_rope_tables(pos_rows), name="proj_ab")
            a_k, a_v, b_k, b_v = (r.reshape(b, t, TILE_N) for r in res[:4])
            qa, ka, qb, kb, vb = (r.reshape(b, t, TILE_N) for r in res[4:9])
            new_ab.append((a_k, a_v, b_k, b_v))
            lam_vecs = jnp.stack([p['a_lq1'][li], p['a_lk1'][li], p['a_lq2'][li], p['a_lk2'][li]]).astype(F32)
            lam_init = _lambda_init(l)
            if seq:
                t_blk = min(SEQ_BLOCK, t)
                vt = res[9].reshape(TILE_N // LANES, b, t // t_blk, LANES, t_blk)
                a_out = _diff_attn_seq(lam_vecs, p['a_subln_g'][li], qa, ka, vt, t_blk=t_blk, lam_init=lam_init)
                b_out = _stick_attn_seq(qb, kb, vb, t_blk=min(STICK_BLOCK, t))
            else:
                pka, pva, pkb, pvb = (c.reshape(-1, c.shape[2], TILE_N) for c in past_ab)
                a_out = _diff_attn_hist(lam_vecs, p['a_subln_g'][li], qa, pka, a_k, pva, a_v,
                                        layer_row0=li * b, lam_init=lam_init)
                b_out = _stick_attn_hist(qb, pkb, b_k, pvb, b_v, layer_row0=li * b)
            w_out = p['ab_w_out'][li]
            half = w_out.shape[0] // 2
            mixed, w_parts = [a_out, b_out], [w_out[:half], w_out[half:]]
        else:
            gains = jnp.stack([_tile_gain(p['c_gq'][li], heads_per_tile), _tile_gain(p['c_gk'][li], heads_per_tile)])
            groups = TILE_N // LANES
            outs = [("rows", d, F32)] * 2 + [("rows", d, BF16)] * 2
            if seq:
                outs = outs + [("heads_t", d // LANES, BF16)]
            tiles = ([(0, False, (("copy", 2, c, SOFTMAX_Q_SCALE),)) for c in (0, TILE_N)]
                     + [(1, False, (("copy", 0, c, 1.0), ("copy", 3, c, 1.0))) for c in (0, TILE_N)]
                     + [(None, False, (("copy", 1, c * TILE_N, 1.0),) + ((("heads_t", 4, c * groups),) if seq else ()))
                        for c in range(d // TILE_N)])
            res = _proj(x, p['mix_g'][l], p['c_w_in'][li], gains, HEAD_DIM, tiles, outs, name="proj_c")
            k_f, v_f, q_c, k_c = (r.reshape(b, t, d) for r in res[:4])
            band = C_PAST_CHUNKS * CHUNK
            if seq:
                tq = min(BAND_BLOCK, t)
                nb = band // tq + 1
                bias = _band_bias(p['c_bias'][li], (nb - 1) * tq + np.arange(tq), np.arange(nb * tq))
                bias_t = jnp.transpose(bias.reshape(-1, 2, tq, nb * tq), (0, 3, 1, 2)).reshape(-1, nb * tq, 2 * tq)
                o = _band_attn_seq(bias_t, q_c, k_c, res[4], tq=tq, n_win=nb, n_sub=min(BAND_GROUP, t // tq))
                keep = min(band, t)
                new_c.append((k_f[:, t - keep:], v_f[:, t - keep:]))
            else:
                pk, pv = (c.reshape(-1, c.shape[2], d) for c in past_c)
                pc = pk.shape[1]
                bias = _band_bias(p['c_bias'][li], pos0 + np.arange(t), pos0 - pc + np.arange(pc + t))
                o, k_roll, v_roll = _band_attn_hist(bias, q_c, pk, k_f, pv, v_f, layer_row0=li * b)
                new_c.append((k_roll, v_roll))
            mixed, w_parts = [o], [p['c_w_out'][li]]
        if seq:
            mk, mv, mem_row0 = mem_k[l], mem_v[l], 0
        else:
            mk, mv, mem_row0 = mem_k, mem_v, l * b
        x = _mem_attn(x.reshape(b, t, d), mixed, w_parts, p['mem_g_x'][l], p['mem_wq'][l], p['mem_gq'][l],
                      mk, mv, p['mem_wo'][l], mem_row0=mem_row0).reshape(rows, d)
        x = _ffn(x, p['ffn2_g'][l], p['ffn2_wg'][l], p['ffn2_wu'][l], p['ffn2_wd'][l])
    return x.reshape(b, t, d), new_ab, new_c


def kernel(x_prompt, x_sample, cache_a_k, cache_a_v, cache_b_k, cache_b_v, cache_c_k, cache_c_v, cache_mem_k, cache_mem_v, mem_prompt, ffn1_g, ffn1_wg, ffn1_wu, ffn1_wd, ffn2_g, ffn2_wg, ffn2_wu, ffn2_wd, mix_g, ab_w_in, ab_w_out, a_gq, a_gk, a_lq1, a_lk1, a_lq2, a_lk2, a_subln_g, c_w_in, c_w_out, c_gq, c_gk, c_bias, mem_g_x, mem_g_m, mem_wq, mem_wk, mem_wv, mem_wo, mem_gq, mem_gk):
    depth = ffn1_g.shape[0]
    p = dict(ffn1_g=ffn1_g, ffn2_g=ffn2_g, mix_g=mix_g, a_gq=a_gq, a_gk=a_gk, a_lq1=a_lq1, a_lk1=a_lk1,
             a_lq2=a_lq2, a_lk2=a_lk2, a_subln_g=a_subln_g, c_gq=c_gq, c_gk=c_gk, c_bias=c_bias,
             mem_g_x=mem_g_x, mem_gq=mem_gq)
    for name, w in (('ffn1_wg', ffn1_wg), ('ffn1_wu', ffn1_wu), ('ffn1_wd', ffn1_wd), ('ffn2_wg', ffn2_wg),
                    ('ffn2_wu', ffn2_wu), ('ffn2_wd', ffn2_wd), ('ab_w_in', ab_w_in), ('ab_w_out', ab_w_out),
                    ('c_w_in', c_w_in), ('c_w_out', c_w_out), ('mem_wq', mem_wq), ('mem_wo', mem_wo)):
        p[name] = w.astype(BF16)

    bp, tp, d = x_prompt.shape
    bs, ts, _ = x_sample.shape
    m_len = mem_prompt.shape[1]
    past_len = cache_a_k.shape[2]

    mem_k_list, mem_v_list = [], []
    n_half = d // TILE_N
    for l in range(depth):
        w_kv = jnp.concatenate([mem_wk[l], mem_wv[l]], axis=1).astype(BF16)
        gains = _tile_gain(mem_gk[l], TILE_N // MEM_HEAD_DIM)[None]
        tiles = ([(0, False, (("copy", 0, c * TILE_N, 1.0),)) for c in range(n_half)]
                 + [(None, False, (("copy", 1, c * TILE_N, 1.0),)) for c in range(n_half)])
        mk, mv = _proj(mem_prompt.reshape(bp * m_len, d), mem_g_m[l], w_kv, gains, MEM_HEAD_DIM, tiles,
                       [("rows", d, F32)] * 2, name="proj_mem_kv")
        mem_k_list.append(mk.reshape(bp, m_len, d))
        mem_v_list.append(mv.reshape(bp, m_len, d))

    y_prompt, ab_p, c_p = _trunk(x_prompt, 0, mem_k_list, mem_v_list, None, None, p)
    cache_mem_k2 = cache_mem_k.reshape(depth * bs, m_len, d)
    cache_mem_v2 = cache_mem_v.reshape(depth * bs, m_len, d)
    y_sample, ab_s, c_s = _trunk(x_sample, past_len, cache_mem_k2, cache_mem_v2,
                                 (cache_a_k, cache_a_v, cache_b_k, cache_b_v), (cache_c_k, cache_c_v), p)

    a_heads = cache_a_v.shape[3]
    b_heads = cache_b_k.shape[3]
    c_heads = cache_c_k.shape[3]

    def stack(rows, idx, heads, width):
        return jnp.stack([r[idx].reshape(r[idx].shape[0], r[idx].shape[1], heads, width) for r in rows])

    outs = [y_prompt, y_sample]
    outs += [stack(ab_p, 0, 2 * a_heads, HEAD_DIM), stack(ab_p, 1, a_heads, 2 * HEAD_DIM),
             stack(ab_p, 2, b_heads, HEAD_DIM), stack(ab_p, 3, b_heads, HEAD_DIM),
             stack(c_p, 0, c_heads, HEAD_DIM), stack(c_p, 1, c_heads, HEAD_DIM)]
    outs += [jnp.stack(mem_k_list).reshape(depth, bp, m_len, MEM_HEADS, MEM_HEAD_DIM),
             jnp.stack(mem_v_list).reshape(depth, bp, m_len, MEM_HEADS, MEM_HEAD_DIM)]
    outs += [stack(ab_s, 0, 2 * a_heads, HEAD_DIM), stack(ab_s, 1, a_heads, 2 * HEAD_DIM),
             stack(ab_s, 2, b_heads, HEAD_DIM), stack(ab_s, 3, b_heads, HEAD_DIM),
             stack(c_s, 0, c_heads, HEAD_DIM), stack(c_s, 1, c_heads, HEAD_DIM)]
    return tuple(outs)
```

```python
import functools
import math

import jax
import jax.numpy as jnp
import numpy as np
from jax import lax
from jax.experimental import pallas as pl
from jax.experimental.pallas import tpu as pltpu

D_MODEL = 1024
CHUNK = 64
HEAD_DIM = 64
ROT_DIM = HEAD_DIM // 4
ROPE_THETA = 500000.0
C_PAST_CHUNKS = 8
REL_CLIP = 128
MEM_HEADS = 4
MEM_HEAD_DIM = D_MODEL // MEM_HEADS
RMS_EPS = 1e-6
NEG = -1e30
LOG2E = 1.4426950408889634
LANES = 128
TILE_N = 512
SEQ_BLOCK = 512
STICK_BLOCK = 256
STICK_PAIRS = 2
BAND_BLOCK = 256
BAND_GROUP = 8
FFN_TILE = 256
SUM_ROWS = 16
EXP_UNDERFLOW = -104.0
VMEM_LIMIT = 56 * 1024 * 1024
SOFTMAX_Q_SCALE = HEAD_DIM ** -0.5 * LOG2E
STICK_Q_SCALE = HEAD_DIM ** -0.5

BF16 = jnp.bfloat16
F32 = jnp.float32


def _dot(a, b):
    return jnp.dot(a, b, preferred_element_type=F32)


def _dot_nt(a, b):
    return lax.dot_general(a, b, (((1,), (1,)), ((), ())), preferred_element_type=F32)


def _rms(x, g):
    return x * lax.rsqrt(jnp.mean(x * x, axis=-1, keepdims=True) + RMS_EPS) * g


def _split_bf16(x):
    hi = x.astype(BF16)
    lo = (x - hi.astype(F32)).astype(BF16)
    return hi, lo


def _params(*sem):
    return pltpu.CompilerParams(dimension_semantics=sem, vmem_limit_bytes=VMEM_LIMIT)


def _lane_masks(shape):
    lane = lax.broadcasted_iota(jnp.int32, shape, 1)
    return lane < HEAD_DIM


def _head_pair_queries(q):
    first = _lane_masks(q.shape)
    zero = jnp.zeros_like(q)
    return jnp.where(first, q, zero), jnp.where(first, zero, q)


def _lambda(lam_ref, lam_init):
    lv = lam_ref[...]
    return (jnp.exp(jnp.sum(lv[0:1] * lv[1:2], axis=-1, keepdims=True))
            - jnp.exp(jnp.sum(lv[2:3] * lv[3:4], axis=-1, keepdims=True)) + lam_init)


def _ffn_kernel(x_ref, g_ref, wg_ref, wu_ref, wd_ref, o_ref):
    x = x_ref[...]
    h = _rms(x, g_ref[...]).astype(BF16)
    acc = None
    for f in range(wg_ref.shape[1] // FFN_TILE):
        sl = slice(f * FFN_TILE, (f + 1) * FFN_TILE)
        a = _dot(h, wg_ref[:, sl])
        u = _dot(h, wu_ref[:, sl])
        act = a * (1.0 / (1.0 + jnp.exp(-a))) * u
        part = _dot(act.astype(BF16), wd_ref[sl, :])
        acc = part if acc is None else acc + part
    o_ref[...] = x + 0.5 * acc


def _ffn(x, g, wg, wu, wd):
    rows, d = x.shape
    ff = wg.shape[1]
    tm = min(512, rows)
    once = pl.Buffered(1)
    return pl.pallas_call(
        _ffn_kernel,
        grid=(rows // tm,),
        in_specs=[
            pl.BlockSpec((tm, d), lambda i: (i, 0)),
            pl.BlockSpec((1, d), lambda i: (0, 0)),
            pl.BlockSpec((d, ff), lambda i: (0, 0), pipeline_mode=once),
            pl.BlockSpec((d, ff), lambda i: (0, 0), pipeline_mode=once),
            pl.BlockSpec((ff, d), lambda i: (0, 0), pipeline_mode=once),
        ],
        out_specs=pl.BlockSpec((tm, d), lambda i: (i, 0)),
        out_shape=jax.ShapeDtypeStruct((rows, d), F32),
        compiler_params=_params("parallel"),
        name="ffn",
    )(x, g.reshape(1, d), wg, wu, wd)


def _proj_kernel(*refs, tiles, rope):
    x_ref, g_ref, w_ref, gain_ref, seg_ref = refs[:5]
    n_in = 8 if rope else 5
    out_refs = refs[n_in:]
    h = _rms(x_ref[...], g_ref[...]).astype(BF16)
    for t, (norm, use_rope, dests) in enumerate(tiles):
        y = _dot(h, w_ref[:, t * TILE_N:(t + 1) * TILE_N])
        if norm is not None:
            ms = _dot((y * y).astype(BF16), seg_ref[...])
            y = y * lax.rsqrt(ms + RMS_EPS) * gain_ref[norm:norm + 1, :]
        if use_rope:
            cos, sa, sb = refs[5][...], refs[6][...], refs[7][...]
            half = ROT_DIM // 2
            blocks = []
            for c in range(TILE_N // LANES):
                yb = y[:, c * LANES:(c + 1) * LANES]
                blocks.append(yb * cos + pltpu.roll(yb, LANES - half, 1) * sa + pltpu.roll(yb, half, 1) * sb)
            y = jnp.concatenate(blocks, axis=-1)
        for dest in dests:
            ref = out_refs[dest[1]]
            if dest[0] == "copy":
                _, _, col, scale = dest
                ref[:, col:col + TILE_N] = (y if scale == 1.0 else y * scale).astype(ref.dtype)
            elif dest[0] == "split":
                ref[...] = y.reshape(ref.shape).astype(ref.dtype)
            else:
                groups = TILE_N // LANES
                ref[dest[2]:dest[2] + groups] = y.T.reshape((groups,) + ref.shape[1:]).astype(ref.dtype)


def _seg_matrix(seg):
    idx = np.arange(TILE_N) // seg
    return jnp.asarray((idx[:, None] == idx[None, :]).astype(np.float32) / seg, dtype=BF16)


def _proj(x, g, w, gains, seg, tiles, outs, rope_tables=None, name="proj"):
    rows, d = x.shape
    tm = min(SEQ_BLOCK, rows)
    rope = rope_tables is not None
    in_specs = [
        pl.BlockSpec((tm, d), lambda i: (i, 0)),
        pl.BlockSpec((1, d), lambda i: (0, 0)),
        pl.BlockSpec(w.shape, lambda i: (0, 0)),
        pl.BlockSpec(gains.shape, lambda i: (0, 0)),
        pl.BlockSpec((TILE_N, TILE_N), lambda i: (0, 0)),
    ]
    args = [x, g.reshape(1, d), w, gains, _seg_matrix(seg)]
    if rope:
        in_specs += [pl.BlockSpec((tm, LANES), lambda i: (i, 0))] * 3
        args += list(rope_tables)
    out_specs, out_shape = [], []
    for o in outs:
        if o[0] == "rows":
            _, width, dtype = o
            out_specs.append(pl.BlockSpec((tm, width), lambda i: (i, 0)))
            out_shape.append(jax.ShapeDtypeStruct((rows, width), dtype))
        elif o[0] == "heads":
            _, heads, head_width, dtype = o
            out_specs.append(pl.BlockSpec((tm, heads, head_width), lambda i: (i, 0, 0)))
            out_shape.append(jax.ShapeDtypeStruct((rows, heads, head_width), dtype))
        else:
            _, groups, dtype = o
            out_specs.append(pl.BlockSpec((groups, None, LANES, tm), lambda i: (0, i, 0, 0)))
            out_shape.append(jax.ShapeDtypeStruct((groups, rows // tm, LANES, tm), dtype))
    return pl.pallas_call(
        functools.partial(_proj_kernel, tiles=tuple(tiles), rope=rope),
        grid=(rows // tm,),
        in_specs=in_specs,
        out_specs=out_specs,
        out_shape=out_shape,
        compiler_params=_params("parallel"),
        name=name,
    )(*args)


def _rope_tables(pos):
    half = ROT_DIM // 2
    rows = pos.shape[0]
    inv_freq = ROPE_THETA ** (-2.0 * jnp.arange(half, dtype=F32) / ROT_DIM)
    ang = pos.astype(F32)[:, None] * inv_freq[None, :]
    c, s = jnp.cos(ang), jnp.sin(ang)
    rest = HEAD_DIM - ROT_DIM
    cos = jnp.concatenate([c, c, jnp.ones((rows, rest), F32)], axis=-1)
    sa = jnp.concatenate([-s, jnp.zeros((rows, HEAD_DIM - half), F32)], axis=-1)
    sb = jnp.concatenate([jnp.zeros((rows, half), F32), s, jnp.zeros((rows, rest), F32)], axis=-1)
    rep = LANES // HEAD_DIM
    return tuple(jnp.tile(t, (1, rep)) for t in (cos, sa, sb))


def _mem_kernel(*refs, n_parts):
    x_ref = refs[0]
    part_refs = refs[1:1 + n_parts]
    w_part_refs = refs[1 + n_parts:1 + 2 * n_parts]
    g_ref, wq_ref, gq_ref, k_ref, v_ref, wo_ref, o_ref = refs[1 + 2 * n_parts:]
    x = x_ref[...]
    for a_ref, w_ref in zip(part_refs, w_part_refs):
        x = x + _dot(a_ref[...], w_ref[...])
    h = _rms(x, g_ref[...]).astype(BF16)
    q = _dot(h, wq_ref[...])
    gq = gq_ref[...]
    scale = MEM_HEAD_DIM ** -0.5
    outs = []
    for hd in range(MEM_HEADS):
        sl = slice(hd * MEM_HEAD_DIM, (hd + 1) * MEM_HEAD_DIM)
        qh = _rms(q[:, sl], gq) * scale
        s = _dot_nt(qh.astype(BF16), k_ref[:, sl].astype(BF16))
        m = jnp.max(s, axis=-1, keepdims=True)
        p = jnp.exp(s - m)
        l = jnp.sum(p, axis=-1, keepdims=True)
        oh = _dot(p.astype(BF16), v_ref[:, sl].astype(BF16)) / l
        outs.append(oh.astype(BF16))
    o = jnp.concatenate(outs, axis=-1)
    o_ref[...] = x + _dot(o, wo_ref[...])


def _mem_attn(x, parts, w_parts, g, wq, gq, mk, mv, wo, *, mem_row0=0):
    b, t, d = x.shape
    m = mk.shape[1]
    tm = min(512, t)
    row_block = lambda width: pl.BlockSpec((None, tm, width), lambda bi, i: (bi, i, 0))
    mem_block = pl.BlockSpec((None, m, d), lambda bi, i: (mem_row0 + bi, 0, 0))
    return pl.pallas_call(
        functools.partial(_mem_kernel, n_parts=len(parts)),
        grid=(b, t // tm),
        in_specs=[row_block(d)] + [row_block(a.shape[2]) for a in parts]
        + [pl.BlockSpec(w.shape, lambda bi, i: (0, 0)) for w in w_parts] + [
            pl.BlockSpec((1, d), lambda bi, i: (0, 0)),
            pl.BlockSpec((d, d), lambda bi, i: (0, 0)),
            pl.BlockSpec((1, MEM_HEAD_DIM), lambda bi, i: (0, 0)),
            mem_block, mem_block,
            pl.BlockSpec((d, d), lambda bi, i: (0, 0)),
        ],
        out_specs=row_block(d),
        out_shape=jax.ShapeDtypeStruct((b, t, d), F32),
        compiler_params=_params("parallel", "parallel"),
        name="mem_attn",
    )(x, *parts, *w_parts, g.reshape(1, d), wq, gq.reshape(1, MEM_HEAD_DIM), mk, mv, wo)


def _history_rows(past_ref, new_ref, lanes):
    return jnp.concatenate([past_ref[:, lanes].astype(BF16), new_ref[:, lanes].astype(BF16)], axis=0)


def _lane_blocks(width):
    return [slice(p * LANES, (p + 1) * LANES) for p in range(width // LANES)]


def _hist_specs(tq, tp, width, layer_row0):
    new = pl.BlockSpec((None, tq, width), lambda bi: (bi, 0, 0))
    past = pl.BlockSpec((None, tp, width), lambda bi: (layer_row0 + bi, 0, 0))
    return [new, past, new, past, new]


def _diff_hist_kernel(lam_ref, g_ref, q_ref, kp_ref, kn_ref, vp_ref, vn_ref, o_ref, *, lam_init):
    tq, tp = q_ref.shape[0], kp_ref.shape[0]
    vis = (lax.broadcasted_iota(jnp.int32, (tq, tp + tq), 1) // CHUNK
           <= (tp + lax.broadcasted_iota(jnp.int32, (tq, tp + tq), 0)) // CHUNK)
    lam = _lambda(lam_ref, lam_init)
    for lanes in _lane_blocks(q_ref.shape[1]):
        kb = _history_rows(kp_ref, kn_ref, lanes)
        vb = _history_rows(vp_ref, vn_ref, lanes)
        outs = []
        for qm in _head_pair_queries(q_ref[:, lanes]):
            s = jnp.where(vis, _dot_nt(qm, kb), NEG)
            p = jnp.exp2(s - jnp.max(s, axis=-1, keepdims=True))
            outs.append(_dot(p.astype(BF16), vb) / jnp.sum(p, axis=-1, keepdims=True))
        o = outs[0] - lam * outs[1]
        o_ref[:, lanes] = (_rms(o, g_ref[...]) * (1.0 - lam_init)).astype(o_ref.dtype)


def _diff_attn_hist(lam_vecs, subln_g, q, k_past, k_new, v_past, v_new, *, layer_row0, lam_init):
    b, tq, width = q.shape
    tp = k_past.shape[1]
    return pl.pallas_call(
        functools.partial(_diff_hist_kernel, lam_init=lam_init),
        grid=(b,),
        in_specs=[
            pl.BlockSpec((4, HEAD_DIM), lambda bi: (0, 0)),
            pl.BlockSpec((1, LANES), lambda bi: (0, 0)),
        ] + _hist_specs(tq, tp, width, layer_row0),
        out_specs=pl.BlockSpec((None, tq, width), lambda bi: (bi, 0, 0)),
        out_shape=jax.ShapeDtypeStruct((b, tq, width), BF16),
        compiler_params=_params("parallel"),
        name="diff_attn_hist",
    )(lam_vecs, subln_g.reshape(1, LANES), q, k_past, k_new, v_past, v_new)


def _diff_seq_kernel(lam_ref, g_ref, q_ref, k_ref, vt_ref, o_ref, m_ref, acc_ref, sa_ref, sb_ref,
                     *, t_blk, lam_init):
    i = pl.program_id(2)
    q2 = jnp.concatenate(_head_pair_queries(q_ref[...]), axis=0)
    q2t = q2.astype(F32).T.astype(BF16)
    m_ref[...] = jnp.full_like(m_ref, NEG)
    acc_ref[...] = jnp.zeros_like(acc_ref)

    def scores(j, dst_ref):
        dst_ref[...] = _dot(k_ref[j], q2t)

    def update(src_ref, j, masked):
        st = src_ref[...]
        if masked:
            vis = (lax.broadcasted_iota(jnp.int32, (t_blk, 2 * t_blk), 0) // CHUNK
                   <= (lax.broadcasted_iota(jnp.int32, (t_blk, 2 * t_blk), 1) % t_blk) // CHUNK)
            st = jnp.where(vis, st, NEG)
        m_old = m_ref[...]
        m_new = jnp.maximum(m_old, jnp.max(st, axis=0, keepdims=True))
        alpha = jnp.exp2(m_old - m_new)
        pt = jnp.exp2(st - m_new)
        vext = jnp.concatenate([vt_ref[j], jnp.ones((SUM_ROWS, t_blk), BF16)], axis=0)
        acc_ref[...] = alpha * acc_ref[...] + _dot(vext, pt.astype(BF16))
        m_ref[...] = m_new

    scores(0, sa_ref)

    def body(jj, carry):
        j = 2 * jj
        scores(j + 1, sb_ref)
        update(sa_ref, j, False)
        scores(j + 2, sa_ref)
        update(sb_ref, j + 1, False)
        return carry

    lax.fori_loop(0, i // 2, body, 0)

    @pl.when(i % 2 == 1)
    def _():
        scores(i, sb_ref)
        update(sa_ref, i - 1, False)
        update(sb_ref, i, True)

    @pl.when(i % 2 == 0)
    def _():
        update(sa_ref, i, True)

    o12 = acc_ref[0:LANES, :] / acc_ref[LANES:LANES + 1, :]
    ot = o12[:, :t_blk] - _lambda(lam_ref, lam_init) * o12[:, t_blk:]
    ot = ot * lax.rsqrt(jnp.mean(ot * ot, axis=0, keepdims=True) + RMS_EPS) * g_ref[...] * (1.0 - lam_init)
    o_ref[...] = ot.T.astype(o_ref.dtype)


def _diff_attn_seq(lam_vecs, subln_g, q, k, vt, *, t_blk, lam_init):
    b, t, width = q.shape
    n_heads = width // LANES
    n_blk = t // t_blk
    return pl.pallas_call(
        functools.partial(_diff_seq_kernel, t_blk=t_blk, lam_init=lam_init),
        grid=(b, n_heads, n_blk),
        in_specs=[
            pl.BlockSpec((4, HEAD_DIM), lambda bi, h, i: (0, 0)),
            pl.BlockSpec((LANES, 1), lambda bi, h, i: (0, 0)),
            pl.BlockSpec((None, t_blk, LANES), lambda bi, h, i: (bi, i, h)),
            pl.BlockSpec((None, n_blk, t_blk, LANES), lambda bi, h, i: (bi, 0, 0, h)),
            pl.BlockSpec((None, None, n_blk, LANES, t_blk), lambda bi, h, i: (h, bi, 0, 0, 0)),
        ],
        out_specs=pl.BlockSpec((None, t_blk, LANES), lambda bi, h, i: (bi, i, h)),
        out_shape=jax.ShapeDtypeStruct((b, t, width), BF16),
        scratch_shapes=[pltpu.VMEM((1, 2 * t_blk), F32), pltpu.VMEM((LANES + SUM_ROWS, 2 * t_blk), F32),
                        pltpu.VMEM((t_blk, 2 * t_blk), F32), pltpu.VMEM((t_blk, 2 * t_blk), F32)],
        compiler_params=_params("parallel", "parallel", "arbitrary"),
        name="diff_attn_seq",
    )(lam_vecs, subln_g.reshape(LANES, 1), q, k.reshape(b, n_blk, t_blk, width), vt)


def _upper(n):
    return jnp.asarray((np.arange(n)[:, None] > np.arange(n)[None, :]).astype(np.float32), dtype=BF16)


def _stick_stages(zs, vbs, u, causal, r_olds):
    sps = [jnp.maximum(z, 0.0) + jnp.log(1.0 + jnp.exp(-jnp.abs(z))) for z in zs]
    log_1ms = [-sp if causal is None else jnp.where(causal, -sp, 0.0) for sp in sps]
    splits = [_split_bf16(log_1m) for log_1m in log_1ms]
    afters = [_dot(hi, u) + _dot(lo, u) for hi, lo in splits]
    ws = [jnp.exp((z - sp) + after + r_old) for z, sp, after, r_old in zip(zs, sps, afters, r_olds)]
    if causal is not None:
        ws = [jnp.where(causal, w, 0.0) for w in ws]
    wvs = [_dot(w.astype(BF16), vb) for w, vb in zip(ws, vbs)]
    return wvs, [after[:, 0:1] + log_1m[:, 0:1] for after, log_1m in zip(afters, log_1ms)]


def _stick_hist_kernel(u_ref, q_ref, kp_ref, kn_ref, vp_ref, vn_ref, o_ref):
    tq, tp = q_ref.shape[0], kp_ref.shape[0]
    causal = (lax.broadcasted_iota(jnp.int32, (tq, tp + tq), 1)
              < tp + lax.broadcasted_iota(jnp.int32, (tq, tp + tq), 0))
    for lanes in _lane_blocks(q_ref.shape[1]):
        kb = _history_rows(kp_ref, kn_ref, lanes)
        vb = _history_rows(vp_ref, vn_ref, lanes)
        zs = [_dot_nt(qm, kb) for qm in _head_pair_queries(q_ref[:, lanes])]
        outs, _ = _stick_stages(zs, [vb, vb], u_ref[...], causal, [0.0, 0.0])
        o_ref[:, lanes] = jnp.where(_lane_masks((tq, LANES)), outs[0], outs[1]).astype(o_ref.dtype)


def _stick_attn_hist(q, k_past, k_new, v_past, v_new, *, layer_row0):
    b, tq, width = q.shape
    tp = k_past.shape[1]
    return pl.pallas_call(
        _stick_hist_kernel,
        grid=(b,),
        in_specs=[pl.BlockSpec((tp + tq, tp + tq), lambda bi: (0, 0))] + _hist_specs(tq, tp, width, layer_row0),
        out_specs=pl.BlockSpec((None, tq, width), lambda bi: (bi, 0, 0)),
        out_shape=jax.ShapeDtypeStruct((b, tq, width), BF16),
        compiler_params=_params("parallel"),
        name="stick_attn_hist",
    )(_upper(tp + tq), q, k_past, k_new, v_past, v_new)


def _stick_seq_kernel(u_ref, q_ref, k_ref, v_ref, o_ref, acc_ref, r_ref, z_ref, *, t_blk, n_pairs):
    i = pl.program_id(2)
    lanes = [slice(p * LANES, (p + 1) * LANES) for p in range(n_pairs)]
    q2s = [jnp.concatenate(_head_pair_queries(q_ref[:, sl]), axis=0) for sl in lanes]
    acc_ref[...] = jnp.zeros_like(acc_ref)
    r_ref[...] = jnp.zeros_like(r_ref)
    u = u_ref[...]

    def block(j, masked):
        causal = None
        if masked:
            causal = (lax.broadcasted_iota(jnp.int32, (2 * t_blk, t_blk), 1)
                      < lax.broadcasted_iota(jnp.int32, (2 * t_blk, t_blk), 0) % t_blk)
        zs = [z_ref[p] for p in range(n_pairs)]
        k_next = k_ref[jnp.maximum(j - 1, 0)]
        for p in range(n_pairs):
            z_ref[p] = _dot_nt(q2s[p], k_next[:, lanes[p]])
        vb = v_ref[j]
        r_olds = [r_ref[p] for p in range(n_pairs)]
        wvs, r_blks = _stick_stages(zs, [vb[:, sl] for sl in lanes], u, causal, r_olds)
        r_max = None
        for p in range(n_pairs):
            acc_ref[p] += wvs[p]
            r_new = r_olds[p] + r_blks[p]
            r_ref[p] = r_new
            mx = jnp.max(r_new)
            r_max = mx if r_max is None else jnp.maximum(r_max, mx)
        return r_max

    k_diag = k_ref[i]
    for p in range(n_pairs):
        z_ref[p] = _dot_nt(q2s[p], k_diag[:, lanes[p]])
    block(i, True)

    def cond(carry):
        j, done = carry
        return jnp.logical_and(j >= 0, done == 0)

    def body(carry):
        j, _ = carry
        r_max = block(j, False)
        return j - 1, (r_max < EXP_UNDERFLOW).astype(jnp.int32)

    lax.while_loop(cond, body, (i - 1, jnp.int32(0)))
    for p in range(n_pairs):
        acc = acc_ref[p]
        o_ref[:, lanes[p]] = jnp.where(_lane_masks((t_blk, LANES)), acc[:t_blk], acc[t_blk:]).astype(o_ref.dtype)


def _stick_attn_seq(q, k, v, *, t_blk):
    b, t, width = q.shape
    n_blk = t // t_blk
    n_pairs = STICK_PAIRS
    w = n_pairs * LANES
    kv_spec = pl.BlockSpec((None, n_blk, t_blk, w), lambda bi, h, i: (bi, 0, 0, h))
    return pl.pallas_call(
        functools.partial(_stick_seq_kernel, t_blk=t_blk, n_pairs=n_pairs),
        grid=(b, width // w, n_blk),
        in_specs=[
            pl.BlockSpec((t_blk, t_blk), lambda bi, h, i: (0, 0)),
            pl.BlockSpec((None, t_blk, w), lambda bi, h, i: (bi, i, h)),
            kv_spec, kv_spec,
        ],
        out_specs=pl.BlockSpec((None, t_blk, w), lambda bi, h, i: (bi, i, h)),
        out_shape=jax.ShapeDtypeStruct((b, t, width), BF16),
        scratch_shapes=[pltpu.VMEM((n_pairs, 2 * t_blk, LANES), F32), pltpu.VMEM((n_pairs, 2 * t_blk, 1), F32),
                        pltpu.VMEM((n_pairs, 2 * t_blk, t_blk), F32)],
        compiler_params=_params("parallel", "parallel", "arbitrary"),
        name="stick_attn_seq",
    )(_upper(t_blk), q, k.reshape(b, n_blk, t_blk, width), v.reshape(b, n_blk, t_blk, width))


def _band_hist_kernel(bias_ref, q_ref, kp_ref, kn_ref, vp_ref, vn_ref, o_ref, k_roll_ref, v_roll_ref):
    tq, tp = q_ref.shape[0], kp_ref.shape[0]
    for past_ref, new_ref, roll_ref in ((kp_ref, kn_ref, k_roll_ref), (vp_ref, vn_ref, v_roll_ref)):
        roll_ref[0:tp - tq, :] = past_ref[tq:tp, :]
        roll_ref[tp - tq:tp, :] = new_ref[...]
    for pair, lanes in enumerate(_lane_blocks(q_ref.shape[1])):
        kb = _history_rows(kp_ref, kn_ref, lanes)
        vb = _history_rows(vp_ref, vn_ref, lanes)
        outs = []
        for hd, qm in enumerate(_head_pair_queries(q_ref[:, lanes])):
            s = _dot_nt(qm, kb) + bias_ref[2 * pair + hd]
            p = jnp.exp2(s - jnp.max(s, axis=-1, keepdims=True))
            outs.append(_dot(p.astype(BF16), vb) / jnp.sum(p, axis=-1, keepdims=True))
        o_ref[:, lanes] = jnp.where(_lane_masks((tq, LANES)), outs[0], outs[1]).astype(o_ref.dtype)


def _band_bias(bias_table, qpos, kpos):
    tq, tk = len(qpos), len(kpos)
    n = tq + tk
    shift = np.arange(n)
    c_minus_r = np.where(shift < tk, shift, shift - n)
    rel = np.clip(int(qpos[0] - kpos[0]) - c_minus_r, -REL_CLIP, REL_CLIP) + REL_CLIP
    vec = bias_table.astype(F32)[:, rel] * LOG2E
    toeplitz = jnp.tile(vec, (1, tq))[:, :tq * (n - 1)].reshape(-1, tq, n - 1)[:, :, :tk]
    qc, kc = qpos[:, None] // CHUNK, kpos[None, :] // CHUNK
    vis = (kpos[None, :] >= 0) & (kc <= qc) & (qc - kc <= C_PAST_CHUNKS)
    return jnp.where(jnp.asarray(vis)[None], toeplitz, NEG)


def _band_attn_hist(bias, q, k_past, k_new, v_past, v_new, *, layer_row0):
    b, tq, width = q.shape
    tp = k_past.shape[1]
    assert tq <= tp, "the rolling band buffer keeps the newest Tp rows"
    roll_spec = pl.BlockSpec((None, tp, width), lambda bi: (bi, 0, 0))
    return pl.pallas_call(
        _band_hist_kernel,
        grid=(b,),
        in_specs=[pl.BlockSpec(bias.shape, lambda bi: (0, 0, 0))] + _hist_specs(tq, tp, width, layer_row0),
        out_specs=[pl.BlockSpec((None, tq, width), lambda bi: (bi, 0, 0)), roll_spec, roll_spec],
        out_shape=[jax.ShapeDtypeStruct((b, tq, width), BF16),
                   jax.ShapeDtypeStruct((b, tp, width), k_past.dtype), jax.ShapeDtypeStruct((b, tp, width), v_past.dtype)],
        compiler_params=_params("parallel"),
        name="band_attn_hist",
    )(bias, q, k_past, k_new, v_past, v_new)


def _band_seq_kernel(bias_ref, q_ref, kp_ref, kc_ref, vp_ref, vc_ref, o_ref, *, tq, n_sub, n_win):
    big = n_sub * tq
    win = n_win * tq
    half = vp_ref.shape[2]
    prev_ok = pl.program_id(2) > 0
    top = lax.broadcasted_iota(jnp.int32, (LANES, tq), 0) < HEAD_DIM

    def value_piece(ref, off):
        return ref[off // half][:, off % half:off % half + tq]

    def scores(g):
        q2 = jnp.concatenate(_head_pair_queries(q_ref[g * tq:(g + 1) * tq, :]), axis=0)
        q2t = q2.astype(F32).T.astype(BF16)
        start = big - (n_win - 1) * tq + g * tq
        n_prev = max(big - start, 0)
        k_pieces, v_pieces = [], []
        for d in range(n_win):
            off = start + d * tq
            if off < big:
                k_pieces.append(kp_ref[off:off + tq, :])
                v_pieces.append(value_piece(vp_ref, off))
            else:
                k_pieces.append(kc_ref[off - big:off - big + tq, :])
                v_pieces.append(value_piece(vc_ref, off - big))
        st = _dot(jnp.concatenate(k_pieces, axis=0), q2t) + bias_ref[...]
        if n_prev > 0:
            before_start = jnp.logical_and(lax.broadcasted_iota(jnp.int32, (win, 2 * tq), 0) < n_prev,
                                           jnp.logical_not(prev_ok))
            st = jnp.where(before_start, NEG, st)
        return st, jnp.concatenate(v_pieces, axis=1)

    def finish(g, st, vwt):
        pt = jnp.exp2(st - jnp.max(st, axis=0, keepdims=True))
        ot = _dot(vwt, pt.astype(BF16)) / jnp.sum(pt, axis=0, keepdims=True)
        o_ref[g * tq:(g + 1) * tq, :] = jnp.where(top, ot[:, :tq], ot[:, tq:]).T.astype(o_ref.dtype)

    nxt = scores(0)
    for g in range(n_sub):
        cur_scores = nxt
        if g + 1 < n_sub:
            nxt = scores(g + 1)
        finish(g, *cur_scores)


def _band_attn_seq(bias_t, q, k, vt, *, tq, n_win, n_sub):
    b, t, width = q.shape
    big = n_sub * tq
    half = vt.shape[3]
    n_half = big // half
    assert (n_win - 1) * tq <= big, "the window must fit in the previous + current block"
    assert half % tq == 0, "a window piece must not straddle two transposed value blocks"
    cur = pl.BlockSpec((None, big, LANES), lambda bi, h, i: (bi, i, h))
    prev = pl.BlockSpec((None, big, LANES), lambda bi, h, i: (bi, jnp.maximum(i - 1, 0), h))
    per_b = t // big
    vcur = pl.BlockSpec((None, n_half, LANES, half), lambda bi, h, i: (h, bi * per_b + i, 0, 0))
    vprev = pl.BlockSpec((None, n_half, LANES, half), lambda bi, h, i: (h, bi * per_b + jnp.maximum(i - 1, 0), 0, 0))
    return pl.pallas_call(
        functools.partial(_band_seq_kernel, tq=tq, n_sub=n_sub, n_win=n_win),
        grid=(b, width // LANES, t // big),
        in_specs=[pl.BlockSpec((None, n_win * tq, 2 * tq), lambda bi, h, i: (h, 0, 0)), cur, prev, cur, vprev, vcur],
        out_specs=cur,
        out_shape=jax.ShapeDtypeStruct((b, t, width), BF16),
        compiler_params=_params("parallel", "parallel", "parallel"),
        name="band_attn_seq",
    )(bias_t, q, k, k, vt, vt)


def _lambda_init(layer):
    return 0.8 - 0.6 * math.exp(-0.3 * layer)


def _tile_gain(g, n):
    return jnp.tile(g.astype(F32), n)


def _trunk(x, pos0, mem_k, mem_v, past_ab, past_c, p):
    b, t, d = x.shape
    rows = b * t
    depth = p['ffn1_g'].shape[0]
    seq = past_ab is None
    new_ab, new_c = [], []
    pos_rows = jnp.tile(pos0 + jnp.arange(t), b)
    x = x.reshape(rows, d)
    heads_per_tile = TILE_N // HEAD_DIM
    for l in range(depth):
        x = _ffn(x, p['ffn1_g'][l], p['ffn1_wg'][l], p['ffn1_wu'][l], p['ffn1_wd'][l])
        li = l // 2
        if l % 2 == 0:
            gains = jnp.stack([_tile_gain(p['a_gq'][li], heads_per_tile), _tile_gain(p['a_gk'][li], heads_per_tile)])
            state_heads = [(TILE_N // HEAD_DIM, HEAD_DIM), (TILE_N // (2 * HEAD_DIM), 2 * HEAD_DIM),
                           (TILE_N // HEAD_DIM, HEAD_DIM), (TILE_N // HEAD_DIM, HEAD_DIM)]
            if seq:
                outs = [("heads", n, w, F32) for n, w in state_heads]
                state = lambda idx: ("split", idx)
            else:
                outs = [("rows", TILE_N, F32)] * 4
                state = lambda idx: ("copy", idx, 0, 1.0)
            outs = outs + [("rows", TILE_N, BF16)] * 5
            v_dests = (state(1),)
            if seq:
                outs = outs + [("heads_t", TILE_N // LANES, BF16)]
                v_dests += (("heads_t", 9, 0),)
            tiles = [
                (0, True, (("copy", 4, 0, SOFTMAX_Q_SCALE),)),
                (1, True, (state(0), ("copy", 5, 0, 1.0))),
                (None, False, v_dests),
                (None, False, (("copy", 6, 0, STICK_Q_SCALE),)),
                (None, False, (state(2), ("copy", 7, 0, 1.0))),
                (None, False, (state(3), ("copy", 8, 0, 1.0))),
            ]
            res = _proj(x, p['mix_g'][l], p['ab_w_in'][li], gains, HEAD_DIM, tiles, outs,
                        rope_tables=_rope_tables(pos_rows), name="proj_ab")
            a_k, a_v, b_k, b_v = (r.reshape((b, t) + r.shape[1:]) for r in res[:4])
            qa, ka, qb, kb, vb = (r.reshape(b, t, TILE_N) for r in res[4:9])
            new_ab.append((a_k, a_v, b_k, b_v))
            lam_vecs = jnp.stack([p['a_lq1'][li], p['a_lk1'][li], p['a_lq2'][li], p['a_lk2'][li]]).astype(F32)
            lam_init = _lambda_init(l)
            if seq:
                t_blk = min(SEQ_BLOCK, t)
                vt = res[9].reshape(TILE_N // LANES, b, t // t_blk, LANES, t_blk)
                a_out = _diff_attn_seq(lam_vecs, p['a_subln_g'][li], qa, ka, vt, t_blk=t_blk, lam_init=lam_init)
                b_out = _stick_attn_seq(qb, kb, vb, t_blk=min(STICK_BLOCK, t))
            else:
                pka, pva, pkb, pvb = (c.reshape(-1, c.shape[2], TILE_N) for c in past_ab)
                a_out = _diff_attn_hist(lam_vecs, p['a_subln_g'][li], qa, pka, a_k, pva, a_v,
                                        layer_row0=li * b, lam_init=lam_init)
                b_out = _stick_attn_hist(qb, pkb, b_k, pvb, b_v, layer_row0=li * b)
            w_out = p['ab_w_out'][li]
            half = w_out.shape[0] // 2
            mixed, w_parts = [a_out, b_out], [w_out[:half], w_out[half:]]
        else:
            gains = jnp.stack([_tile_gain(p['c_gq'][li], heads_per_tile), _tile_gain(p['c_gk'][li], heads_per_tile)])
            groups = TILE_N // LANES
            outs = [("rows", d, F32)] * 2 + [("rows", d, BF16)] * 2
            if seq:
                outs = outs + [("heads_t", d // LANES, BF16)]
            tiles = ([(0, False, (("copy", 2, c, SOFTMAX_Q_SCALE),)) for c in (0, TILE_N)]
                     + [(1, False, (("copy", 0, c, 1.0), ("copy", 3, c, 1.0))) for c in (0, TILE_N)]
                     + [(None, False, (("copy", 1, c * TILE_N, 1.0),) + ((("heads_t", 4, c * groups),) if seq else ()))
                        for c in range(d // TILE_N)])
            res = _proj(x, p['mix_g'][l], p['c_w_in'][li], gains, HEAD_DIM, tiles, outs, name="proj_c")
            k_f, v_f, q_c, k_c = (r.reshape(b, t, d) for r in res[:4])
            band = C_PAST_CHUNKS * CHUNK
            if seq:
                tq = min(BAND_BLOCK, t)
                nb = band // tq + 1
                bias = _band_bias(p['c_bias'][li], (nb - 1) * tq + np.arange(tq), np.arange(nb * tq))
                bias_t = jnp.transpose(bias.reshape(-1, 2, tq, nb * tq), (0, 3, 1, 2)).reshape(-1, nb * tq, 2 * tq)
                o = _band_attn_seq(bias_t, q_c, k_c, res[4], tq=tq, n_win=nb, n_sub=min(BAND_GROUP, t // tq))
                keep = min(band, t)
                new_c.append((k_f[:, t - keep:], v_f[:, t - keep:]))
            else:
                pk, pv = (c.reshape(-1, c.shape[2], d) for c in past_c)
                pc = pk.shape[1]
                bias = _band_bias(p['c_bias'][li], pos0 + np.arange(t), pos0 - pc + np.arange(pc + t))
                o, k_roll, v_roll = _band_attn_hist(bias, q_c, pk, k_f, pv, v_f, layer_row0=li * b)
                new_c.append((k_roll, v_roll))
            mixed, w_parts = [o], [p['c_w_out'][li]]
        if seq:
            mk, mv, mem_row0 = mem_k[l], mem_v[l], 0
        else:
            mk, mv, mem_row0 = mem_k, mem_v, l * b
        x = _mem_attn(x.reshape(b, t, d), mixed, w_parts, p['mem_g_x'][l], p['mem_wq'][l], p['mem_gq'][l],
                      mk, mv, p['mem_wo'][l], mem_row0=mem_row0).reshape(rows, d)
        x = _ffn(x, p['ffn2_g'][l], p['ffn2_wg'][l], p['ffn2_wu'][l], p['ffn2_wd'][l])
    return x.reshape(b, t, d), new_ab, new_c


def kernel(x_prompt, x_sample, cache_a_k, cache_a_v, cache_b_k, cache_b_v, cache_c_k, cache_c_v, cache_mem_k, cache_mem_v, mem_prompt, ffn1_g, ffn1_wg, ffn1_wu, ffn1_wd, ffn2_g, ffn2_wg, ffn2_wu, ffn2_wd, mix_g, ab_w_in, ab_w_out, a_gq, a_gk, a_lq1, a_lk1, a_lq2, a_lk2, a_subln_g, c_w_in, c_w_out, c_gq, c_gk, c_bias, mem_g_x, mem_g_m, mem_wq, mem_wk, mem_wv, mem_wo, mem_gq, mem_gk):
    depth = ffn1_g.shape[0]
    p = dict(ffn1_g=ffn1_g, ffn2_g=ffn2_g, mix_g=mix_g, a_gq=a_gq, a_gk=a_gk, a_lq1=a_lq1, a_lk1=a_lk1,
             a_lq2=a_lq2, a_lk2=a_lk2, a_subln_g=a_subln_g, c_gq=c_gq, c_gk=c_gk, c_bias=c_bias,
             mem_g_x=mem_g_x, mem_gq=mem_gq)
    for name, w in (('ffn1_wg', ffn1_wg), ('ffn1_wu', ffn1_wu), ('ffn1_wd', ffn1_wd), ('ffn2_wg', ffn2_wg),
                    ('ffn2_wu', ffn2_wu), ('ffn2_wd', ffn2_wd), ('ab_w_in', ab_w_in), ('ab_w_out', ab_w_out),
                    ('c_w_in', c_w_in), ('c_w_out', c_w_out), ('mem_wq', mem_wq), ('mem_wo', mem_wo)):
        p[name] = w.astype(BF16)

    bp, tp, d = x_prompt.shape
    bs, ts, _ = x_sample.shape
    m_len = mem_prompt.shape[1]
    past_len = cache_a_k.shape[2]

    mem_k_list, mem_v_list = [], []
    n_half = d // TILE_N
    for l in range(depth):
        w_kv = jnp.concatenate([mem_wk[l], mem_wv[l]], axis=1).astype(BF16)
        gains = _tile_gain(mem_gk[l], TILE_N // MEM_HEAD_DIM)[None]
        tiles = ([(0, False, (("copy", 0, c * TILE_N, 1.0),)) for c in range(n_half)]
                 + [(None, False, (("copy", 1, c * TILE_N, 1.0),)) for c in range(n_half)])
        mk, mv = _proj(mem_prompt.reshape(bp * m_len, d), mem_g_m[l], w_kv, gains, MEM_HEAD_DIM, tiles,
                       [("rows", d, F32)] * 2, name="proj_mem_kv")
        mem_k_list.append(mk.reshape(bp, m_len, d))
        mem_v_list.append(mv.reshape(bp, m_len, d))

    y_prompt, ab_p, c_p = _trunk(x_prompt, 0, mem_k_list, mem_v_list, None, None, p)
    cache_mem_k2 = cache_mem_k.reshape(depth * bs, m_len, d)
    cache_mem_v2 = cache_mem_v.reshape(depth * bs, m_len, d)
    y_sample, ab_s, c_s = _trunk(x_sample, past_len, cache_mem_k2, cache_mem_v2,
                                 (cache_a_k, cache_a_v, cache_b_k, cache_b_v), (cache_c_k, cache_c_v), p)

    a_heads = cache_a_v.shape[3]
    b_heads = cache_b_k.shape[3]
    c_heads = cache_c_k.shape[3]

    def stack(rows, idx, heads, width):
        return jnp.stack([r[idx].reshape(r[idx].shape[0], r[idx].shape[1], heads, width) for r in rows])

    outs = [y_prompt, y_sample]
    outs += [stack(ab_p, 0, 2 * a_heads, HEAD_DIM), stack(ab_p, 1, a_heads, 2 * HEAD_DIM),
             stack(ab_p, 2, b_heads, HEAD_DIM), stack(ab_p, 3, b_heads, HEAD_DIM),
             stack(c_p, 0, c_heads, HEAD_DIM), stack(c_p, 1, c_heads, HEAD_DIM)]
    outs += [jnp.stack(mem_k_list).reshape(depth, bp, m_len, MEM_HEADS, MEM_HEAD_DIM),
             jnp.stack(mem_v_list).reshape(depth, bp, m_len, MEM_HEADS, MEM_HEAD_DIM)]
    outs += [stack(ab_s, 0, 2 * a_heads, HEAD_DIM), stack(ab_s, 1, a_heads, 2 * HEAD_DIM),
             stack(ab_s, 2, b_heads, HEAD_DIM), stack(ab_s, 3, b_heads, HEAD_DIM),
             stack(c_s, 0, c_heads, HEAD_DIM), stack(c_s, 1, c_heads, HEAD_DIM)]
    return tuple(outs)
```

```python
import functools
import math

import jax
import jax.numpy as jnp
import numpy as np
from jax import lax
from jax.experimental import pallas as pl
from jax.experimental.pallas import tpu as pltpu

D_MODEL = 1024
CHUNK = 64
HEAD_DIM = 64
ROT_DIM = HEAD_DIM // 4
ROPE_THETA = 500000.0
C_PAST_CHUNKS = 8
REL_CLIP = 128
MEM_HEADS = 4
MEM_HEAD_DIM = D_MODEL // MEM_HEADS
RMS_EPS = 1e-6
NEG = -1e30
LOG2E = 1.4426950408889634
LANES = 128
TILE_N = 512
SEQ_BLOCK = 512
STICK_BLOCK = 256
STICK_PAIRS = 2
BAND_BLOCK = 256
BAND_GROUP = 16
FFN_TILE = 256
SUM_ROWS = 16
EXP_UNDERFLOW = -104.0
VMEM_LIMIT = 56 * 1024 * 1024
SOFTMAX_Q_SCALE = HEAD_DIM ** -0.5 * LOG2E
STICK_Q_SCALE = HEAD_DIM ** -0.5

BF16 = jnp.bfloat16
F32 = jnp.float32


def _dot(a, b):
    return jnp.dot(a, b, preferred_element_type=F32)


def _dot_nt(a, b):
    return lax.dot_general(a, b, (((1,), (1,)), ((), ())), preferred_element_type=F32)


def _rms(x, g):
    return x * lax.rsqrt(jnp.mean(x * x, axis=-1, keepdims=True) + RMS_EPS) * g


def _split_bf16(x):
    hi = x.astype(BF16)
    lo = (x - hi.astype(F32)).astype(BF16)
    return hi, lo


def _params(*sem):
    return pltpu.CompilerParams(dimension_semantics=sem, vmem_limit_bytes=VMEM_LIMIT)


def _lane_masks(shape):
    lane = lax.broadcasted_iota(jnp.int32, shape, 1)
    return lane < HEAD_DIM


def _head_pair_queries(q):
    first = _lane_masks(q.shape)
    zero = jnp.zeros_like(q)
    return jnp.where(first, q, zero), jnp.where(first, zero, q)


def _lambda(lam_ref, lam_init):
    lv = lam_ref[...]
    return (jnp.exp(jnp.sum(lv[0:1] * lv[1:2], axis=-1, keepdims=True))
            - jnp.exp(jnp.sum(lv[2:3] * lv[3:4], axis=-1, keepdims=True)) + lam_init)


def _ffn_kernel(x_ref, g_ref, wg_ref, wu_ref, wd_ref, o_ref):
    x = x_ref[...]
    h = _rms(x, g_ref[...]).astype(BF16)
    acc = None
    for f in range(wg_ref.shape[1] // FFN_TILE):
        sl = slice(f * FFN_TILE, (f + 1) * FFN_TILE)
        a = _dot(h, wg_ref[:, sl])
        u = _dot(h, wu_ref[:, sl])
        act = a * (1.0 / (1.0 + jnp.exp(-a))) * u
        part = _dot(act.astype(BF16), wd_ref[sl, :])
        acc = part if acc is None else acc + part
    o_ref[...] = x + 0.5 * acc


def _ffn(x, g, wg, wu, wd):
    rows, d = x.shape
    ff = wg.shape[1]
    tm = min(512, rows)
    once = pl.Buffered(1)
    return pl.pallas_call(
        _ffn_kernel,
        grid=(rows // tm,),
        in_specs=[
            pl.BlockSpec((tm, d), lambda i: (i, 0)),
            pl.BlockSpec((1, d), lambda i: (0, 0)),
            pl.BlockSpec((d, ff), lambda i: (0, 0), pipeline_mode=once),
            pl.BlockSpec((d, ff), lambda i: (0, 0), pipeline_mode=once),
            pl.BlockSpec((ff, d), lambda i: (0, 0), pipeline_mode=once),
        ],
        out_specs=pl.BlockSpec((tm, d), lambda i: (i, 0)),
        out_shape=jax.ShapeDtypeStruct((rows, d), F32),
        compiler_params=_params("parallel"),
        name="ffn",
    )(x, g.reshape(1, d), wg, wu, wd)


def _proj_kernel(*refs, tiles, rope):
    x_ref, g_ref, w_ref, gain_ref, seg_ref = refs[:5]
    n_in = 8 if rope else 5
    out_refs = refs[n_in:]
    h = _rms(x_ref[...], g_ref[...]).astype(BF16)
    for t, (norm, use_rope, dests) in enumerate(tiles):
        y = _dot(h, w_ref[:, t * TILE_N:(t + 1) * TILE_N])
        if norm is not None:
            ms = _dot((y * y).astype(BF16), seg_ref[...])
            y = y * lax.rsqrt(ms + RMS_EPS) * gain_ref[norm:norm + 1, :]
        if use_rope:
            cos, sa, sb = refs[5][...], refs[6][...], refs[7][...]
            half = ROT_DIM // 2
            blocks = []
            for c in range(TILE_N // LANES):
                yb = y[:, c * LANES:(c + 1) * LANES]
                blocks.append(yb * cos + pltpu.roll(yb, LANES - half, 1) * sa + pltpu.roll(yb, half, 1) * sb)
            y = jnp.concatenate(blocks, axis=-1)
        for dest in dests:
            ref = out_refs[dest[1]]
            if dest[0] == "copy":
                _, _, col, scale = dest
                ref[:, col:col + TILE_N] = (y if scale == 1.0 else y * scale).astype(ref.dtype)
            elif dest[0] == "split":
                ref[...] = y.reshape(ref.shape).astype(ref.dtype)
            else:
                groups = TILE_N // LANES
                ref[dest[2]:dest[2] + groups] = y.T.reshape((groups,) + ref.shape[1:]).astype(ref.dtype)


def _seg_matrix(seg):
    idx = np.arange(TILE_N) // seg
    return jnp.asarray((idx[:, None] == idx[None, :]).astype(np.float32) / seg, dtype=BF16)


def _proj(x, g, w, gains, seg, tiles, outs, rope_tables=None, name="proj"):
    rows, d = x.shape
    tm = min(SEQ_BLOCK, rows)
    rope = rope_tables is not None
    in_specs = [
        pl.BlockSpec((tm, d), lambda i: (i, 0)),
        pl.BlockSpec((1, d), lambda i: (0, 0)),
        pl.BlockSpec(w.shape, lambda i: (0, 0)),
        pl.BlockSpec(gains.shape, lambda i: (0, 0)),
        pl.BlockSpec((TILE_N, TILE_N), lambda i: (0, 0)),
    ]
    args = [x, g.reshape(1, d), w, gains, _seg_matrix(seg)]
    if rope:
        in_specs += [pl.BlockSpec((tm, LANES), lambda i: (i, 0))] * 3
        args += list(rope_tables)
    out_specs, out_shape = [], []
    for o in outs:
        if o[0] == "rows":
            _, width, dtype = o
            out_specs.append(pl.BlockSpec((tm, width), lambda i: (i, 0)))
            out_shape.append(jax.ShapeDtypeStruct((rows, width), dtype))
        elif o[0] == "heads":
            _, heads, head_width, dtype = o
            out_specs.append(pl.BlockSpec((tm, heads, head_width), lambda i: (i, 0, 0)))
            out_shape.append(jax.ShapeDtypeStruct((rows, heads, head_width), dtype))
        else:
            _, groups, dtype = o
            out_specs.append(pl.BlockSpec((groups, None, LANES, tm), lambda i: (0, i, 0, 0)))
            out_shape.append(jax.ShapeDtypeStruct((groups, rows // tm, LANES, tm), dtype))
    return pl.pallas_call(
        functools.partial(_proj_kernel, tiles=tuple(tiles), rope=rope),
        grid=(rows // tm,),
        in_specs=in_specs,
        out_specs=out_specs,
        out_shape=out_shape,
        compiler_params=_params("parallel"),
        name=name,
    )(*args)


def _rope_tables(pos):
    half = ROT_DIM // 2
    rows = pos.shape[0]
    inv_freq = ROPE_THETA ** (-2.0 * jnp.arange(half, dtype=F32) / ROT_DIM)
    ang = pos.astype(F32)[:, None] * inv_freq[None, :]
    c, s = jnp.cos(ang), jnp.sin(ang)
    rest = HEAD_DIM - ROT_DIM
    cos = jnp.concatenate([c, c, jnp.ones((rows, rest), F32)], axis=-1)
    sa = jnp.concatenate([-s, jnp.zeros((rows, HEAD_DIM - half), F32)], axis=-1)
    sb = jnp.concatenate([jnp.zeros((rows, half), F32), s, jnp.zeros((rows, rest), F32)], axis=-1)
    rep = LANES // HEAD_DIM
    return tuple(jnp.tile(t, (1, rep)) for t in (cos, sa, sb))


def _mem_kernel(*refs, n_parts):
    x_ref = refs[0]
    part_refs = refs[1:1 + n_parts]
    w_part_refs = refs[1 + n_parts:1 + 2 * n_parts]
    g_ref, wq_ref, gq_ref, k_ref, v_ref, wo_ref, o_ref = refs[1 + 2 * n_parts:]
    x = x_ref[...]
    for a_ref, w_ref in zip(part_refs, w_part_refs):
        x = x + _dot(a_ref[...], w_ref[...])
    h = _rms(x, g_ref[...]).astype(BF16)
    q = _dot(h, wq_ref[...])
    gq = gq_ref[...]
    scale = MEM_HEAD_DIM ** -0.5
    sls = [slice(hd * MEM_HEAD_DIM, (hd + 1) * MEM_HEAD_DIM) for hd in range(MEM_HEADS)]
    qhs = [(_rms(q[:, sl], gq) * scale).astype(BF16) for sl in sls]
    ss = [_dot_nt(qh, k_ref[:, sl].astype(BF16)) for qh, sl in zip(qhs, sls)]
    ps = [jnp.exp(s - jnp.max(s, axis=-1, keepdims=True)) for s in ss]
    ohs = [_dot(p.astype(BF16), v_ref[:, sl].astype(BF16)) / jnp.sum(p, axis=-1, keepdims=True)
           for p, sl in zip(ps, sls)]
    o = jnp.concatenate([oh.astype(BF16) for oh in ohs], axis=-1)
    o_ref[...] = x + _dot(o, wo_ref[...])


def _mem_attn(x, parts, w_parts, g, wq, gq, mk, mv, wo, *, mem_row0=0):
    b, t, d = x.shape
    m = mk.shape[1]
    tm = min(512, t)
    row_block = lambda width: pl.BlockSpec((None, tm, width), lambda bi, i: (bi, i, 0))
    mem_block = pl.BlockSpec((None, m, d), lambda bi, i: (mem_row0 + bi, 0, 0))
    return pl.pallas_call(
        functools.partial(_mem_kernel, n_parts=len(parts)),
        grid=(b, t // tm),
        in_specs=[row_block(d)] + [row_block(a.shape[2]) for a in parts]
        + [pl.BlockSpec(w.shape, lambda bi, i: (0, 0)) for w in w_parts] + [
            pl.BlockSpec((1, d), lambda bi, i: (0, 0)),
            pl.BlockSpec((d, d), lambda bi, i: (0, 0)),
            pl.BlockSpec((1, MEM_HEAD_DIM), lambda bi, i: (0, 0)),
            mem_block, mem_block,
            pl.BlockSpec((d, d), lambda bi, i: (0, 0)),
        ],
        out_specs=row_block(d),
        out_shape=jax.ShapeDtypeStruct((b, t, d), F32),
        compiler_params=_params("parallel", "parallel"),
        name="mem_attn",
    )(x, *parts, *w_parts, g.reshape(1, d), wq, gq.reshape(1, MEM_HEAD_DIM), mk, mv, wo)


def _history_rows(past_ref, new_ref, lanes):
    return jnp.concatenate([past_ref[:, lanes].astype(BF16), new_ref[:, lanes].astype(BF16)], axis=0)


def _lane_blocks(width):
    return [slice(p * LANES, (p + 1) * LANES) for p in range(width // LANES)]


def _hist_specs(tq, tp, width, layer_row0):
    new = pl.BlockSpec((None, tq, width), lambda bi: (bi, 0, 0))
    past = pl.BlockSpec((None, tp, width), lambda bi: (layer_row0 + bi, 0, 0))
    return [new, past, new, past, new]


def _diff_hist_kernel(lam_ref, g_ref, q_ref, kp_ref, kn_ref, vp_ref, vn_ref, o_ref, *, lam_init):
    tq, tp = q_ref.shape[0], kp_ref.shape[0]
    vis = (lax.broadcasted_iota(jnp.int32, (tq, tp + tq), 1) // CHUNK
           <= (tp + lax.broadcasted_iota(jnp.int32, (tq, tp + tq), 0)) // CHUNK)
    lam = _lambda(lam_ref, lam_init)
    for lanes in _lane_blocks(q_ref.shape[1]):
        kb = _history_rows(kp_ref, kn_ref, lanes)
        vb = _history_rows(vp_ref, vn_ref, lanes)
        outs = []
        for qm in _head_pair_queries(q_ref[:, lanes]):
            s = jnp.where(vis, _dot_nt(qm, kb), NEG)
            p = jnp.exp2(s - jnp.max(s, axis=-1, keepdims=True))
            outs.append(_dot(p.astype(BF16), vb) / jnp.sum(p, axis=-1, keepdims=True))
        o = outs[0] - lam * outs[1]
        o_ref[:, lanes] = (_rms(o, g_ref[...]) * (1.0 - lam_init)).astype(o_ref.dtype)


def _diff_attn_hist(lam_vecs, subln_g, q, k_past, k_new, v_past, v_new, *, layer_row0, lam_init):
    b, tq, width = q.shape
    tp = k_past.shape[1]
    return pl.pallas_call(
        functools.partial(_diff_hist_kernel, lam_init=lam_init),
        grid=(b,),
        in_specs=[
            pl.BlockSpec((4, HEAD_DIM), lambda bi: (0, 0)),
            pl.BlockSpec((1, LANES), lambda bi: (0, 0)),
        ] + _hist_specs(tq, tp, width, layer_row0),
        out_specs=pl.BlockSpec((None, tq, width), lambda bi: (bi, 0, 0)),
        out_shape=jax.ShapeDtypeStruct((b, tq, width), BF16),
        compiler_params=_params("parallel"),
        name="diff_attn_hist",
    )(lam_vecs, subln_g.reshape(1, LANES), q, k_past, k_new, v_past, v_new)


def _diff_seq_kernel(lam_ref, g_ref, q_ref, k_ref, vt_ref, o_ref, m_ref, acc_ref, sa_ref, sb_ref,
                     *, t_blk, lam_init):
    i = pl.program_id(2)
    q2 = jnp.concatenate(_head_pair_queries(q_ref[...]), axis=0)
    q2t = q2.astype(F32).T.astype(BF16)
    m_ref[...] = jnp.full_like(m_ref, NEG)
    acc_ref[...] = jnp.zeros_like(acc_ref)

    def scores(j, dst_ref):
        dst_ref[...] = _dot(k_ref[j], q2t)

    def update(src_ref, j, masked):
        st = src_ref[...]
        if masked:
            vis = (lax.broadcasted_iota(jnp.int32, (t_blk, 2 * t_blk), 0) // CHUNK
                   <= (lax.broadcasted_iota(jnp.int32, (t_blk, 2 * t_blk), 1) % t_blk) // CHUNK)
            st = jnp.where(vis, st, NEG)
        m_old = m_ref[...]
        m_new = jnp.maximum(m_old, jnp.max(st, axis=0, keepdims=True))
        alpha = jnp.exp2(m_old - m_new)
        pt = jnp.exp2(st - m_new)
        vext = jnp.concatenate([vt_ref[j], jnp.ones((SUM_ROWS, t_blk), BF16)], axis=0)
        acc_ref[...] = alpha * acc_ref[...] + _dot(vext, pt.astype(BF16))
        m_ref[...] = m_new

    scores(0, sa_ref)

    def body(jj, carry):
        j = 2 * jj
        scores(j + 1, sb_ref)
        update(sa_ref, j, False)
        scores(j + 2, sa_ref)
        update(sb_ref, j + 1, False)
        return carry

    lax.fori_loop(0, i // 2, body, 0)

    @pl.when(i % 2 == 1)
    def _():
        scores(i, sb_ref)
        update(sa_ref, i - 1, False)
        update(sb_ref, i, True)

    @pl.when(i % 2 == 0)
    def _():
        update(sa_ref, i, True)

    o12 = acc_ref[0:LANES, :] / acc_ref[LANES:LANES + 1, :]
    ot = o12[:, :t_blk] - _lambda(lam_ref, lam_init) * o12[:, t_blk:]
    ot = ot * lax.rsqrt(jnp.mean(ot * ot, axis=0, keepdims=True) + RMS_EPS) * g_ref[...] * (1.0 - lam_init)
    o_ref[...] = ot.T.astype(o_ref.dtype)


def _diff_attn_seq(lam_vecs, subln_g, q, k, vt, *, t_blk, lam_init):
    b, t, width = q.shape
    n_heads = width // LANES
    n_blk = t // t_blk
    return pl.pallas_call(
        functools.partial(_diff_seq_kernel, t_blk=t_blk, lam_init=lam_init),
        grid=(b, n_heads, n_blk),
        in_specs=[
            pl.BlockSpec((4, HEAD_DIM), lambda bi, h, i: (0, 0)),
            pl.BlockSpec((LANES, 1), lambda bi, h, i: (0, 0)),
            pl.BlockSpec((None, t_blk, LANES), lambda bi, h, i: (bi, i, h)),
            pl.BlockSpec((None, n_blk, t_blk, LANES), lambda bi, h, i: (bi, 0, 0, h)),
            pl.BlockSpec((None, None, n_blk, LANES, t_blk), lambda bi, h, i: (h, bi, 0, 0, 0)),
        ],
        out_specs=pl.BlockSpec((None, t_blk, LANES), lambda bi, h, i: (bi, i, h)),
        out_shape=jax.ShapeDtypeStruct((b, t, width), BF16),
        scratch_shapes=[pltpu.VMEM((1, 2 * t_blk), F32), pltpu.VMEM((LANES + SUM_ROWS, 2 * t_blk), F32),
                        pltpu.VMEM((t_blk, 2 * t_blk), F32), pltpu.VMEM((t_blk, 2 * t_blk), F32)],
        compiler_params=_params("parallel", "parallel", "arbitrary"),
        name="diff_attn_seq",
    )(lam_vecs, subln_g.reshape(LANES, 1), q, k.reshape(b, n_blk, t_blk, width), vt)


def _upper(n):
    return jnp.asarray((np.arange(n)[:, None] > np.arange(n)[None, :]).astype(np.float32), dtype=BF16)


def _stick_stages(zs, vbs, u, causal, r_olds):
    sps = [jnp.maximum(z, 0.0) + jnp.log(1.0 + jnp.exp(-jnp.abs(z))) for z in zs]
    log_1ms = [-sp if causal is None else jnp.where(causal, -sp, 0.0) for sp in sps]
    splits = [_split_bf16(log_1m) for log_1m in log_1ms]
    afters = [_dot(hi, u) + _dot(lo, u) for hi, lo in splits]
    ws = [jnp.exp((z - sp) + after + r_old) for z, sp, after, r_old in zip(zs, sps, afters, r_olds)]
    if causal is not None:
        ws = [jnp.where(causal, w, 0.0) for w in ws]
    wvs = [_dot(w.astype(BF16), vb) for w, vb in zip(ws, vbs)]
    return wvs, [after[:, 0:1] + log_1m[:, 0:1] for after, log_1m in zip(afters, log_1ms)]


def _stick_hist_kernel(u_ref, q_ref, kp_ref, kn_ref, vp_ref, vn_ref, o_ref):
    tq, tp = q_ref.shape[0], kp_ref.shape[0]
    causal = (lax.broadcasted_iota(jnp.int32, (tq, tp + tq), 1)
              < tp + lax.broadcasted_iota(jnp.int32, (tq, tp + tq), 0))
    for lanes in _lane_blocks(q_ref.shape[1]):
        kb = _history_rows(kp_ref, kn_ref, lanes)
        vb = _history_rows(vp_ref, vn_ref, lanes)
        zs = [_dot_nt(qm, kb) for qm in _head_pair_queries(q_ref[:, lanes])]
        outs, _ = _stick_stages(zs, [vb, vb], u_ref[...], causal, [0.0, 0.0])
        o_ref[:, lanes] = jnp.where(_lane_masks((tq, LANES)), outs[0], outs[1]).astype(o_ref.dtype)


def _stick_attn_hist(q, k_past, k_new, v_past, v_new, *, layer_row0):
    b, tq, width = q.shape
    tp = k_past.shape[1]
    return pl.pallas_call(
        _stick_hist_kernel,
        grid=(b,),
        in_specs=[pl.BlockSpec((tp + tq, tp + tq), lambda bi: (0, 0))] + _hist_specs(tq, tp, width, layer_row0),
        out_specs=pl.BlockSpec((None, tq, width), lambda bi: (bi, 0, 0)),
        out_shape=jax.ShapeDtypeStruct((b, tq, width), BF16),
        compiler_params=_params("parallel"),
        name="stick_attn_hist",
    )(_upper(tp + tq), q, k_past, k_new, v_past, v_new)


def _stick_seq_kernel(u_ref, q_ref, k_ref, v_ref, o_ref, acc_ref, r_ref, z_ref, *, t_blk, n_pairs):
    i = pl.program_id(2)
    lanes = [slice(p * LANES, (p + 1) * LANES) for p in range(n_pairs)]
    q2s = [jnp.concatenate(_head_pair_queries(q_ref[:, sl]), axis=0) for sl in lanes]
    acc_ref[...] = jnp.zeros_like(acc_ref)
    r_ref[...] = jnp.zeros_like(r_ref)
    u = u_ref[...]

    def block(j, masked):
        causal = None
        if masked:
            causal = (lax.broadcasted_iota(jnp.int32, (2 * t_blk, t_blk), 1)
                      < lax.broadcasted_iota(jnp.int32, (2 * t_blk, t_blk), 0) % t_blk)
        zs = [z_ref[p] for p in range(n_pairs)]
        k_next = k_ref[jnp.maximum(j - 1, 0)]
        for p in range(n_pairs):
            z_ref[p] = _dot_nt(q2s[p], k_next[:, lanes[p]])
        vb = v_ref[j]
        r_olds = [r_ref[p] for p in range(n_pairs)]
        wvs, r_blks = _stick_stages(zs, [vb[:, sl] for sl in lanes], u, causal, r_olds)
        r_max = None
        for p in range(n_pairs):
            acc_ref[p] += wvs[p]
            r_new = r_olds[p] + r_blks[p]
            r_ref[p] = r_new
            mx = jnp.max(r_new)
            r_max = mx if r_max is None else jnp.maximum(r_max, mx)
        return r_max

    k_diag = k_ref[i]
    for p in range(n_pairs):
        z_ref[p] = _dot_nt(q2s[p], k_diag[:, lanes[p]])
    block(i, True)

    def cond(carry):
        j, done = carry
        return jnp.logical_and(j >= 0, done == 0)

    def body(carry):
        j, _ = carry
        r_max = block(j, False)
        return j - 1, (r_max < EXP_UNDERFLOW).astype(jnp.int32)

    lax.while_loop(cond, body, (i - 1, jnp.int32(0)))
    for p in range(n_pairs):
        acc = acc_ref[p]
        o_ref[:, lanes[p]] = jnp.where(_lane_masks((t_blk, LANES)), acc[:t_blk], acc[t_blk:]).astype(o_ref.dtype)


def _stick_attn_seq(q, k, v, *, t_blk):
    b, t, width = q.shape
    n_blk = t // t_blk
    n_pairs = STICK_PAIRS
    w = n_pairs * LANES
    kv_spec = pl.BlockSpec((None, n_blk, t_blk, w), lambda bi, h, i: (bi, 0, 0, h))
    return pl.pallas_call(
        functools.partial(_stick_seq_kernel, t_blk=t_blk, n_pairs=n_pairs),
        grid=(b, width // w, n_blk),
        in_specs=[
            pl.BlockSpec((t_blk, t_blk), lambda bi, h, i: (0, 0)),
            pl.BlockSpec((None, t_blk, w), lambda bi, h, i: (bi, i, h)),
            kv_spec, kv_spec,
        ],
        out_specs=pl.BlockSpec((None, t_blk, w), lambda bi, h, i: (bi, i, h)),
        out_shape=jax.ShapeDtypeStruct((b, t, width), BF16),
        scratch_shapes=[pltpu.VMEM((n_pairs, 2 * t_blk, LANES), F32), pltpu.VMEM((n_pairs, 2 * t_blk, 1), F32),
                        pltpu.VMEM((n_pairs, 2 * t_blk, t_blk), F32)],
        compiler_params=_params("parallel", "parallel", "arbitrary"),
        name="stick_attn_seq",
    )(_upper(t_blk), q, k.reshape(b, n_blk, t_blk, width), v.reshape(b, n_blk, t_blk, width))


def _band_hist_kernel(bias_ref, q_ref, kp_ref, kn_ref, vp_ref, vn_ref, o_ref, k_roll_ref, v_roll_ref):
    tq, tp = q_ref.shape[0], kp_ref.shape[0]
    for past_ref, new_ref, roll_ref in ((kp_ref, kn_ref, k_roll_ref), (vp_ref, vn_ref, v_roll_ref)):
        roll_ref[0:tp - tq, :] = past_ref[tq:tp, :]
        roll_ref[tp - tq:tp, :] = new_ref[...]
    for pair, lanes in enumerate(_lane_blocks(q_ref.shape[1])):
        kb = _history_rows(kp_ref, kn_ref, lanes)
        vb = _history_rows(vp_ref, vn_ref, lanes)
        outs = []
        for hd, qm in enumerate(_head_pair_queries(q_ref[:, lanes])):
            s = _dot_nt(qm, kb) + bias_ref[2 * pair + hd]
            p = jnp.exp2(s - jnp.max(s, axis=-1, keepdims=True))
            outs.append(_dot(p.astype(BF16), vb) / jnp.sum(p, axis=-1, keepdims=True))
        o_ref[:, lanes] = jnp.where(_lane_masks((tq, LANES)), outs[0], outs[1]).astype(o_ref.dtype)


def _band_bias(bias_table, qpos, kpos):
    tq, tk = len(qpos), len(kpos)
    n = tq + tk
    shift = np.arange(n)
    c_minus_r = np.where(shift < tk, shift, shift - n)
    rel = np.clip(int(qpos[0] - kpos[0]) - c_minus_r, -REL_CLIP, REL_CLIP) + REL_CLIP
    vec = bias_table.astype(F32)[:, rel] * LOG2E
    toeplitz = jnp.tile(vec, (1, tq))[:, :tq * (n - 1)].reshape(-1, tq, n - 1)[:, :, :tk]
    qc, kc = qpos[:, None] // CHUNK, kpos[None, :] // CHUNK
    vis = (kpos[None, :] >= 0) & (kc <= qc) & (qc - kc <= C_PAST_CHUNKS)
    return jnp.where(jnp.asarray(vis)[None], toeplitz, NEG)


def _band_attn_hist(bias, q, k_past, k_new, v_past, v_new, *, layer_row0):
    b, tq, width = q.shape
    tp = k_past.shape[1]
    assert tq <= tp, "the rolling band buffer keeps the newest Tp rows"
    roll_spec = pl.BlockSpec((None, tp, width), lambda bi: (bi, 0, 0))
    return pl.pallas_call(
        _band_hist_kernel,
        grid=(b,),
        in_specs=[pl.BlockSpec(bias.shape, lambda bi: (0, 0, 0))] + _hist_specs(tq, tp, width, layer_row0),
        out_specs=[pl.BlockSpec((None, tq, width), lambda bi: (bi, 0, 0)), roll_spec, roll_spec],
        out_shape=[jax.ShapeDtypeStruct((b, tq, width), BF16),
                   jax.ShapeDtypeStruct((b, tp, width), k_past.dtype), jax.ShapeDtypeStruct((b, tp, width), v_past.dtype)],
        compiler_params=_params("parallel"),
        name="band_attn_hist",
    )(bias, q, k_past, k_new, v_past, v_new)


def _band_seq_kernel(bias_ref, q_ref, kp_ref, kc_ref, vp_ref, vc_ref, o_ref, *, tq, n_sub, n_win):
    big = n_sub * tq
    win = n_win * tq
    half = vp_ref.shape[2]
    prev_ok = pl.program_id(2) > 0
    top = lax.broadcasted_iota(jnp.int32, (LANES, tq), 0) < HEAD_DIM

    def value_piece(ref, off):
        return ref[off // half][:, off % half:off % half + tq]

    def scores(g):
        q2 = jnp.concatenate(_head_pair_queries(q_ref[g * tq:(g + 1) * tq, :]), axis=0)
        q2t = q2.astype(F32).T.astype(BF16)
        start = big - (n_win - 1) * tq + g * tq
        n_prev = max(big - start, 0)
        k_pieces, v_pieces = [], []
        for d in range(n_win):
            off = start + d * tq
            if off < big:
                k_pieces.append(kp_ref[off:off + tq, :])
                v_pieces.append(value_piece(vp_ref, off))
            else:
                k_pieces.append(kc_ref[off - big:off - big + tq, :])
                v_pieces.append(value_piece(vc_ref, off - big))
        st = _dot(jnp.concatenate(k_pieces, axis=0), q2t) + bias_ref[...]
        if n_prev > 0:
            before_start = jnp.logical_and(lax.broadcasted_iota(jnp.int32, (win, 2 * tq), 0) < n_prev,
                                           jnp.logical_not(prev_ok))
            st = jnp.where(before_start, NEG, st)
        return st, jnp.concatenate(v_pieces, axis=1)

    def finish(g, st, vwt):
        pt = jnp.exp2(st - jnp.max(st, axis=0, keepdims=True))
        ot = _dot(vwt, pt.astype(BF16)) / jnp.sum(pt, axis=0, keepdims=True)
        o_ref[g * tq:(g + 1) * tq, :] = jnp.where(top, ot[:, :tq], ot[:, tq:]).T.astype(o_ref.dtype)

    nxt = scores(0)
    for g in range(n_sub):
        cur_scores = nxt
        if g + 1 < n_sub:
            nxt = scores(g + 1)
        finish(g, *cur_scores)


def _band_attn_seq(bias_t, q, k, vt, *, tq, n_win, n_sub):
    b, t, width = q.shape
    big = n_sub * tq
    half = vt.shape[3]
    n_half = big // half
    assert (n_win - 1) * tq <= big, "the window must fit in the previous + current block"
    assert half % tq == 0, "a window piece must not straddle two transposed value blocks"
    cur = pl.BlockSpec((None, big, LANES), lambda bi, h, i: (bi, i, h))
    prev = pl.BlockSpec((None, big, LANES), lambda bi, h, i: (bi, jnp.maximum(i - 1, 0), h))
    per_b = t // big
    vcur = pl.BlockSpec((None, n_half, LANES, half), lambda bi, h, i: (h, bi * per_b + i, 0, 0))
    vprev = pl.BlockSpec((None, n_half, LANES, half), lambda bi, h, i: (h, bi * per_b + jnp.maximum(i - 1, 0), 0, 0))
    return pl.pallas_call(
        functools.partial(_band_seq_kernel, tq=tq, n_sub=n_sub, n_win=n_win),
        grid=(b, width // LANES, t // big),
        in_specs=[pl.BlockSpec((None, n_win * tq, 2 * tq), lambda bi, h, i: (h, 0, 0)), cur, prev, cur, vprev, vcur],
        out_specs=cur,
        out_shape=jax.ShapeDtypeStruct((b, t, width), BF16),
        compiler_params=_params("parallel", "parallel", "parallel"),
        name="band_attn_seq",
    )(bias_t, q, k, k, vt, vt)


def _lambda_init(layer):
    return 0.8 - 0.6 * math.exp(-0.3 * layer)


def _tile_gain(g, n):
    return jnp.tile(g.astype(F32), n)


def _trunk(x, pos0, mem_k, mem_v, past_ab, past_c, p):
    b, t, d = x.shape
    rows = b * t
    depth = p['ffn1_g'].shape[0]
    seq = past_ab is None
    new_ab, new_c = [], []
    pos_rows = jnp.tile(pos0 + jnp.arange(t), b)
    x = x.reshape(rows, d)
    heads_per_tile = TILE_N // HEAD_DIM
    for l in range(depth):
        x = _ffn(x, p['ffn1_g'][l], p['ffn1_wg'][l], p['ffn1_wu'][l], p['ffn1_wd'][l])
        li = l // 2
        if l % 2 == 0:
            gains = jnp.stack([_tile_gain(p['a_gq'][li], heads_per_tile), _tile_gain(p['a_gk'][li], heads_per_tile)])
            state_heads = [(TILE_N // HEAD_DIM, HEAD_DIM), (TILE_N // (2 * HEAD_DIM), 2 * HEAD_DIM),
                           (TILE_N // HEAD_DIM, HEAD_DIM), (TILE_N // HEAD_DIM, HEAD_DIM)]
            if seq:
                outs = [("heads", n, w, F32) for n, w in state_heads]
                state = lambda idx: ("split", idx)
            else:
                outs = [("rows", TILE_N, F32)] * 4
                state = lambda idx: ("copy", idx, 0, 1.0)
            outs = outs + [("rows", TILE_N, BF16)] * 5
            v_dests = (state(1),)
            if seq:
                outs = outs + [("heads_t", TILE_N // LANES, BF16)]
                v_dests += (("heads_t", 9, 0),)
            tiles = [
                (0, True, (("copy", 4, 0, SOFTMAX_Q_SCALE),)),
                (1, True, (state(0), ("copy", 5, 0, 1.0))),
                (None, False, v_dests),
                (None, False, (("copy", 6, 0, STICK_Q_SCALE),)),
                (None, False, (state(2), ("copy", 7, 0, 1.0))),
                (None, False, (state(3), ("copy", 8, 0, 1.0))),
            ]
            res = _proj(x, p['mix_g'][l], p['ab_w_in'][li], gains, HEAD_DIM, tiles, outs,
                        rope_tables=_rope_tables(pos_rows), name="proj_ab")
            a_k, a_v, b_k, b_v = (r.reshape((b, t) + r.shape[1:]) for r in res[:4])
            qa, ka, qb, kb, vb = (r.reshape(b, t, TILE_N) for r in res[4:9])
            new_ab.append((a_k, a_v, b_k, b_v))
            lam_vecs = jnp.stack([p['a_lq1'][li], p['a_lk1'][li], p['a_lq2'][li], p['a_lk2'][li]]).astype(F32)
            lam_init = _lambda_init(l)
            if seq:
                t_blk = min(SEQ_BLOCK, t)
                vt = res[9].reshape(TILE_N // LANES, b, t // t_blk, LANES, t_blk)
                a_out = _diff_attn_seq(lam_vecs, p['a_subln_g'][li], qa, ka, vt, t_blk=t_blk, lam_init=lam_init)
                b_out = _stick_attn_seq(qb, kb, vb, t_blk=min(STICK_BLOCK, t))
            else:
                pka, pva, pkb, pvb = (c.reshape(-1, c.shape[2], TILE_N) for c in past_ab)
                a_out = _diff_attn_hist(lam_vecs, p['a_subln_g'][li], qa, pka, a_k, pva, a_v,
                                        layer_row0=li * b, lam_init=lam_init)
                b_out = _stick_attn_hist(qb, pkb, b_k, pvb, b_v, layer_row0=li * b)
            w_out = p['ab_w_out'][li]
            half = w_out.shape[0] // 2
            mixed, w_parts = [a_out, b_out], [w_out[:half], w_out[half:]]
        else:
            gains = jnp.stack([_tile_gain(p['c_gq'][li], heads_per_tile), _tile_gain(p['c_gk'][li], heads_per_tile)])
            groups = TILE_N // LANES
            outs = [("rows", d, F32)] * 2 + [("rows", d, BF16)] * 2
            if seq:
                outs = outs + [("heads_t", d // LANES, BF16)]
            tiles = ([(0, False, (("copy", 2, c, SOFTMAX_Q_SCALE),)) for c in (0, TILE_N)]
                     + [(1, False, (("copy", 0, c, 1.0), ("copy", 3, c, 1.0))) for c in (0, TILE_N)]
                     + [(None, False, (("copy", 1, c * TILE_N, 1.0),) + ((("heads_t", 4, c * groups),) if seq else ()))
                        for c in range(d // TILE_N)])
            res = _proj(x, p['mix_g'][l], p['c_w_in'][li], gains, HEAD_DIM, tiles, outs, name="proj_c")
            k_f, v_f, q_c, k_c = (r.reshape(b, t, d) for r in res[:4])
            band = C_PAST_CHUNKS * CHUNK
            if seq:
                tq = min(BAND_BLOCK, t)
                nb = band // tq + 1
                bias = _band_bias(p['c_bias'][li], (nb - 1) * tq + np.arange(tq), np.arange(nb * tq))
                bias_t = jnp.transpose(bias.reshape(-1, 2, tq, nb * tq), (0, 3, 1, 2)).reshape(-1, nb * tq, 2 * tq)
                o = _band_attn_seq(bias_t, q_c, k_c, res[4], tq=tq, n_win=nb, n_sub=min(BAND_GROUP, t // tq))
                keep = min(band, t)
                new_c.append((k_f[:, t - keep:], v_f[:, t - keep:]))
            else:
                pk, pv = (c.reshape(-1, c.shape[2], d) for c in past_c)
                pc = pk.shape[1]
                bias = _band_bias(p['c_bias'][li], pos0 + np.arange(t), pos0 - pc + np.arange(pc + t))
                o, k_roll, v_roll = _band_attn_hist(bias, q_c, pk, k_f, pv, v_f, layer_row0=li * b)
                new_c.append((k_roll, v_roll))
            mixed, w_parts = [o], [p['c_w_out'][li]]
        if seq:
            mk, mv, mem_row0 = mem_k[l], mem_v[l], 0
        else:
            mk, mv, mem_row0 = mem_k, mem_v, l * b
        x = _mem_attn(x.reshape(b, t, d), mixed, w_parts, p['mem_g_x'][l], p['mem_wq'][l], p['mem_gq'][l],
                      mk, mv, p['mem_wo'][l], mem_row0=mem_row0).reshape(rows, d)
        x = _ffn(x, p['ffn2_g'][l], p['ffn2_wg'][l], p['ffn2_wu'][l], p['ffn2_wd'][l])
    return x.reshape(b, t, d), new_ab, new_c


def kernel(x_prompt, x_sample, cache_a_k, cache_a_v, cache_b_k, cache_b_v, cache_c_k, cache_c_v, cache_mem_k, cache_mem_v, mem_prompt, ffn1_g, ffn1_wg, ffn1_wu, ffn1_wd, ffn2_g, ffn2_wg, ffn2_wu, ffn2_wd, mix_g, ab_w_in, ab_w_out, a_gq, a_gk, a_lq1, a_lk1, a_lq2, a_lk2, a_subln_g, c_w_in, c_w_out, c_gq, c_gk, c_bias, mem_g_x, mem_g_m, mem_wq, mem_wk, mem_wv, mem_wo, mem_gq, mem_gk):
    depth = ffn1_g.shape[0]
    p = dict(ffn1_g=ffn1_g, ffn2_g=ffn2_g, mix_g=mix_g, a_gq=a_gq, a_gk=a_gk, a_lq1=a_lq1, a_lk1=a_lk1,
             a_lq2=a_lq2, a_lk2=a_lk2, a_subln_g=a_subln_g, c_gq=c_gq, c_gk=c_gk, c_bias=c_bias,
             mem_g_x=mem_g_x, mem_gq=mem_gq)
    for name, w in (('ffn1_wg', ffn1_wg), ('ffn1_wu', ffn1_wu), ('ffn1_wd', ffn1_wd), ('ffn2_wg', ffn2_wg),
                    ('ffn2_wu', ffn2_wu), ('ffn2_wd', ffn2_wd), ('ab_w_in', ab_w_in), ('ab_w_out', ab_w_out),
                    ('c_w_in', c_w_in), ('c_w_out', c_w_out), ('mem_wq', mem_wq), ('mem_wo', mem_wo)):
        p[name] = w.astype(BF16)

    bp, tp, d = x_prompt.shape
    bs, ts, _ = x_sample.shape
    m_len = mem_prompt.shape[1]
    past_len = cache_a_k.shape[2]

    mem_k_list, mem_v_list = [], []
    n_half = d // TILE_N
    for l in range(depth):
        w_kv = jnp.concatenate([mem_wk[l], mem_wv[l]], axis=1).astype(BF16)
        gains = _tile_gain(mem_gk[l], TILE_N // MEM_HEAD_DIM)[None]
        tiles = ([(0, False, (("copy", 0, c * TILE_N, 1.0),)) for c in range(n_half)]
                 + [(None, False, (("copy", 1, c * TILE_N, 1.0),)) for c in range(n_half)])
        mk, mv = _proj(mem_prompt.reshape(bp * m_len, d), mem_g_m[l], w_kv, gains, MEM_HEAD_DIM, tiles,
                       [("rows", d, F32)] * 2, name="proj_mem_kv")
        mem_k_list.append(mk.reshape(bp, m_len, d))
        mem_v_list.append(mv.reshape(bp, m_len, d))

    y_prompt, ab_p, c_p = _trunk(x_prompt, 0, mem_k_list, mem_v_list, None, None, p)
    cache_mem_k2 = cache_mem_k.reshape(depth * bs, m_len, d)
    cache_mem_v2 = cache_mem_v.reshape(depth * bs, m_len, d)
    y_sample, ab_s, c_s = _trunk(x_sample, past_len, cache_mem_k2, cache_mem_v2,
                                 (cache_a_k, cache_a_v, cache_b_k, cache_b_v), (cache_c_k, cache_c_v), p)

    a_heads = cache_a_v.shape[3]
    b_heads = cache_b_k.shape[3]
    c_heads = cache_c_k.shape[3]

    def stack(rows, idx, heads, width):
        return jnp.stack([r[idx].reshape(r[idx].shape[0], r[idx].shape[1], heads, width) for r in rows])

    outs = [y_prompt, y_sample]
    outs += [stack(ab_p, 0, 2 * a_heads, HEAD_DIM), stack(ab_p, 1, a_heads, 2 * HEAD_DIM),
             stack(ab_p, 2, b_heads, HEAD_DIM), stack(ab_p, 3, b_heads, HEAD_DIM),
             stack(c_p, 0, c_heads, HEAD_DIM), stack(c_p, 1, c_heads, HEAD_DIM)]
    outs += [jnp.stack(mem_k_list).reshape(depth, bp, m_len, MEM_HEADS, MEM_HEAD_DIM),
             jnp.stack(mem_v_list).reshape(depth, bp, m_len, MEM_HEADS, MEM_HEAD_DIM)]
    outs += [stack(ab_s, 0, 2 * a_heads, HEAD_DIM), stack(ab_s, 1, a_heads, 2 * HEAD_DIM),
             stack(ab_s, 2, b_heads, HEAD_DIM), stack(ab_s, 3, b_heads, HEAD_DIM),
             stack(c_s, 0, c_heads, HEAD_DIM), stack(c_s, 1, c_heads, HEAD_DIM)]
    return tuple(outs)
```
